```python
import math
import jax, jax.numpy as jnp
from jax import lax
import numpy as np

D_MODEL = 1024
BATCH = 1
SEQ = 16384
DEPTH = 4

GRID_W = 64
CTX_LEN = 256
N_MIXERS = 2
N_GDN_LAYERS = (DEPTH + N_MIXERS - 1) // N_MIXERS
N_DIFF_LAYERS = DEPTH // N_MIXERS
GDN_HEAD_DIM = 128
GDN_QK_HEADS = D_MODEL // GDN_HEAD_DIM
GDN_V_HEADS = 2 * GDN_QK_HEADS
GDN_QK_DIM = GDN_QK_HEADS * GDN_HEAD_DIM
GDN_V_DIM = GDN_V_HEADS * GDN_HEAD_DIM
GDN_CONV_DIM = 2 * GDN_QK_DIM + GDN_V_DIM
GDN_IN_DIM = GDN_CONV_DIM + GDN_V_DIM + 4 * GDN_V_HEADS
GDN_CONV_W = 5
GDN_CHUNK = 64
DIFF_HEADS = D_MODEL // 128
DIFF_HEAD_DIM = 64
DIFF_V_DIM = 2 * DIFF_HEAD_DIM
ROPE_THETA = 10000.0
Q_BLOCK = 128
N_EXPERTS = 32
TOP_K = 4
D_FF = D_MODEL
SWIGLU_LIMIT = 7.0
SWIGLU_ALPHA = 1.702
MOE_BLOCK = 128
NORM_EPS = 1e-5
DEEPNORM_ALPHA = (2 * DEPTH) ** 0.25
DEEPNORM_BETA = (8 * DEPTH) ** -0.25

kernel_name = 'hybrid_gdn_diffattn_moe_dit'


def _layer_norm(x, w, b):
    xf = x.astype(jnp.float32)
    mu = jnp.mean(xf, -1, keepdims=True)
    var = jnp.mean(jnp.square(xf - mu), -1, keepdims=True)
    return ((xf - mu) * lax.rsqrt(var + NORM_EPS)).astype(x.dtype) * w + b


def _rms_norm(x, w):
    xf = x.astype(jnp.float32)
    return xf * lax.rsqrt(jnp.mean(jnp.square(xf), -1, keepdims=True) + NORM_EPS) * w


def _l2_normalize(x):
    xf = x.astype(jnp.float32)
    return xf * lax.rsqrt(jnp.sum(jnp.square(xf), -1, keepdims=True) + 1e-6)


def _centred_depthwise_conv(u, w):
    pad = GDN_CONV_W // 2
    return lax.conv_general_dilated(u, w[:, None, :].astype(u.dtype), window_strides=(1,),
                                    padding=[(pad, pad)], dimension_numbers=('NWC', 'WIO', 'NWC'),
                                    feature_group_count=u.shape[-1])


def _gdn_features(u, w_in, conv_w, a_log, dt_bias):
    B, L, _ = u.shape
    p = u @ w_in
    qkv = jax.nn.silu(_centred_depthwise_conv(p[..., :GDN_CONV_DIM], conv_w))
    z = p[..., GDN_CONV_DIM:GDN_CONV_DIM + GDN_V_DIM].reshape(B, L, GDN_V_HEADS, GDN_HEAD_DIM)
    ab = p[..., GDN_CONV_DIM + GDN_V_DIM:].astype(jnp.float32).reshape(B, L, 2, 2, GDN_V_HEADS)
    rep = GDN_V_HEADS // GDN_QK_HEADS
    q = jnp.repeat(_l2_normalize(qkv[..., :GDN_QK_DIM].reshape(B, L, GDN_QK_HEADS, GDN_HEAD_DIM)), rep, axis=2)
    k = jnp.repeat(_l2_normalize(qkv[..., GDN_QK_DIM:2 * GDN_QK_DIM].reshape(B, L, GDN_QK_HEADS, GDN_HEAD_DIM)), rep, axis=2)
    v = qkv[..., 2 * GDN_QK_DIM:].reshape(B, L, GDN_V_HEADS, GDN_HEAD_DIM)
    g = -jnp.exp(a_log.astype(jnp.float32)) * jax.nn.softplus(ab[:, :, 0] + dt_bias.astype(jnp.float32))
    beta = jax.nn.sigmoid(ab[:, :, 1])
    return q, k, v, z, g, beta


def _chunked_gated_delta(q, k, v, g, beta, s0):
    B, L, H, DK = k.shape
    DV = v.shape[-1]
    C = GDN_CHUNK
    n = L // C

    def blocks(t):
        t = t.astype(jnp.float32).reshape(B, n, C, H, *t.shape[3:])
        return jnp.moveaxis(t, (1, 3), (0, 2))

    q = blocks(q) * DK ** -0.5
    k, v = blocks(k), blocks(v)
    g = jnp.cumsum(blocks(g), axis=-1)
    beta = blocks(beta)
    incl = jnp.tril(jnp.ones((C, C), bool))
    strict = jnp.tril(jnp.ones((C, C), bool), -1)
    decay = jnp.exp(jnp.where(incl, g[..., :, None] - g[..., None, :], -jnp.inf))
    kk = jnp.einsum('nbhid,nbhjd->nbhij', k, k)
    a_mat = jnp.where(strict, beta[..., :, None] * kk * decay, 0.0) + jnp.eye(C, dtype=jnp.float32)
    rhs = jnp.concatenate([v * beta[..., None], k * (beta * jnp.exp(g))[..., None]], -1)
    sol = lax.linalg.triangular_solve(a_mat, rhs, left_side=True, lower=True, unit_diagonal=True)
    u_new, w_dec = sol[..., :DV], sol[..., DV:]
    qk = jnp.einsum('nbhid,nbhjd->nbhij', q, k) * decay
    q_dec = q * jnp.exp(g)[..., None]
    g_last = g[..., -1]
    k_dec = k * jnp.exp(g_last[..., None] - g)[..., None]

    def step(s, blk):
        qk_i, q_i, u_i, w_i, k_i, gl_i = blk
        v_new = u_i - jnp.einsum('bhck,bhkv->bhcv', w_i, s)
        o = jnp.einsum('bhck,bhkv->bhcv', q_i, s) + jnp.einsum('bhij,bhjv->bhiv', qk_i, v_new)
        s = s * jnp.exp(gl_i)[..., None, None] + jnp.einsum('bhck,bhcv->bhkv', k_i, v_new)
        return s, o

    s_fin, o = lax.scan(step, s0, (qk, q_dec, u_new, w_dec, k_dec, g_last))
    o = jnp.moveaxis(o, (0, 2), (1, 3)).reshape(B, L, H, DV)
    return s_fin, o


def _gdn_direction(fc, fl, d):
    rev = (lambda t: jnp.flip(t, axis=1)) if d == 1 else (lambda t: t)

    def run(f, s0):
        q, k, v, _, g, beta = f
        s, o = _chunked_gated_delta(rev(q), rev(k), rev(v), rev(g[:, :, d]), rev(beta[:, :, d]), s0)
        return s, rev(o)

    B = fl[0].shape[0]
    s0 = jnp.zeros((B, GDN_V_HEADS, GDN_HEAD_DIM, GDN_HEAD_DIM), jnp.float32)
    s_ctx, o_ctx = run(fc, s0)
    _, o_lat = run(fl, s_ctx)
    return o_ctx, o_lat


def _gated_deltanet(u_ctx, u_lat, w_in, conv_w, a_log, dt_bias, norm_w, w_out):
    fc = _gdn_features(u_ctx, w_in, conv_w, a_log, dt_bias)
    fl = _gdn_features(u_lat, w_in, conv_w, a_log, dt_bias)
    oc_f, ol_f = _gdn_direction(fc, fl, 0)
    oc_b, ol_b = _gdn_direction(fc, fl, 1)

    def out(o, z, u):
        B, L, _ = u.shape
        y = _rms_norm(o, norm_w) * jax.nn.silu(z.astype(jnp.float32))
        return y.astype(u.dtype).reshape(B, L, GDN_V_DIM) @ w_out

    return out(oc_f + oc_b, fc[3], u_ctx), out(ol_f + ol_b, fl[3], u_lat)


def _axial_rope(n_lat):
    rows = n_lat // GRID_W
    row = jnp.repeat(jnp.arange(rows, dtype=jnp.float32), GRID_W)
    col = jnp.tile(jnp.arange(GRID_W, dtype=jnp.float32), rows)
    axis_dim = DIFF_HEAD_DIM // 2
    inv_freq = ROPE_THETA ** (-jnp.arange(0, axis_dim, 2, dtype=jnp.float32) / axis_dim)
    ang_r = row[:, None] * inv_freq
    ang_c = col[:, None] * inv_freq
    ang = jnp.concatenate([ang_r, ang_r, ang_c, ang_c], -1)
    return jnp.cos(ang), jnp.sin(ang)


def _apply_axial_rope(x, cos, sin):
    x1, x2, x3, x4 = jnp.split(x, 4, axis=-1)
    rot = jnp.concatenate([-x2, x1, -x4, x3], -1)
    return (x * cos[None, :, None, :] + rot * sin[None, :, None, :]).astype(x.dtype)


def _diff_attention(u_ctx, u_lat, w_in, lam, subln_w, w_out, lambda_init, need_ctx_out):
    B, L_lat, D = u_lat.shape

    def qkv(u):
        L = u.shape[1]
        q, k, v = jnp.split(u @ w_in, 3, axis=-1)
        return (q.reshape(B, L, 2 * DIFF_HEADS, DIFF_HEAD_DIM), k.reshape(B, L, 2 * DIFF_HEADS, DIFF_HEAD_DIM),
                v.reshape(B, L, DIFF_HEADS, DIFF_V_DIM))

    qc, kc, vc = qkv(u_ctx)
    ql, kl, vl = qkv(u_lat)
    cos, sin = _axial_rope(L_lat)
    ql, kl = _apply_axial_rope(ql, cos, sin), _apply_axial_rope(kl, cos, sin)
    lf = lam.astype(jnp.float32)
    lam_full = jnp.exp(jnp.sum(lf[0] * lf[1])) - jnp.exp(jnp.sum(lf[2] * lf[3])) + lambda_init

    def attend(q, k, v):
        Lq, Lk = q.shape[1], k.shape[1]
        s = jnp.einsum('bqhd,bkhd->bhqk', q, k).astype(jnp.float32) * DIFF_HEAD_DIM ** -0.5
        p = jax.nn.softmax(s, axis=-1).reshape(B, DIFF_HEADS, 2, Lq, Lk)
        a = (p[:, :, 0] - lam_full * p[:, :, 1]).astype(v.dtype)
        o = jnp.einsum('bhqk,bkhe->bqhe', a, v)
        o = _rms_norm(o, subln_w) * (1.0 - lambda_init)
        return o.astype(v.dtype).reshape(B, Lq, DIFF_HEADS * DIFF_V_DIM)

    keys = jnp.concatenate([kc, kl], axis=1)
    vals = jnp.concatenate([vc, vl], axis=1)
    qb = jnp.moveaxis(ql.reshape(B, L_lat // Q_BLOCK, Q_BLOCK, 2 * DIFF_HEADS, DIFF_HEAD_DIM), 1, 0)
    ob = lax.map(lambda q_blk: attend(q_blk, keys, vals), qb)
    y_lat = jnp.moveaxis(ob, 0, 1).reshape(B, L_lat, D) @ w_out
    y_ctx = attend(qc, kc, vc) @ w_out if need_ctx_out else None
    return y_ctx, y_lat


def _moe(x2d, router_w, router_b, w_gu, b_gu, w_dn, b_dn):
    N, D = x2d.shape
    logits = (x2d @ router_w + router_b).astype(jnp.float32)
    top_val, top_idx = lax.top_k(logits, TOP_K)
    gates = jax.nn.softmax(top_val, axis=-1).astype(x2d.dtype)
    NK = N * TOP_K
    flat_e = top_idx.reshape(NK)
    order = jnp.argsort(flat_e)
    sorted_e = flat_e[order]
    counts = jnp.zeros((N_EXPERTS,), jnp.int32).at[flat_e].add(1)
    padded = (counts + MOE_BLOCK - 1) // MOE_BLOCK * MOE_BLOCK
    pad_end = jnp.cumsum(padded)
    pad_start = pad_end - padded
    start = jnp.cumsum(counts) - counts
    dest_sorted = pad_start[sorted_e] + (jnp.arange(NK, dtype=jnp.int32) - start[sorted_e])
    dest = jnp.zeros((NK,), jnp.int32).at[order].set(dest_sorted)
    n_blocks = -(-NK // MOE_BLOCK) + N_EXPERTS
    n_rows = n_blocks * MOE_BLOCK
    row_tok = jnp.full((n_rows,), N, jnp.int32).at[dest].set(jnp.arange(NK, dtype=jnp.int32) // TOP_K)
    block_e = jnp.minimum(jnp.searchsorted(pad_end, jnp.arange(n_blocks, dtype=jnp.int32) * MOE_BLOCK, side='right'),
                          N_EXPERTS - 1)
    x_pad = jnp.concatenate([x2d, jnp.zeros((1, D), x2d.dtype)], axis=0)
    xb = x_pad[row_tok].reshape(n_blocks, MOE_BLOCK, D)

    def expert_block(args):
        xe, e = args
        h = xe @ w_gu[e] + b_gu[e]
        gate = jnp.minimum(h[:, :D_FF], SWIGLU_LIMIT)
        up = jnp.clip(h[:, D_FF:], -SWIGLU_LIMIT, SWIGLU_LIMIT)
        glu = gate * jax.nn.sigmoid(gate * SWIGLU_ALPHA)
        return ((up + 1.0) * glu) @ w_dn[e] + b_dn[e]

    yb = lax.map(expert_block, (xb, block_e)).reshape(n_rows, D)
    y = yb[dest].reshape(N, TOP_K, D)
    return jnp.einsum('nk,nkd->nd', gates, y)


def setup_inputs(seed: int = 0) -> dict:
    key = jax.random.key(seed)
    ks = jax.random.split(key, 24)
    f32 = jnp.float32

    def nrm(k, shape, scale):
        return jax.random.normal(k, shape, f32) * scale

    G, K = N_GDN_LAYERS, N_DIFF_LAYERS
    dt = jnp.exp(jax.random.uniform(ks[11], (G, 2, GDN_V_HEADS), f32, math.log(1e-3), math.log(1e-1)))
    return {
        'x': nrm(ks[0], (BATCH, SEQ, D_MODEL), 1.0),
        'c': nrm(ks[1], (BATCH, D_MODEL), 1.0),
        'ctx': nrm(ks[2], (BATCH, CTX_LEN, D_MODEL), 1.0),
        'c_ctx': nrm(ks[3], (D_MODEL,), 1.0),
        'ada_w': nrm(ks[4], (DEPTH, D_MODEL, 6 * D_MODEL), 0.5 * D_MODEL ** -0.5),
        'ada_b': nrm(ks[5], (DEPTH, 6 * D_MODEL), 0.02),
        'ln_w': 1.0 + nrm(ks[6], (DEPTH, 2, D_MODEL), 0.02),
        'ln_b': nrm(ks[7], (DEPTH, 2, D_MODEL), 0.02),
        'gdn_w_in': nrm(ks[8], (G, D_MODEL, GDN_IN_DIM), D_MODEL ** -0.5),
        'gdn_conv_w': nrm(ks[9], (G, GDN_CONV_W, GDN_CONV_DIM), GDN_CONV_W ** -0.5),
        'gdn_a_log': jnp.log(jax.random.uniform(ks[10], (G, 2, GDN_V_HEADS), f32, 1.0, 16.0)),
        'gdn_dt_bias': dt + jnp.log(-jnp.expm1(-dt)),
        'gdn_norm_w': 1.0 + nrm(ks[12], (G, GDN_HEAD_DIM), 0.02),
        'gdn_w_out': nrm(ks[13], (G, GDN_V_DIM, D_MODEL), DEEPNORM_BETA * GDN_V_DIM ** -0.5),
        'diff_w_in': nrm(ks[14], (K, D_MODEL, 3 * D_MODEL), D_MODEL ** -0.5),
        'diff_lambda': nrm(ks[15], (K, 4, DIFF_HEAD_DIM), 0.1),
        'diff_subln_w': 1.0 + nrm(ks[16], (K, DIFF_V_DIM), 0.02),
        'diff_w_out': nrm(ks[17], (K, DIFF_HEADS * DIFF_V_DIM, D_MODEL), DEEPNORM_BETA * D_MODEL ** -0.5),
        'router_w': nrm(ks[18], (DEPTH, D_MODEL, N_EXPERTS), D_MODEL ** -0.5),
        'router_b': nrm(ks[19], (DEPTH, N_EXPERTS), 0.01),
        'moe_w_gate_up': nrm(ks[20], (DEPTH, N_EXPERTS, D_MODEL, 2 * D_FF), D_MODEL ** -0.5),
        'moe_b_gate_up': nrm(ks[21], (DEPTH, N_EXPERTS, 2 * D_FF), 0.02),
        'moe_w_down': nrm(ks[22], (DEPTH, N_EXPERTS, D_FF, D_MODEL), DEEPNORM_BETA * D_FF ** -0.5),
        'moe_b_down': nrm(ks[23], (DEPTH, N_EXPERTS, D_MODEL), 0.02),
    }


def reference(x, c, ctx, c_ctx, ada_w, ada_b, ln_w, ln_b, gdn_w_in, gdn_conv_w, gdn_a_log, gdn_dt_bias,
              gdn_norm_w, gdn_w_out, diff_w_in, diff_lambda, diff_subln_w, diff_w_out, router_w, router_b,
              moe_w_gate_up, moe_b_gate_up, moe_w_down, moe_b_down):
    B, L_lat, D = x.shape
    L_ctx = ctx.shape[1]
    h_lat, h_ctx = x, ctx
    s_lat = jax.nn.silu(c)[:, None, :]
    s_ctx = jax.nn.silu(c_ctx)[None, None, :]
    for i in range(DEPTH):
        last = i == DEPTH - 1
        j = i // N_MIXERS
        m_lat = jnp.split(s_lat @ ada_w[i] + ada_b[i], 6, axis=-1)
        m_ctx = jnp.split(s_ctx @ ada_w[i] + ada_b[i], 6, axis=-1)
        u_lat = h_lat * (1.0 + m_lat[1]) + m_lat[0]
        u_ctx = h_ctx * (1.0 + m_ctx[1]) + m_ctx[0]
        if i % N_MIXERS == 0:
            y_ctx, y_lat = _gated_deltanet(u_ctx, u_lat, gdn_w_in[j], gdn_conv_w[j], gdn_a_log[j], gdn_dt_bias[j],
                                           gdn_norm_w[j], gdn_w_out[j])
        else:
            lambda_init = 0.8 - 0.6 * math.exp(-0.3 * i)
            y_ctx, y_lat = _diff_attention(u_ctx, u_lat, diff_w_in[j], diff_lambda[j], diff_subln_w[j], diff_w_out[j],
                                           lambda_init, not last)
        h_lat = _layer_norm(DEEPNORM_ALPHA * h_lat + m_lat[2] * y_lat, ln_w[i, 0], ln_b[i, 0])
        v_lat = h_lat * (1.0 + m_lat[4]) + m_lat[3]
        if last:
            f_lat = _moe(v_lat.reshape(-1, D), router_w[i], router_b[i], moe_w_gate_up[i], moe_b_gate_up[i],
                         moe_w_down[i], moe_b_down[i]).reshape(B, L_lat, D)
        else:
            h_ctx = _layer_norm(DEEPNORM_ALPHA * h_ctx + m_ctx[2] * y_ctx, ln_w[i, 0], ln_b[i, 0])
            v_ctx = h_ctx * (1.0 + m_ctx[4]) + m_ctx[3]
            v_all = jnp.concatenate([v_ctx, v_lat], axis=1).reshape(-1, D)
            f_all = _moe(v_all, router_w[i], router_b[i], moe_w_gate_up[i], moe_b_gate_up[i],
                         moe_w_down[i], moe_b_down[i]).reshape(B, L_ctx + L_lat, D)
            f_ctx, f_lat = f_all[:, :L_ctx], f_all[:, L_ctx:]
            h_ctx = _layer_norm(DEEPNORM_ALPHA * h_ctx + m_ctx[5] * f_ctx, ln_w[i, 1], ln_b[i, 1])
        h_lat = _layer_norm(DEEPNORM_ALPHA * h_lat + m_lat[5] * f_lat, ln_w[i, 1], ln_b[i, 1])
    return h_lat
```

```python
import functools
import math

import jax
import jax.numpy as jnp
from jax import lax
from jax.experimental import pallas as pl
from jax.experimental.pallas import tpu as pltpu

F32 = jnp.float32
BF16 = jnp.bfloat16
HIGHEST = lax.Precision.HIGHEST

GRID_W = 64
GDN_HEAD_DIM = 128
GDN_CHUNK = 64
DIFF_HEAD_DIM = 64
ROPE_THETA = 10000.0
TOP_K = 4
SWIGLU_LIMIT = 7.0
SWIGLU_ALPHA = 1.702
NORM_EPS = 1e-5
L2_EPS = 1e-6

LANES = 128
SUBLANES = 8
HALO = 16
VMEM_LIMIT = 56 * 1024 * 1024

TM = 256
MOE_BM = 256


def _cparams(sem):
    return pltpu.CompilerParams(dimension_semantics=sem, vmem_limit_bytes=VMEM_LIMIT)


def _silu(x):
    return x * jax.nn.sigmoid(x)


def _dot(a, b):
    return jnp.dot(a, b, preferred_element_type=F32)


def _dot_nt(a, b):
    return lax.dot_general(a, b, (((1,), (1,)), ((), ())), preferred_element_type=F32)


def _modulate(h, mod_ref, shift, scale):
    return h * (1.0 + mod_ref[0, scale:scale + 1, :]) + mod_ref[0, shift:shift + 1, :]


def _layer_norm(x, w, b):
    mu = jnp.mean(x, axis=-1, keepdims=True)
    xc = x - mu
    var = jnp.mean(xc * xc, axis=-1, keepdims=True)
    return xc * lax.rsqrt(var + NORM_EPS) * w + b


def _mod_kernel(s_ref, w_ref, b_ref, o_ref):
    s = _silu(s_ref[...])
    o_ref[0] = jnp.dot(s, w_ref[0], preferred_element_type=F32, precision=HIGHEST) + b_ref[0]


def _modulations(c_ctx, c, ada_w, ada_b):
    depth, d, d6 = ada_w.shape
    tn = d6 // 4
    s = jnp.zeros((SUBLANES, d), F32).at[0].set(c_ctx).at[1].set(c[0])
    out = pl.pallas_call(
        _mod_kernel,
        grid=(depth, d6 // tn),
        in_specs=[pl.BlockSpec((SUBLANES, d), lambda i, n: (0, 0)),
                  pl.BlockSpec((1, d, tn), lambda i, n: (i, 0, n)),
                  pl.BlockSpec((1, 1, tn), lambda i, n: (i, 0, n))],
        out_specs=pl.BlockSpec((1, SUBLANES, tn), lambda i, n: (i, 0, n)),
        out_shape=jax.ShapeDtypeStruct((depth, SUBLANES, d6), F32),
        compiler_params=_cparams(("parallel", "parallel")),
        name="adaln_mod",
    )(s, ada_w, ada_b.reshape(depth, 1, d6))
    return out[:, :2].reshape(depth, 2, 6, d)


def _proj_kernel(h_ref, mod_ref, *refs, n_w, chunk):
    w_refs, o_refs = refs[:n_w], refs[n_w:]
    u = _modulate(h_ref[...], mod_ref, 0, 1).astype(BF16)
    for w_ref, o_ref in zip(w_refs, o_refs):
        n = w_ref.shape[1]
        for j in range(0, n, chunk):
            jc = min(chunk, n - j)
            o_ref[:, j:j + jc] = _dot(u, w_ref[:, j:j + jc]).astype(o_ref.dtype)


def _project(h, mod, nct, weights, out_dtypes):
    n, d = h.shape
    return pl.pallas_call(
        functools.partial(_proj_kernel, n_w=len(weights), chunk=512),
        grid=(n // TM,),
        in_specs=[pl.BlockSpec((TM, d), lambda m: (m, 0)),
                  pl.BlockSpec((1, 6, d), lambda m: (jnp.where(m < nct, 0, 1), 0, 0))]
                 + [pl.BlockSpec(w.shape, lambda m: (0, 0)) for w in weights],
        out_specs=[pl.BlockSpec((TM, w.shape[1]), lambda m: (m, 0)) for w in weights],
        out_shape=[jax.ShapeDtypeStruct((n, w.shape[1]), dt) for w, dt in zip(weights, out_dtypes)],
        compiler_params=_cparams(("parallel",)),
        name="mod_proj",
    )(h, mod, *weights)


def _qkv_rope_kernel(h_ref, mod_ref, w_ref, cos_ref, sin_ref, o_ref, *, chunk, n_rope, n_q, q_scale):
    u = _modulate(h_ref[...], mod_ref, 0, 1).astype(BF16)
    rep = chunk // LANES
    cos = jnp.concatenate([cos_ref[...]] * rep, axis=1)
    sin = jnp.concatenate([sin_ref[...]] * rep, axis=1)
    lane = lax.broadcasted_iota(jnp.int32, (h_ref.shape[0], chunk), 1)
    first = (lane % 32) < 16
    for j in range(w_ref.shape[1] // chunk):
        y = _dot(u, w_ref[:, j * chunk:(j + 1) * chunk])
        if j < n_rope:
            rot = jnp.where(first, -pltpu.roll(y, chunk - 16, 1), pltpu.roll(y, 16, 1))
            y = y * cos + rot * sin
            if j < n_q:
                y = y * q_scale
        o_ref[:, j * chunk:(j + 1) * chunk] = y.astype(o_ref.dtype)


def _qkv_rope(h, mod, nct, w, cos, sin):
    n, d = h.shape
    chunk = 512
    return pl.pallas_call(
        functools.partial(_qkv_rope_kernel, chunk=chunk, n_rope=2 * d // chunk, n_q=d // chunk,
                          q_scale=DIFF_HEAD_DIM ** -0.5),
        grid=(n // TM,),
        in_specs=[pl.BlockSpec((TM, d), lambda m: (m, 0)),
                  pl.BlockSpec((1, 6, d), lambda m: (jnp.where(m < nct, 0, 1), 0, 0)),
                  pl.BlockSpec(w.shape, lambda m: (0, 0)),
                  pl.BlockSpec((TM, LANES), lambda m: (m, 0)),
                  pl.BlockSpec((TM, LANES), lambda m: (m, 0))],
        out_specs=pl.BlockSpec((TM, w.shape[1]), lambda m: (m, 0)),
        out_shape=jax.ShapeDtypeStruct((n, w.shape[1]), BF16),
        compiler_params=_cparams(("parallel",)),
        name="qkv_rope",
    )(h, mod, w, cos, sin)


def _gdn_gates_kernel(ab_ref, alog_ref, dtb_ref, o_ref, *, chunk, n_heads):
    x = ab_ref[...]
    t = x + dtb_ref[...]
    softplus = jnp.maximum(t, 0.0) + jnp.log1p(jnp.exp(-jnp.abs(t)))
    g = -jnp.exp(alog_ref[...]) * softplus
    beta = jax.nn.sigmoid(x)
    tm = x.shape[0]
    r = lax.broadcasted_iota(jnp.int32, (tm, tm), 0)
    c = lax.broadcasted_iota(jnp.int32, (tm, tm), 1)
    same = (r // chunk) == (c // chunk)
    t_fwd = jnp.where(same & (r >= c), 1.0, 0.0).astype(F32)
    t_bwd = jnp.where(same & (r <= c), 1.0, 0.0).astype(F32)
    g_fwd = jnp.dot(t_fwd, g, preferred_element_type=F32, precision=HIGHEST)
    g_bwd = jnp.dot(t_bwd, g, preferred_element_type=F32, precision=HIGHEST)
    lane = lax.broadcasted_iota(jnp.int32, x.shape, 1)
    o_ref[...] = jnp.where(lane < n_heads, g_fwd, jnp.where(lane < 2 * n_heads, g_bwd, beta))


def _gdn_gates(ab, a_log, dt_bias):
    n = ab.shape[0]
    nh = a_log.shape[-1]
    pad = lambda v: jnp.zeros((1, LANES), F32).at[0, :2 * nh].set(v.reshape(-1))
    return pl.pallas_call(
        functools.partial(_gdn_gates_kernel, chunk=GDN_CHUNK, n_heads=nh),
        grid=(n // TM,),
        in_specs=[pl.BlockSpec((TM, LANES), lambda m: (m, 0)),
                  pl.BlockSpec((1, LANES), lambda m: (0, 0)),
                  pl.BlockSpec((1, LANES), lambda m: (0, 0))],
        out_specs=pl.BlockSpec((TM, LANES), lambda m: (m, 0)),
        out_shape=jax.ShapeDtypeStruct((n, LANES), F32),
        compiler_params=_cparams(("parallel",)),
        name="gdn_gates",
    )(ab, pad(a_log), pad(dt_bias))


def _gdn_feat_kernel(x_ref, prev_ref, next_ref, w_ref, o_ref, xs_ref, *, nct, n_tiles, n_norm, n_q, q_scale, width):
    m = pl.program_id(0)
    j = pl.program_id(1)
    tm, tn = x_ref.shape
    pad = width // 2
    prev_ok = jnp.logical_and(m != 0, m != nct)
    next_ok = jnp.logical_and(m != nct - 1, m != n_tiles - 1)
    xs_ref[0:HALO, :] = jnp.where(prev_ok, prev_ref[...].astype(F32), 0.0)
    xs_ref[HALO:HALO + tm, :] = x_ref[...].astype(F32)
    xs_ref[HALO + tm:, :] = jnp.where(next_ok, next_ref[...].astype(F32), 0.0)
    acc = jnp.zeros((tm, tn), F32)
    for t in range(width):
        acc = acc + w_ref[t:t + 1, :] * xs_ref[HALO - pad + t:HALO - pad + t + tm, :]
    y = _silu(acc)

    @pl.when(j >= n_norm)
    def _():
        o_ref[...] = y.astype(o_ref.dtype)

    @pl.when(j < n_norm)
    def _():
        scale = jnp.where(j < n_q, q_scale, 1.0).astype(F32)
        for hh in range(tn // GDN_HEAD_DIM):
            sl = slice(hh * GDN_HEAD_DIM, (hh + 1) * GDN_HEAD_DIM)
            yh = y[:, sl]
            inv = lax.rsqrt(jnp.sum(yh * yh, axis=-1, keepdims=True) + L2_EPS)
            o_ref[:, sl] = (yh * (inv * scale)).astype(o_ref.dtype)


def _gdn_features(p, conv_w, nct, qk_dim):
    n = p.shape[0]
    width, conv_dim = conv_w.shape
    tn = 512
    n_tiles = n // TM
    rb = TM // HALO
    last_rb = n // HALO - 1
    return pl.pallas_call(
        functools.partial(_gdn_feat_kernel, nct=nct, n_tiles=n_tiles, n_norm=2 * qk_dim // tn, n_q=qk_dim // tn,
                          q_scale=GDN_HEAD_DIM ** -0.5, width=width),
        grid=(n_tiles, conv_dim // tn),
        in_specs=[pl.BlockSpec((TM, tn), lambda m, j: (m, j)),
                  pl.BlockSpec((HALO, tn), lambda m, j: (jnp.maximum(m * rb - 1, 0), j)),
                  pl.BlockSpec((HALO, tn), lambda m, j: (jnp.minimum((m + 1) * rb, last_rb), j)),
                  pl.BlockSpec((width, tn), lambda m, j: (0, j))],
        out_specs=pl.BlockSpec((TM, tn), lambda m, j: (m, j)),
        out_shape=jax.ShapeDtypeStruct((n, conv_dim), BF16),
        scratch_shapes=[pltpu.VMEM((TM + 2 * HALO, tn), F32)],
        compiler_params=_cparams(("parallel", "parallel")),
        name="gdn_features",
    )(p, p, p, conv_w)


def _gdn_scan_kernel(q_ref, k_ref, kt_ref, v_ref, scol_ref, srow_ref, srowc_ref, glrow_ref, o_ref, s_ref,
                     *, reverse, chunk):
    t = pl.program_id(1)

    @pl.when(t == 0)
    def _():
        s_ref[...] = jnp.zeros_like(s_ref)

    rows = q_ref.shape[0]
    hd = GDN_HEAD_DIM
    n_chunk = rows // chunk
    q = q_ref[...]
    k = k_ref[...]
    v = v_ref[...]
    scol = scol_ref[0, 0]
    srow = srow_ref[0, 0]

    ri = lax.broadcasted_iota(jnp.int32, (rows, rows), 0)
    ci = lax.broadcasted_iota(jnp.int32, (rows, rows), 1)
    same = (ri // chunk) == (ci // chunk)
    if reverse:
        incl, strict = same & (ri <= ci), same & (ri < ci)
    else:
        incl, strict = same & (ri >= ci), same & (ri > ci)
    eye = jnp.where(ri == ci, 1.0, 0.0).astype(F32)
    r64 = lax.broadcasted_iota(jnp.int32, (chunk, chunk), 0)
    c64 = lax.broadcasted_iota(jnp.int32, (chunk, chunk), 1)
    incl64 = (r64 <= c64) if reverse else (r64 >= c64)

    kk = _dot_nt(k, k)
    n_double = int(math.log2(chunk)) - 1
    order = range(n_chunk - 1, -1, -1) if reverse else range(n_chunk)
    qk_blocks = {}
    for j in order:
        rs = slice(j * chunk, (j + 1) * chunk)
        qk_blocks[j] = _dot_nt(q[rs], k[rs])

    for a in range(2):
        gcol, bcol, glcol = scol[:, a:a + 1], scol[:, 2 + a:3 + a], scol[:, 4 + a:5 + a]
        grow, brow = srow[a:a + 1, :], srow[2 + a:3 + a, :]
        decay = jnp.exp(jnp.where(incl, gcol - grow, -jnp.inf))
        neg_l = jnp.where(strict, -(bcol * kk * decay), 0.0)
        inv = eye + neg_l
        pw = neg_l.astype(BF16)
        for _ in range(n_double):
            pw2 = _dot(pw, pw)
            pw = pw2.astype(BF16)
            inv = inv + _dot(inv.astype(BF16), pw)
        u_all = _dot((inv * brow).astype(BF16), v[:, a * hd:(a + 1) * hd])
        w_all = _dot((inv * (brow * jnp.exp(grow))).astype(BF16), k)
        e_g = jnp.exp(gcol)
        e_k = jnp.exp(glcol - gcol)
        for j in order:
            rs = slice(j * chunk, (j + 1) * chunk)
            growc = srowc_ref[0, 0, j, a:a + 1, :]
            decay_c = jnp.exp(jnp.where(incl64, gcol[rs] - growc, -jnp.inf))
            qkd = (qk_blocks[j] * decay_c).astype(BF16)
            s_old = s_ref[a]
            s_b = s_old.astype(BF16)
            v_new = u_all[rs] - _dot(w_all[rs].astype(BF16), s_b)
            o = e_g[rs] * _dot(q[rs], s_b) + _dot(qkd, v_new.astype(BF16))
            o_ref[0, rs, a * hd:(a + 1) * hd] = o.astype(o_ref.dtype)
            v_s = (v_new * e_k[rs]).astype(BF16)
            e_l = jnp.exp(glrow_ref[0, 0, j, a:a + 1, :])
            s_ref[a] = s_old * e_l + _dot(kt_ref[0, j], v_s)


def _gdn_scan(feat, kt, scol, srow, srowc, glrow, nct, reverse):
    n = feat.shape[0]
    hd = GDN_HEAD_DIM
    n_qk = kt.shape[0]
    n_tiles = n // TM
    cpt = TM // GDN_CHUNK

    def tile(t):
        if not reverse:
            return t
        return jnp.where(t < nct, nct - 1 - t, n_tiles - 1 - (t - nct))

    d = 1 if reverse else 0
    return pl.pallas_call(
        functools.partial(_gdn_scan_kernel, reverse=reverse, chunk=GDN_CHUNK),
        grid=(n_qk, n_tiles),
        in_specs=[pl.BlockSpec((TM, hd), lambda h, t: (tile(t), h)),
                  pl.BlockSpec((TM, hd), lambda h, t: (tile(t), n_qk + h)),
                  pl.BlockSpec((1, cpt, hd, GDN_CHUNK), lambda h, t: (h, tile(t), 0, 0)),
                  pl.BlockSpec((TM, 2 * hd), lambda h, t: (tile(t), n_qk + h)),
                  pl.BlockSpec((1, 1, TM, 6), lambda h, t: (d, h, tile(t), 0)),
                  pl.BlockSpec((1, 1, 4, TM), lambda h, t: (d, h, 0, tile(t))),
                  pl.BlockSpec((1, 1, cpt, 4, GDN_CHUNK), lambda h, t: (d, h, tile(t), 0, 0)),
                  pl.BlockSpec((1, 1, cpt, 2, hd), lambda h, t: (d, h, tile(t), 0, 0))],
        out_specs=pl.BlockSpec((1, TM, 2 * hd), lambda h, t: (0, tile(t), h)),
        out_shape=jax.ShapeDtypeStruct((1, n, 2 * n_qk * hd), BF16),
        scratch_shapes=[pltpu.VMEM((2, hd, hd), F32)],
        compiler_params=_cparams(("parallel", "arbitrary")),
        name="gdn_scan_bwd" if reverse else "gdn_scan_fwd",
    )(feat, feat, kt, feat, scol, srow, srowc, glrow)[0]


def _post_mixer(y, h_ref, mod_ref, lnw_ref, lnb_ref, rw_ref, rb_ref, h1_ref, v_ref, lg_ref, alpha):
    x = alpha * h_ref[...] + mod_ref[0, 2:3, :] * y
    h1 = _layer_norm(x, lnw_ref[...], lnb_ref[...])
    h1_ref[...] = h1
    v = _modulate(h1, mod_ref, 3, 4)
    v_ref[...] = v.astype(v_ref.dtype)
    lg_ref[...] = jnp.dot(v, rw_ref[...], preferred_element_type=F32, precision=HIGHEST) + rb_ref[...]


def _gdn_out_kernel(of_ref, ob_ref, z_ref, nw_ref, w_ref, h_ref, mod_ref, lnw_ref, lnb_ref, rw_ref, rb_ref,
                    h1_ref, v_ref, lg_ref, a_ref, *, alpha):
    hd = GDN_HEAD_DIM
    for hh in range(of_ref.shape[1] // hd):
        sl = slice(hh * hd, (hh + 1) * hd)
        o = of_ref[:, sl].astype(F32) + ob_ref[:, sl].astype(F32)
        o = o * lax.rsqrt(jnp.mean(o * o, axis=-1, keepdims=True) + NORM_EPS) * nw_ref[...]
        a_ref[:, sl] = (o * _silu(z_ref[:, sl].astype(F32))).astype(BF16)
    y = _dot(a_ref[...], w_ref[...])
    _post_mixer(y, h_ref, mod_ref, lnw_ref, lnb_ref, rw_ref, rb_ref, h1_ref, v_ref, lg_ref, alpha)


def _attn_out_kernel(a_ref, w_ref, h_ref, mod_ref, lnw_ref, lnb_ref, rw_ref, rb_ref, h1_ref, v_ref, lg_ref, *, alpha):
    y = _dot(a_ref[...], w_ref[...])
    _post_mixer(y, h_ref, mod_ref, lnw_ref, lnb_ref, rw_ref, rb_ref, h1_ref, v_ref, lg_ref, alpha)


def _mixer_out(kind, acts, w_out, h, mod, nct, ln_w, ln_b, router_w, router_b, alpha):
    n, d = h.shape
    row = lambda width: pl.BlockSpec((TM, width), lambda m: (m, 0))
    full = lambda arr: pl.BlockSpec(arr.shape, lambda m: (0,) * arr.ndim)
    n_e = router_w.shape[1]
    rw = jnp.zeros((d, LANES), F32).at[:, :n_e].set(router_w)
    rb = jnp.zeros((1, LANES), F32).at[0, :n_e].set(router_b)
    tail = [w_out, h, mod, ln_w.reshape(1, d), ln_b.reshape(1, d), rw, rb]
    tail_specs = [full(w_out), row(d), pl.BlockSpec((1, 6, d), lambda m: (jnp.where(m < nct, 0, 1), 0, 0)),
                  pl.BlockSpec((1, d), lambda m: (0, 0)), pl.BlockSpec((1, d), lambda m: (0, 0)), full(rw), full(rb)]
    if kind == "gdn":
        o_f, o_b, p, norm_w, z_col = acts
        v_dim = o_f.shape[1]
        body = functools.partial(_gdn_out_kernel, alpha=alpha)
        args = [o_f, o_b, p, norm_w.reshape(1, -1)] + tail
        specs = [row(v_dim), row(v_dim), pl.BlockSpec((TM, v_dim), lambda m: (m, z_col)),
                 pl.BlockSpec((1, norm_w.shape[0]), lambda m: (0, 0))] + tail_specs
        scratch = [pltpu.VMEM((TM, v_dim), BF16)]
    else:
        (a,) = acts
        body = functools.partial(_attn_out_kernel, alpha=alpha)
        args = [a] + tail
        specs = [row(a.shape[1])] + tail_specs
        scratch = []
    return pl.pallas_call(
        body,
        grid=(n // TM,),
        in_specs=specs,
        out_specs=[row(d), row(d), row(LANES)],
        out_shape=[jax.ShapeDtypeStruct((n, d), F32), jax.ShapeDtypeStruct((n, d), BF16),
                   jax.ShapeDtypeStruct((n, LANES), F32)],
        scratch_shapes=scratch,
        compiler_params=_cparams(("parallel",)),
        name=kind + "_out_ln",
    )(*args)


def _attn_kernel(q_ref, k_ref, v_ref, lam_ref, sw_ref, o_ref, acc_ref, m_ref, l_ref,
                 *, tk, nct, n_kv_ctx, n_kv_all, lambda_init):
    mt = pl.program_id(1)
    tq = q_ref.shape[0]
    q = q_ref[...]
    lane = lax.broadcasted_iota(jnp.int32, q.shape, 1)
    zero = jnp.zeros_like(q)
    q_maps = (jnp.where(lane < DIFF_HEAD_DIM, q, zero), jnp.where(lane >= DIFF_HEAD_DIM, q, zero))
    acc_ref[...] = jnp.zeros_like(acc_ref)
    m_ref[...] = jnp.full_like(m_ref, -jnp.inf)
    l_ref[...] = jnp.zeros_like(l_ref)

    def body(j, carry):
        start = pl.multiple_of(j * tk, tk)
        kj = k_ref[pl.ds(start, tk), :]
        vj = v_ref[pl.ds(start, tk), :]
        for i, qm in enumerate(q_maps):
            s = _dot_nt(qm, kj)
            m_old = m_ref[i]
            m_new = jnp.maximum(m_old, jnp.max(s, axis=-1, keepdims=True))
            alpha = jnp.exp(m_old - m_new)
            p = jnp.exp(s - m_new)
            l_ref[i] = alpha * l_ref[i] + jnp.sum(p, axis=-1, keepdims=True)
            acc_ref[i] = alpha * acc_ref[i] + _dot(p.astype(BF16), vj)
            m_ref[i] = m_new
        return carry

    lax.fori_loop(0, jnp.where(mt < nct, n_kv_ctx, n_kv_all), body, 0)

    lam = lam_ref[...]
    lam_full = (jnp.exp(jnp.sum(lam[0:1] * lam[1:2], axis=-1, keepdims=True))
                - jnp.exp(jnp.sum(lam[2:3] * lam[3:4], axis=-1, keepdims=True)) + lambda_init)
    o = acc_ref[0] / l_ref[0] - lam_full * (acc_ref[1] / l_ref[1])
    o = o * lax.rsqrt(jnp.mean(o * o, axis=-1, keepdims=True) + NORM_EPS) * sw_ref[...]
    o_ref[...] = (o * (1.0 - lambda_init)).astype(o_ref.dtype)


def _diff_attention(qkv, lam, subln_w, nct, l_ctx, lambda_init):
    n = qkv.shape[0]
    d = qkv.shape[1] // 3
    vd = subln_w.shape[0]
    n_heads = d // vd
    tk = TM
    return pl.pallas_call(
        functools.partial(_attn_kernel, tk=tk, nct=nct, n_kv_ctx=l_ctx // tk, n_kv_all=n // tk,
                          lambda_init=lambda_init),
        grid=(n_heads, n // TM),
        in_specs=[pl.BlockSpec((TM, vd), lambda h, m: (m, h)),
                  pl.BlockSpec((n, vd), lambda h, m: (0, n_heads + h)),
                  pl.BlockSpec((n, vd), lambda h, m: (0, 2 * n_heads + h)),
                  pl.BlockSpec(lam.shape, lambda h, m: (0, 0)),
                  pl.BlockSpec((1, vd), lambda h, m: (0, 0))],
        out_specs=pl.BlockSpec((TM, vd), lambda h, m: (m, h)),
        out_shape=jax.ShapeDtypeStruct((n, d), BF16),
        scratch_shapes=[pltpu.VMEM((2, TM, vd), F32), pltpu.VMEM((2, TM, 1), F32), pltpu.VMEM((2, TM, 1), F32)],
        compiler_params=_cparams(("parallel", "parallel")),
        name="diff_attn",
    )(qkv, qkv, qkv, lam, subln_w.reshape(1, vd))


def _moe_kernel(be_ref, nb_ref, x_ref, wgu_ref, bgu_ref, wdn_ref, bdn_ref, y_ref, wgu_s, wdn_s, *, d_ff):
    b = pl.program_id(0)
    changed = jnp.logical_or(b == 0, be_ref[b] != be_ref[jnp.maximum(b - 1, 0)])

    @pl.when(changed)
    def _():
        wgu_s[...] = wgu_ref[0].astype(BF16)
        wdn_s[...] = wdn_ref[0].astype(BF16)

    @pl.when(b < nb_ref[0])
    def _():
        h = _dot(x_ref[...], wgu_s[...]) + bgu_ref[0]
        gate = jnp.minimum(h[:, :d_ff], SWIGLU_LIMIT)
        up = jnp.clip(h[:, d_ff:], -SWIGLU_LIMIT, SWIGLU_LIMIT)
        glu = gate * jax.nn.sigmoid(gate * SWIGLU_ALPHA)
        act = ((up + 1.0) * glu).astype(BF16)
        y_ref[...] = (_dot(act, wdn_s[...]) + bdn_ref[0]).astype(y_ref.dtype)

    @pl.when(b >= nb_ref[0])
    def _():
        y_ref[...] = jnp.zeros_like(y_ref)


def _moe_experts(x_sorted, block_e, n_used, w_gu, b_gu, w_dn, b_dn):
    n_rows, d = x_sorted.shape
    n_e, _, f2 = w_gu.shape
    grid_spec = pltpu.PrefetchScalarGridSpec(
        num_scalar_prefetch=2,
        grid=(n_rows // MOE_BM,),
        in_specs=[pl.BlockSpec((MOE_BM, d), lambda b, be, nb: (b, 0)),
                  pl.BlockSpec((1, d, f2), lambda b, be, nb: (be[b], 0, 0)),
                  pl.BlockSpec((1, 1, f2), lambda b, be, nb: (be[b], 0, 0)),
                  pl.BlockSpec((1, f2 // 2, d), lambda b, be, nb: (be[b], 0, 0)),
                  pl.BlockSpec((1, 1, d), lambda b, be, nb: (be[b], 0, 0))],
        out_specs=pl.BlockSpec((MOE_BM, d), lambda b, be, nb: (b, 0)),
        scratch_shapes=[pltpu.VMEM((d, f2), BF16), pltpu.VMEM((f2 // 2, d), BF16)],
    )
    return pl.pallas_call(
        functools.partial(_moe_kernel, d_ff=f2 // 2),
        grid_spec=grid_spec,
        out_shape=jax.ShapeDtypeStruct((n_rows, d), BF16),
        compiler_params=_cparams(("arbitrary",)),
        name="moe_experts",
    )(block_e, n_used, x_sorted, w_gu, b_gu.reshape(n_e, 1, f2), w_dn, b_dn.reshape(n_e, 1, d))


def _moe_combine_kernel(y0_ref, y1_ref, y2_ref, y3_ref, g_ref, h_ref, mod_ref, lnw_ref, lnb_ref, o_ref, *, alpha):
    g = g_ref[...]
    f = jnp.zeros(h_ref.shape, F32)
    for i, y_ref in enumerate((y0_ref, y1_ref, y2_ref, y3_ref)):
        f = f + g[:, i:i + 1] * y_ref[...].astype(F32)
    x = alpha * h_ref[...] + mod_ref[0, 5:6, :] * f
    o_ref[...] = _layer_norm(x, lnw_ref[...], lnb_ref[...])


def _moe_combine(ys, gates, h1, mod, nct, ln_w, ln_b, alpha):
    n, d = h1.shape
    row = lambda width: pl.BlockSpec((TM, width), lambda m: (m, 0))
    return pl.pallas_call(
        functools.partial(_moe_combine_kernel, alpha=alpha),
        grid=(n // TM,),
        in_specs=[row(d)] * TOP_K + [row(TOP_K), row(d),
                                     pl.BlockSpec((1, 6, d), lambda m: (jnp.where(m < nct, 0, 1), 0, 0)),
                                     pl.BlockSpec((1, d), lambda m: (0, 0)), pl.BlockSpec((1, d), lambda m: (0, 0))],
        out_specs=row(d),
        out_shape=jax.ShapeDtypeStruct((n, d), F32),
        compiler_params=_cparams(("parallel",)),
        name="moe_combine_ln",
    )(*ys, gates, h1, mod, ln_w.reshape(1, d), ln_b.reshape(1, d))


def _moe(v_bf16, logits, n_e, h1, mod, nct, ln_w, ln_b, alpha, w_gu, b_gu, w_dn, b_dn):
    n, d = v_bf16.shape
    top_val, top_idx = lax.top_k(logits[:, :n_e], TOP_K)
    gates = jax.nn.softmax(top_val, axis=-1)
    nk = n * TOP_K
    flat_e = top_idx.reshape(nk)
    order = jnp.argsort(flat_e)
    sorted_e = flat_e[order]
    counts = jnp.zeros((n_e,), jnp.int32).at[flat_e].add(1)
    padded = (counts + MOE_BM - 1) // MOE_BM * MOE_BM
    pad_end = jnp.cumsum(padded)
    pad_start = pad_end - padded
    start = jnp.cumsum(counts) - counts
    dest_sorted = pad_start[sorted_e] + (jnp.arange(nk, dtype=jnp.int32) - start[sorted_e])
    dest = jnp.zeros((nk,), jnp.int32).at[order].set(dest_sorted)
    n_blocks = -(-nk // MOE_BM) + n_e
    n_rows = n_blocks * MOE_BM
    row_tok = jnp.full((n_rows,), n, jnp.int32).at[dest].set(jnp.arange(nk, dtype=jnp.int32) // TOP_K)
    block_e = jnp.minimum(jnp.searchsorted(pad_end, jnp.arange(n_blocks, dtype=jnp.int32) * MOE_BM, side='right'),
                          n_e - 1).astype(jnp.int32)
    n_used = (pad_end[-1:] // MOE_BM).astype(jnp.int32)
    x_pad = jnp.concatenate([v_bf16, jnp.zeros((1, d), v_bf16.dtype)], axis=0)
    x_sorted = x_pad[row_tok]
    yb = _moe_experts(x_sorted, block_e, n_used, w_gu, b_gu, w_dn, b_dn)
    dest2 = dest.reshape(n, TOP_K)
    ys = [yb[dest2[:, i]] for i in range(TOP_K)]
    return _moe_combine(ys, gates, h1, mod, nct, ln_w, ln_b, alpha)


def _rope_tables(l_ctx, l_lat):
    rows = l_lat // GRID_W
    row = jnp.repeat(jnp.arange(rows, dtype=F32), GRID_W)
    col = jnp.tile(jnp.arange(GRID_W, dtype=F32), rows)
    axis_dim = DIFF_HEAD_DIM // 2
    inv_freq = ROPE_THETA ** (-jnp.arange(0, axis_dim, 2, dtype=F32) / axis_dim)
    ang_r = row[:, None] * inv_freq
    ang_c = col[:, None] * inv_freq
    ang = jnp.concatenate([ang_r, ang_r, ang_c, ang_c], -1)
    ang = jnp.concatenate([jnp.zeros((l_ctx, DIFF_HEAD_DIM), F32), ang], 0)
    ang = jnp.concatenate([ang, ang], -1)
    return jnp.cos(ang), jnp.sin(ang)


def _gdn_scalars(gates, n_vh):
    n = gates.shape[0]
    n_qk = n_vh // 2
    nc = n // GDN_CHUNK
    gc = gates[:, :2 * n_vh].reshape(n, 2, n_qk, 2)
    beta = gates[:, 2 * n_vh:4 * n_vh].reshape(n, 2, n_qk, 2)
    gcc = gc.reshape(nc, GDN_CHUNK, 2, n_qk, 2)
    gl = jnp.stack([gcc[:, -1, 0], gcc[:, 0, 1]], axis=1)
    gl_tok = jnp.repeat(gl, GDN_CHUNK, axis=0)
    col = jnp.concatenate([gc, beta, gl_tok], axis=-1)
    scol = jnp.transpose(col, (1, 2, 0, 3))
    srow = jnp.transpose(col[..., :4], (1, 2, 3, 0))
    srowc = jnp.transpose(col[..., :4].reshape(nc, GDN_CHUNK, 2, n_qk, 4), (2, 3, 0, 4, 1))
    glrow = jnp.broadcast_to(jnp.transpose(gl, (1, 2, 0, 3))[..., None], (2, n_qk, nc, 2, GDN_HEAD_DIM))
    return scol, srow, srowc, glrow


def kernel(x, c, ctx, c_ctx, ada_w, ada_b, ln_w, ln_b, gdn_w_in, gdn_conv_w, gdn_a_log, gdn_dt_bias, gdn_norm_w,
           gdn_w_out, diff_w_in, diff_lambda, diff_subln_w, diff_w_out, router_w, router_b, moe_w_gate_up,
           moe_b_gate_up, moe_w_down, moe_b_down):
    batch, l_lat, d = x.shape
    l_ctx = ctx.shape[1]
    depth = ada_w.shape[0]
    assert batch == 1 and l_ctx % TM == 0 and l_lat % TM == 0 and l_lat % GRID_W == 0
    nct = l_ctx // TM
    n_e = router_w.shape[-1]
    alpha = (2 * depth) ** 0.25
    n_vh = gdn_a_log.shape[-1]
    v_dim = n_vh * GDN_HEAD_DIM
    conv_dim = gdn_conv_w.shape[-1]
    qk_dim = (conv_dim - v_dim) // 2
    n_qk = qk_dim // GDN_HEAD_DIM

    h = jnp.concatenate([ctx[0], x[0]], axis=0)
    n = h.shape[0]
    mods = _modulations(c_ctx, c, ada_w, ada_b)
    cos, sin = _rope_tables(l_ctx, l_lat)

    for i in range(depth):
        j = i // 2
        mod = mods[i]
        if i % 2 == 0:
            w_in = gdn_w_in[j]
            w_main = w_in[:, :conv_dim + v_dim].astype(BF16)
            w_ab = jnp.zeros((d, LANES), F32).at[:, :4 * n_vh].set(w_in[:, conv_dim + v_dim:]).astype(BF16)
            p, ab = _project(h, mod, nct, [w_main, w_ab], [BF16, F32])
            gates = _gdn_gates(ab, gdn_a_log[j], gdn_dt_bias[j])
            feat = _gdn_features(p, gdn_conv_w[j], nct, qk_dim)
            kt = jnp.transpose(feat[:, qk_dim:2 * qk_dim].reshape(n // GDN_CHUNK, GDN_CHUNK, n_qk, GDN_HEAD_DIM),
                               (2, 0, 3, 1))
            scol, srow, srowc, glrow = _gdn_scalars(gates, n_vh)
            o_f = _gdn_scan(feat, kt, scol, srow, srowc, glrow, nct, False)
            o_b = _gdn_scan(feat, kt, scol, srow, srowc, glrow, nct, True)
            acts = (o_f, o_b, p, gdn_norm_w[j], conv_dim // v_dim)
            h1, v, logits = _mixer_out("gdn", acts, gdn_w_out[j].astype(BF16), h, mod, nct, ln_w[i, 0], ln_b[i, 0],
                                       router_w[i], router_b[i], alpha)
        else:
            lambda_init = 0.8 - 0.6 * math.exp(-0.3 * i)
            qkv = _qkv_rope(h, mod, nct, diff_w_in[j].astype(BF16), cos, sin)
            a = _diff_attention(qkv, diff_lambda[j], diff_subln_w[j], nct, l_ctx, lambda_init)
            h1, v, logits = _mixer_out("attn", (a,), diff_w_out[j].astype(BF16), h, mod, nct, ln_w[i, 0], ln_b[i, 0],
                                       router_w[i], router_b[i], alpha)
        h = _moe(v, logits, n_e, h1, mod, nct, ln_w[i, 1], ln_b[i, 1], alpha,
                 moe_w_gate_up[i], moe_b_gate_up[i], moe_w_down[i], moe_b_down[i])
    return h[l_ctx:].reshape(batch, l_lat, d)
```

```python
import functools
import math

import jax
import jax.numpy as jnp
from jax import lax
from jax.experimental import pallas as pl
from jax.experimental.pallas import tpu as pltpu

F32 = jnp.float32
BF16 = jnp.bfloat16
HIGHEST = lax.Precision.HIGHEST

GRID_W = 64
GDN_HEAD_DIM = 128
GDN_CHUNK = 64
DIFF_HEAD_DIM = 64
ROPE_THETA = 10000.0
TOP_K = 4
SWIGLU_LIMIT = 7.0
SWIGLU_ALPHA = 1.702
NORM_EPS = 1e-5
L2_EPS = 1e-6

LANES = 128
SUBLANES = 8
HALO = 16
VMEM_LIMIT = 56 * 1024 * 1024

TM = 256
MOE_BM = 256
ATTN_TK = 1024


def _cparams(sem):
    return pltpu.CompilerParams(dimension_semantics=sem, vmem_limit_bytes=VMEM_LIMIT)


def _silu(x):
    return x * jax.nn.sigmoid(x)


def _dot(a, b):
    return jnp.dot(a, b, preferred_element_type=F32)


def _dot_nt(a, b):
    return lax.dot_general(a, b, (((1,), (1,)), ((), ())), preferred_element_type=F32)


def _modulate(h, mod_ref, shift, scale):
    return h * (1.0 + mod_ref[0, scale:scale + 1, :]) + mod_ref[0, shift:shift + 1, :]


def _layer_norm(x, w, b):
    mu = jnp.mean(x, axis=-1, keepdims=True)
    xc = x - mu
    var = jnp.mean(xc * xc, axis=-1, keepdims=True)
    return xc * lax.rsqrt(var + NORM_EPS) * w + b


def _mod_kernel(s_ref, w_ref, b_ref, o_ref):
    s = _silu(s_ref[...])
    o_ref[0] = jnp.dot(s, w_ref[0], preferred_element_type=F32, precision=HIGHEST) + b_ref[0]


def _modulations(c_ctx, c, ada_w, ada_b):
    depth, d, d6 = ada_w.shape
    tn = d6 // 4
    s = jnp.zeros((SUBLANES, d), F32).at[0].set(c_ctx).at[1].set(c[0])
    out = pl.pallas_call(
        _mod_kernel,
        grid=(depth, d6 // tn),
        in_specs=[pl.BlockSpec((SUBLANES, d), lambda i, n: (0, 0)),
                  pl.BlockSpec((1, d, tn), lambda i, n: (i, 0, n)),
                  pl.BlockSpec((1, 1, tn), lambda i, n: (i, 0, n))],
        out_specs=pl.BlockSpec((1, SUBLANES, tn), lambda i, n: (i, 0, n)),
        out_shape=jax.ShapeDtypeStruct((depth, SUBLANES, d6), F32),
        compiler_params=_cparams(("parallel", "parallel")),
        name="adaln_mod",
    )(s, ada_w, ada_b.reshape(depth, 1, d6))
    return out[:, :2].reshape(depth, 2, 6, d)


def _proj_kernel(h_ref, mod_ref, *refs, n_w, chunk):
    w_refs, o_refs = refs[:n_w], refs[n_w:]
    u = _modulate(h_ref[...], mod_ref, 0, 1).astype(BF16)
    for w_ref, o_ref in zip(w_refs, o_refs):
        n = w_ref.shape[1]
        for j in range(0, n, chunk):
            jc = min(chunk, n - j)
            o_ref[:, j:j + jc] = _dot(u, w_ref[:, j:j + jc]).astype(o_ref.dtype)


def _project(h, mod, nct, weights, out_dtypes):
    n, d = h.shape
    return pl.pallas_call(
        functools.partial(_proj_kernel, n_w=len(weights), chunk=512),
        grid=(n // TM,),
        in_specs=[pl.BlockSpec((TM, d), lambda m: (m, 0)),
                  pl.BlockSpec((1, 6, d), lambda m: (jnp.where(m < nct, 0, 1), 0, 0))]
                 + [pl.BlockSpec(w.shape, lambda m: (0, 0)) for w in weights],
        out_specs=[pl.BlockSpec((TM, w.shape[1]), lambda m: (m, 0)) for w in weights],
        out_shape=[jax.ShapeDtypeStruct((n, w.shape[1]), dt) for w, dt in zip(weights, out_dtypes)],
        compiler_params=_cparams(("parallel",)),
        name="mod_proj",
    )(h, mod, *weights)


def _qkv_rope_kernel(h_ref, mod_ref, w_ref, cos_ref, sin_ref, o_ref, *, chunk, n_rope, n_q, q_scale):
    u = _modulate(h_ref[...], mod_ref, 0, 1).astype(BF16)
    rep = chunk // LANES
    cos = jnp.concatenate([cos_ref[...]] * rep, axis=1)
    sin = jnp.concatenate([sin_ref[...]] * rep, axis=1)
    lane = lax.broadcasted_iota(jnp.int32, (h_ref.shape[0], chunk), 1)
    first = (lane % 32) < 16
    for j in range(w_ref.shape[1] // chunk):
        y = _dot(u, w_ref[:, j * chunk:(j + 1) * chunk])
        if j < n_rope:
            rot = jnp.where(first, -pltpu.roll(y, chunk - 16, 1), pltpu.roll(y, 16, 1))
            y = y * cos + rot * sin
            if j < n_q:
                y = y * q_scale
        o_ref[:, j * chunk:(j + 1) * chunk] = y.astype(o_ref.dtype)


def _qkv_rope(h, mod, nct, w, cos, sin):
    n, d = h.shape
    chunk = 512
    return pl.pallas_call(
        functools.partial(_qkv_rope_kernel, chunk=chunk, n_rope=2 * d // chunk, n_q=d // chunk,
                          q_scale=DIFF_HEAD_DIM ** -0.5 * math.log2(math.e)),
        grid=(n // TM,),
        in_specs=[pl.BlockSpec((TM, d), lambda m: (m, 0)),
                  pl.BlockSpec((1, 6, d), lambda m: (jnp.where(m < nct, 0, 1), 0, 0)),
                  pl.BlockSpec(w.shape, lambda m: (0, 0)),
                  pl.BlockSpec((TM, LANES), lambda m: (m, 0)),
                  pl.BlockSpec((TM, LANES), lambda m: (m, 0))],
        out_specs=pl.BlockSpec((TM, w.shape[1]), lambda m: (m, 0)),
        out_shape=jax.ShapeDtypeStruct((n, w.shape[1]), BF16),
        compiler_params=_cparams(("parallel",)),
        name="qkv_rope",
    )(h, mod, w, cos, sin)


def _gdn_gates_kernel(ab_ref, alog_ref, dtb_ref, o_ref, *, chunk, n_heads):
    x = ab_ref[...]
    t = x + dtb_ref[...]
    softplus = jnp.maximum(t, 0.0) + jnp.log1p(jnp.exp(-jnp.abs(t)))
    g = -jnp.exp(alog_ref[...]) * softplus
    beta = jax.nn.sigmoid(x)
    tm = x.shape[0]
    r = lax.broadcasted_iota(jnp.int32, (tm, tm), 0)
    c = lax.broadcasted_iota(jnp.int32, (tm, tm), 1)
    same = (r // chunk) == (c // chunk)
    t_fwd = jnp.where(same & (r >= c), 1.0, 0.0).astype(F32)
    t_bwd = jnp.where(same & (r <= c), 1.0, 0.0).astype(F32)
    g_fwd = jnp.dot(t_fwd, g, preferred_element_type=F32, precision=HIGHEST)
    g_bwd = jnp.dot(t_bwd, g, preferred_element_type=F32, precision=HIGHEST)
    lane = lax.broadcasted_iota(jnp.int32, x.shape, 1)
    o_ref[...] = jnp.where(lane < n_heads, g_fwd, jnp.where(lane < 2 * n_heads, g_bwd, beta))


def _gdn_gates(ab, a_log, dt_bias):
    n = ab.shape[0]
    nh = a_log.shape[-1]
    pad = lambda v: jnp.zeros((1, LANES), F32).at[0, :2 * nh].set(v.reshape(-1))
    return pl.pallas_call(
        functools.partial(_gdn_gates_kernel, chunk=GDN_CHUNK, n_heads=nh),
        grid=(n // TM,),
        in_specs=[pl.BlockSpec((TM, LANES), lambda m: (m, 0)),
                  pl.BlockSpec((1, LANES), lambda m: (0, 0)),
                  pl.BlockSpec((1, LANES), lambda m: (0, 0))],
        out_specs=pl.BlockSpec((TM, LANES), lambda m: (m, 0)),
        out_shape=jax.ShapeDtypeStruct((n, LANES), F32),
        compiler_params=_cparams(("parallel",)),
        name="gdn_gates",
    )(ab, pad(a_log), pad(dt_bias))


def _gdn_feat_kernel(x_ref, prev_ref, next_ref, w_ref, o_ref, xs_ref, *, nct, n_tiles, n_norm, n_q, q_scale, width):
    m = pl.program_id(0)
    j = pl.program_id(1)
    tm, tn = x_ref.shape
    pad = width // 2
    prev_ok = jnp.logical_and(m != 0, m != nct)
    next_ok = jnp.logical_and(m != nct - 1, m != n_tiles - 1)
    xs_ref[0:HALO, :] = jnp.where(prev_ok, prev_ref[...].astype(F32), 0.0)
    xs_ref[HALO:HALO + tm, :] = x_ref[...].astype(F32)
    xs_ref[HALO + tm:, :] = jnp.where(next_ok, next_ref[...].astype(F32), 0.0)
    acc = jnp.zeros((tm, tn), F32)
    for t in range(width):
        acc = acc + w_ref[t:t + 1, :] * xs_ref[HALO - pad + t:HALO - pad + t + tm, :]
    y = _silu(acc)

    @pl.when(j >= n_norm)
    def _():
        o_ref[...] = y.astype(o_ref.dtype)

    @pl.when(j < n_norm)
    def _():
        scale = jnp.where(j < n_q, q_scale, 1.0).astype(F32)
        for hh in range(tn // GDN_HEAD_DIM):
            sl = slice(hh * GDN_HEAD_DIM, (hh + 1) * GDN_HEAD_DIM)
            yh = y[:, sl]
            inv = lax.rsqrt(jnp.sum(yh * yh, axis=-1, keepdims=True) + L2_EPS)
            o_ref[:, sl] = (yh * (inv * scale)).astype(o_ref.dtype)


def _gdn_features(p, conv_w, nct, qk_dim):
    n = p.shape[0]
    width, conv_dim = conv_w.shape
    tn = 512
    n_tiles = n // TM
    rb = TM // HALO
    last_rb = n // HALO - 1
    return pl.pallas_call(
        functools.partial(_gdn_feat_kernel, nct=nct, n_tiles=n_tiles, n_norm=2 * qk_dim // tn, n_q=qk_dim // tn,
                          q_scale=GDN_HEAD_DIM ** -0.5, width=width),
        grid=(n_tiles, conv_dim // tn),
        in_specs=[pl.BlockSpec((TM, tn), lambda m, j: (m, j)),
                  pl.BlockSpec((HALO, tn), lambda m, j: (jnp.maximum(m * rb - 1, 0), j)),
                  pl.BlockSpec((HALO, tn), lambda m, j: (jnp.minimum((m + 1) * rb, last_rb), j)),
                  pl.BlockSpec((width, tn), lambda m, j: (0, j))],
        out_specs=pl.BlockSpec((TM, tn), lambda m, j: (m, j)),
        out_shape=jax.ShapeDtypeStruct((n, conv_dim), BF16),
        scratch_shapes=[pltpu.VMEM((TM + 2 * HALO, tn), F32)],
        compiler_params=_cparams(("parallel", "parallel")),
        name="gdn_features",
    )(p, p, p, conv_w)


def _gdn_scan_kernel(q_ref, k_ref, kt_ref, v_ref, scol_ref, srow_ref, srowc_ref, glrow_ref, o_ref, s_ref,
                     *, reverse, chunk):
    t = pl.program_id(1)

    @pl.when(t == 0)
    def _():
        s_ref[...] = jnp.zeros_like(s_ref)

    rows = q_ref.shape[0]
    hd = GDN_HEAD_DIM
    n_chunk = rows // chunk
    q = q_ref[...]
    k = k_ref[...]
    v = v_ref[...]
    scol = scol_ref[0, 0]
    srow = srow_ref[0, 0]

    ri = lax.broadcasted_iota(jnp.int32, (rows, rows), 0)
    ci = lax.broadcasted_iota(jnp.int32, (rows, rows), 1)
    same = (ri // chunk) == (ci // chunk)
    if reverse:
        incl, strict = same & (ri <= ci), same & (ri < ci)
    else:
        incl, strict = same & (ri >= ci), same & (ri > ci)
    eye = jnp.where(ri == ci, 1.0, 0.0).astype(F32)
    r64 = lax.broadcasted_iota(jnp.int32, (chunk, chunk), 0)
    c64 = lax.broadcasted_iota(jnp.int32, (chunk, chunk), 1)
    incl64 = (r64 <= c64) if reverse else (r64 >= c64)

    kk = _dot_nt(k, k)
    n_double = int(math.log2(chunk)) - 1
    order = range(n_chunk - 1, -1, -1) if reverse else range(n_chunk)
    qk_blocks = {}
    for j in order:
        rs = slice(j * chunk, (j + 1) * chunk)
        qk_blocks[j] = _dot_nt(q[rs], k[rs])

    for a in range(2):
        gcol, bcol, glcol = scol[:, a:a + 1], scol[:, 2 + a:3 + a], scol[:, 4 + a:5 + a]
        grow, brow = srow[a:a + 1, :], srow[2 + a:3 + a, :]
        decay = jnp.exp(jnp.where(incl, gcol - grow, -jnp.inf))
        neg_l = jnp.where(strict, -(bcol * kk * decay), 0.0)
        inv = eye + neg_l
        pw = neg_l.astype(BF16)
        for _ in range(n_double):
            pw2 = _dot(pw, pw)
            pw = pw2.astype(BF16)
            inv = inv + _dot(inv.astype(BF16), pw)
        u_all = _dot((inv * brow).astype(BF16), v[:, a * hd:(a + 1) * hd])
        w_all = _dot((inv * (brow * jnp.exp(grow))).astype(BF16), k)
        e_g = jnp.exp(gcol)
        e_k = jnp.exp(glcol - gcol)
        for j in order:
            rs = slice(j * chunk, (j + 1) * chunk)
            growc = srowc_ref[0, 0, j, a:a + 1, :]
            decay_c = jnp.exp(jnp.where(incl64, gcol[rs] - growc, -jnp.inf))
            qkd = (qk_blocks[j] * decay_c).astype(BF16)
            s_old = s_ref[a]
            s_b = s_old.astype(BF16)
            v_new = u_all[rs] - _dot(w_all[rs].astype(BF16), s_b)
            o = e_g[rs] * _dot(q[rs], s_b) + _dot(qkd, v_new.astype(BF16))
            o_ref[0, rs, a * hd:(a + 1) * hd] = o.astype(o_ref.dtype)
            v_s = (v_new * e_k[rs]).astype(BF16)
            e_l = jnp.exp(glrow_ref[0, 0, j, a:a + 1, :])
            s_ref[a] = s_old * e_l + _dot(kt_ref[0, j], v_s)


def _gdn_scan(feat, kt, scol, srow, srowc, glrow, nct, reverse):
    n = feat.shape[0]
    hd = GDN_HEAD_DIM
    n_qk = kt.shape[0]
    n_tiles = n // TM
    cpt = TM // GDN_CHUNK

    def tile(t):
        if not reverse:
            return t
        return jnp.where(t < nct, nct - 1 - t, n_tiles - 1 - (t - nct))

    d = 1 if reverse else 0
    return pl.pallas_call(
        functools.partial(_gdn_scan_kernel, reverse=reverse, chunk=GDN_CHUNK),
        grid=(n_qk, n_tiles),
        in_specs=[pl.BlockSpec((TM, hd), lambda h, t: (tile(t), h)),
                  pl.BlockSpec((TM, hd), lambda h, t: (tile(t), n_qk + h)),
                  pl.BlockSpec((1, cpt, hd, GDN_CHUNK), lambda h, t: (h, tile(t), 0, 0)),
                  pl.BlockSpec((TM, 2 * hd), lambda h, t: (tile(t), n_qk + h)),
                  pl.BlockSpec((1, 1, TM, 6), lambda h, t: (d, h, tile(t), 0)),
                  pl.BlockSpec((1, 1, 4, TM), lambda h, t: (d, h, 0, tile(t))),
                  pl.BlockSpec((1, 1, cpt, 4, GDN_CHUNK), lambda h, t: (d, h, tile(t), 0, 0)),
                  pl.BlockSpec((1, 1, cpt, 2, hd), lambda h, t: (d, h, tile(t), 0, 0))],
        out_specs=pl.BlockSpec((1, TM, 2 * hd), lambda h, t: (0, tile(t), h)),
        out_shape=jax.ShapeDtypeStruct((1, n, 2 * n_qk * hd), BF16),
        scratch_shapes=[pltpu.VMEM((2, hd, hd), F32)],
        compiler_params=_cparams(("parallel", "arbitrary")),
        name="gdn_scan_bwd" if reverse else "gdn_scan_fwd",
    )(feat, feat, kt, feat, scol, srow, srowc, glrow)[0]


def _post_mixer(y, h_ref, mod_ref, lnw_ref, lnb_ref, rw_ref, rb_ref, h1_ref, v_ref, lg_ref, alpha):
    x = alpha * h_ref[...] + mod_ref[0, 2:3, :] * y
    h1 = _layer_norm(x, lnw_ref[...], lnb_ref[...])
    h1_ref[...] = h1
    v = _modulate(h1, mod_ref, 3, 4)
    v_ref[...] = v.astype(v_ref.dtype)
    lg_ref[...] = jnp.dot(v, rw_ref[...], preferred_element_type=F32, precision=HIGHEST) + rb_ref[...]


def _gdn_out_kernel(of_ref, ob_ref, z_ref, nw_ref, w_ref, h_ref, mod_ref, lnw_ref, lnb_ref, rw_ref, rb_ref,
                    h1_ref, v_ref, lg_ref, a_ref, *, alpha):
    hd = GDN_HEAD_DIM
    for hh in range(of_ref.shape[1] // hd):
        sl = slice(hh * hd, (hh + 1) * hd)
        o = of_ref[:, sl].astype(F32) + ob_ref[:, sl].astype(F32)
        o = o * lax.rsqrt(jnp.mean(o * o, axis=-1, keepdims=True) + NORM_EPS) * nw_ref[...]
        a_ref[:, sl] = (o * _silu(z_ref[:, sl].astype(F32))).astype(BF16)
    y = _dot(a_ref[...], w_ref[...])
    _post_mixer(y, h_ref, mod_ref, lnw_ref, lnb_ref, rw_ref, rb_ref, h1_ref, v_ref, lg_ref, alpha)


def _attn_out_kernel(a_ref, w_ref, h_ref, mod_ref, lnw_ref, lnb_ref, rw_ref, rb_ref, h1_ref, v_ref, lg_ref, *, alpha):
    y = _dot(a_ref[...], w_ref[...])
    _post_mixer(y, h_ref, mod_ref, lnw_ref, lnb_ref, rw_ref, rb_ref, h1_ref, v_ref, lg_ref, alpha)


def _mixer_out(kind, acts, w_out, h, mod, nct, ln_w, ln_b, router_w, router_b, alpha):
    n, d = h.shape
    row = lambda width: pl.BlockSpec((TM, width), lambda m: (m, 0))
    full = lambda arr: pl.BlockSpec(arr.shape, lambda m: (0,) * arr.ndim)
    n_e = router_w.shape[1]
    rw = jnp.zeros((d, LANES), F32).at[:, :n_e].set(router_w)
    rb = jnp.zeros((1, LANES), F32).at[0, :n_e].set(router_b)
    tail = [w_out, h, mod, ln_w.reshape(1, d), ln_b.reshape(1, d), rw, rb]
    tail_specs = [full(w_out), row(d), pl.BlockSpec((1, 6, d), lambda m: (jnp.where(m < nct, 0, 1), 0, 0)),
                  pl.BlockSpec((1, d), lambda m: (0, 0)), pl.BlockSpec((1, d), lambda m: (0, 0)), full(rw), full(rb)]
    if kind == "gdn":
        o_f, o_b, p, norm_w, z_col = acts
        v_dim = o_f.shape[1]
        body = functools.partial(_gdn_out_kernel, alpha=alpha)
        args = [o_f, o_b, p, norm_w.reshape(1, -1)] + tail
        specs = [row(v_dim), row(v_dim), pl.BlockSpec((TM, v_dim), lambda m: (m, z_col)),
                 pl.BlockSpec((1, norm_w.shape[0]), lambda m: (0, 0))] + tail_specs
        scratch = [pltpu.VMEM((TM, v_dim), BF16)]
    else:
        (a,) = acts
        body = functools.partial(_attn_out_kernel, alpha=alpha)
        args = [a] + tail
        specs = [row(a.shape[1])] + tail_specs
        scratch = []
    return pl.pallas_call(
        body,
        grid=(n // TM,),
        in_specs=specs,
        out_specs=[row(d), row(d), row(LANES)],
        out_shape=[jax.ShapeDtypeStruct((n, d), F32), jax.ShapeDtypeStruct((n, d), BF16),
                   jax.ShapeDtypeStruct((n, LANES), F32)],
        scratch_shapes=scratch,
        compiler_params=_cparams(("parallel",)),
        name=kind + "_out_ln",
    )(*args)


def _attn_kernel(q_ref, k_ref, v_ref, lam_ref, sw_ref, o_ref, acc_ref, m_ref, l_ref,
                 *, tk, nct, l_ctx, n_lat_chunks, lambda_init):
    mt = pl.program_id(1)
    q = q_ref[...]
    lane = lax.broadcasted_iota(jnp.int32, q.shape, 1)
    zero = jnp.zeros_like(q)
    q_maps = (jnp.where(lane < DIFF_HEAD_DIM, q, zero), jnp.where(lane >= DIFF_HEAD_DIM, q, zero))
    acc_ref[...] = jnp.zeros_like(acc_ref)
    m_ref[...] = jnp.full_like(m_ref, -jnp.inf)
    l_ref[...] = jnp.zeros_like(l_ref)

    def process(kj, vj):
        for i, qm in enumerate(q_maps):
            s = _dot_nt(qm, kj)
            m_old = m_ref[i]
            m_new = jnp.maximum(m_old, jnp.max(s, axis=-1, keepdims=True))
            alpha = jnp.exp2(m_old - m_new)
            p = jnp.exp2(s - m_new)
            l_ref[i] = alpha * l_ref[i] + jnp.sum(p, axis=-1, keepdims=True)
            acc_ref[i] = alpha * acc_ref[i] + _dot(p.astype(BF16), vj)
            m_ref[i] = m_new

    process(k_ref[0:l_ctx, :], v_ref[0:l_ctx, :])

    def body(j, carry):
        start = pl.multiple_of(l_ctx + j * tk, LANES)
        process(k_ref[pl.ds(start, tk), :], v_ref[pl.ds(start, tk), :])
        return carry

    lax.fori_loop(0, jnp.where(mt < nct, 0, n_lat_chunks), body, 0)

    lam = lam_ref[...]
    lam_full = (jnp.exp(jnp.sum(lam[0:1] * lam[1:2], axis=-1, keepdims=True))
                - jnp.exp(jnp.sum(lam[2:3] * lam[3:4], axis=-1, keepdims=True)) + lambda_init)
    o = acc_ref[0] / l_ref[0] - lam_full * (acc_ref[1] / l_ref[1])
    o = o * lax.rsqrt(jnp.mean(o * o, axis=-1, keepdims=True) + NORM_EPS) * sw_ref[...]
    o_ref[...] = (o * (1.0 - lambda_init)).astype(o_ref.dtype)


def _diff_attention(qkv, lam, subln_w, nct, l_ctx, lambda_init):
    n = qkv.shape[0]
    d = qkv.shape[1] // 3
    vd = subln_w.shape[0]
    n_heads = d // vd
    l_lat = n - l_ctx
    tk = math.gcd(l_lat, ATTN_TK)
    return pl.pallas_call(
        functools.partial(_attn_kernel, tk=tk, nct=nct, l_ctx=l_ctx, n_lat_chunks=l_lat // tk,
                          lambda_init=lambda_init),
        grid=(n_heads, n // TM),
        in_specs=[pl.BlockSpec((TM, vd), lambda h, m: (m, h)),
                  pl.BlockSpec((n, vd), lambda h, m: (0, n_heads + h)),
                  pl.BlockSpec((n, vd), lambda h, m: (0, 2 * n_heads + h)),
                  pl.BlockSpec(lam.shape, lambda h, m: (0, 0)),
                  pl.BlockSpec((1, vd), lambda h, m: (0, 0))],
        out_specs=pl.BlockSpec((TM, vd), lambda h, m: (m, h)),
        out_shape=jax.ShapeDtypeStruct((n, d), BF16),
        scratch_shapes=[pltpu.VMEM((2, TM, vd), F32), pltpu.VMEM((2, TM, 1), F32), pltpu.VMEM((2, TM, 1), F32)],
        compiler_params=_cparams(("parallel", "parallel")),
        name="diff_attn",
    )(qkv, qkv, qkv, lam, subln_w.reshape(1, vd))


def _moe_kernel(be_ref, nb_ref, x_ref, wgu_ref, bgu_ref, wdn_ref, bdn_ref, y_ref, wgu_s, wdn_s, *, d_ff):
    b = pl.program_id(0)
    changed = jnp.logical_or(b == 0, be_ref[b] != be_ref[jnp.maximum(b - 1, 0)])

    @pl.when(changed)
    def _():
        wgu_s[...] = wgu_ref[0].astype(BF16)
        wdn_s[...] = wdn_ref[0].astype(BF16)

    @pl.when(b < nb_ref[0])
    def _():
        h = _dot(x_ref[...], wgu_s[...]) + bgu_ref[0]
        gate = jnp.minimum(h[:, :d_ff], SWIGLU_LIMIT)
        up = jnp.clip(h[:, d_ff:], -SWIGLU_LIMIT, SWIGLU_LIMIT)
        glu = gate * jax.nn.sigmoid(gate * SWIGLU_ALPHA)
        act = ((up + 1.0) * glu).astype(BF16)
        y_ref[...] = (_dot(act, wdn_s[...]) + bdn_ref[0]).astype(y_ref.dtype)

    @pl.when(b >= nb_ref[0])
    def _():
        y_ref[...] = jnp.zeros_like(y_ref)


def _moe_experts(x_sorted, block_e, n_used, w_gu, b_gu, w_dn, b_dn):
    n_rows, d = x_sorted.shape
    n_e, _, f2 = w_gu.shape
    grid_spec = pltpu.PrefetchScalarGridSpec(
        num_scalar_prefetch=2,
        grid=(n_rows // MOE_BM,),
        in_specs=[pl.BlockSpec((MOE_BM, d), lambda b, be, nb: (b, 0)),
                  pl.BlockSpec((1, d, f2), lambda b, be, nb: (be[b], 0, 0)),
                  pl.BlockSpec((1, 1, f2), lambda b, be, nb: (be[b], 0, 0)),
                  pl.BlockSpec((1, f2 // 2, d), lambda b, be, nb: (be[b], 0, 0)),
                  pl.BlockSpec((1, 1, d), lambda b, be, nb: (be[b], 0, 0))],
        out_specs=pl.BlockSpec((MOE_BM, d), lambda b, be, nb: (b, 0)),
        scratch_shapes=[pltpu.VMEM((d, f2), BF16), pltpu.VMEM((f2 // 2, d), BF16)],
    )
    return pl.pallas_call(
        functools.partial(_moe_kernel, d_ff=f2 // 2),
        grid_spec=grid_spec,
        out_shape=jax.ShapeDtypeStruct((n_rows, d), BF16),
        compiler_params=_cparams(("arbitrary",)),
        name="moe_experts",
    )(block_e, n_used, x_sorted, w_gu, b_gu.reshape(n_e, 1, f2), w_dn, b_dn.reshape(n_e, 1, d))


def _moe_combine_kernel(y0_ref, y1_ref, y2_ref, y3_ref, g_ref, h_ref, mod_ref, lnw_ref, lnb_ref, o_ref, *, alpha):
    g = g_ref[...]
    f = jnp.zeros(h_ref.shape, F32)
    for i, y_ref in enumerate((y0_ref, y1_ref, y2_ref, y3_ref)):
        f = f + g[:, i:i + 1] * y_ref[...].astype(F32)
    x = alpha * h_ref[...] + mod_ref[0, 5:6, :] * f
    o_ref[...] = _layer_norm(x, lnw_ref[...], lnb_ref[...])


def _moe_combine(ys, gates, h1, mod, nct, ln_w, ln_b, alpha):
    n, d = h1.shape
    row = lambda width: pl.BlockSpec((TM, width), lambda m: (m, 0))
    return pl.pallas_call(
        functools.partial(_moe_combine_kernel, alpha=alpha),
        grid=(n // TM,),
        in_specs=[row(d)] * TOP_K + [row(TOP_K), row(d),
                                     pl.BlockSpec((1, 6, d), lambda m: (jnp.where(m < nct, 0, 1), 0, 0)),
                                     pl.BlockSpec((1, d), lambda m: (0, 0)), pl.BlockSpec((1, d), lambda m: (0, 0))],
        out_specs=row(d),
        out_shape=jax.ShapeDtypeStruct((n, d), F32),
        compiler_params=_cparams(("parallel",)),
        name="moe_combine_ln",
    )(*ys, gates, h1, mod, ln_w.reshape(1, d), ln_b.reshape(1, d))


def _moe(v_bf16, logits, n_e, h1, mod, nct, ln_w, ln_b, alpha, w_gu, b_gu, w_dn, b_dn):
    n, d = v_bf16.shape
    top_val, top_idx = lax.top_k(logits[:, :n_e], TOP_K)
    gates = jax.nn.softmax(top_val, axis=-1)
    nk = n * TOP_K
    flat_e = top_idx.reshape(nk)
    order = jnp.argsort(flat_e)
    sorted_e = flat_e[order]
    counts = jnp.sum(flat_e[:, None] == jnp.arange(n_e, dtype=flat_e.dtype)[None, :], axis=0, dtype=jnp.int32)
    padded = (counts + MOE_BM - 1) // MOE_BM * MOE_BM
    pad_end = jnp.cumsum(padded)
    pad_start = pad_end - padded
    start = jnp.cumsum(counts) - counts
    dest_sorted = pad_start[sorted_e] + (jnp.arange(nk, dtype=jnp.int32) - start[sorted_e])
    dest = jnp.zeros((nk,), jnp.int32).at[order].set(dest_sorted)
    n_blocks = -(-nk // MOE_BM) + n_e
    n_rows = n_blocks * MOE_BM
    row_tok = jnp.zeros((n_rows,), jnp.int32).at[dest].set(jnp.arange(nk, dtype=jnp.int32) // TOP_K)
    block_e = jnp.minimum(jnp.searchsorted(pad_end, jnp.arange(n_blocks, dtype=jnp.int32) * MOE_BM, side='right'),
                          n_e - 1).astype(jnp.int32)
    n_used = (pad_end[-1:] // MOE_BM).astype(jnp.int32)
    x_sorted = v_bf16[row_tok]
    yb = _moe_experts(x_sorted, block_e, n_used, w_gu, b_gu, w_dn, b_dn)
    dest2 = dest.reshape(n, TOP_K)
    ys = [yb[dest2[:, i]] for i in range(TOP_K)]
    return _moe_combine(ys, gates, h1, mod, nct, ln_w, ln_b, alpha)


def _rope_tables(l_ctx, l_lat):
    rows = l_lat // GRID_W
    row = jnp.repeat(jnp.arange(rows, dtype=F32), GRID_W)
    col = jnp.tile(jnp.arange(GRID_W, dtype=F32), rows)
    axis_dim = DIFF_HEAD_DIM // 2
    inv_freq = ROPE_THETA ** (-jnp.arange(0, axis_dim, 2, dtype=F32) / axis_dim)
    ang_r = row[:, None] * inv_freq
    ang_c = col[:, None] * inv_freq
    ang = jnp.concatenate([ang_r, ang_r, ang_c, ang_c], -1)
    ang = jnp.concatenate([jnp.zeros((l_ctx, DIFF_HEAD_DIM), F32), ang], 0)
    ang = jnp.concatenate([ang, ang], -1)
    return jnp.cos(ang), jnp.sin(ang)


def _gdn_scalars(gates, n_vh):
    n = gates.shape[0]
    n_qk = n_vh // 2
    nc = n // GDN_CHUNK
    gc = gates[:, :2 * n_vh].reshape(n, 2, n_qk, 2)
    beta = gates[:, 2 * n_vh:4 * n_vh].reshape(n, 2, n_qk, 2)
    gcc = gc.reshape(nc, GDN_CHUNK, 2, n_qk, 2)
    gl = jnp.stack([gcc[:, -1, 0], gcc[:, 0, 1]], axis=1)
    gl_tok = jnp.repeat(gl, GDN_CHUNK, axis=0)
    col = jnp.concatenate([gc, beta, gl_tok], axis=-1)
    scol = jnp.transpose(col, (1, 2, 0, 3))
    srow = jnp.transpose(col[..., :4], (1, 2, 3, 0))
    srowc = jnp.transpose(col[..., :4].reshape(nc, GDN_CHUNK, 2, n_qk, 4), (2, 3, 0, 4, 1))
    glrow = jnp.broadcast_to(jnp.transpose(gl, (1, 2, 0, 3))[..., None], (2, n_qk, nc, 2, GDN_HEAD_DIM))
    return scol, srow, srowc, glrow


def kernel(x, c, ctx, c_ctx, ada_w, ada_b, ln_w, ln_b, gdn_w_in, gdn_conv_w, gdn_a_log, gdn_dt_bias, gdn_norm_w,
           gdn_w_out, diff_w_in, diff_lambda, diff_subln_w, diff_w_out, router_w, router_b, moe_w_gate_up,
           moe_b_gate_up, moe_w_down, moe_b_down):
    batch, l_lat, d = x.shape
    l_ctx = ctx.shape[1]
    depth = ada_w.shape[0]
    assert batch == 1 and l_ctx % TM == 0 and l_lat % TM == 0 and l_lat % GRID_W == 0
    nct = l_ctx // TM
    n_e = router_w.shape[-1]
    alpha = (2 * depth) ** 0.25
    n_vh = gdn_a_log.shape[-1]
    v_dim = n_vh * GDN_HEAD_DIM
    conv_dim = gdn_conv_w.shape[-1]
    qk_dim = (conv_dim - v_dim) // 2
    n_qk = qk_dim // GDN_HEAD_DIM

    h = jnp.concatenate([ctx[0], x[0]], axis=0)
    n = h.shape[0]
    mods = _modulations(c_ctx, c, ada_w, ada_b)
    cos, sin = _rope_tables(l_ctx, l_lat)

    for i in range(depth):
        j = i // 2
        mod = mods[i]
        if i % 2 == 0:
            w_in = gdn_w_in[j]
            w_main = w_in[:, :conv_dim + v_dim].astype(BF16)
            w_ab = jnp.zeros((d, LANES), F32).at[:, :4 * n_vh].set(w_in[:, conv_dim + v_dim:]).astype(BF16)
            p, ab = _project(h, mod, nct, [w_main, w_ab], [BF16, F32])
            gates = _gdn_gates(ab, gdn_a_log[j], gdn_dt_bias[j])
            feat = _gdn_features(p, gdn_conv_w[j], nct, qk_dim)
            kt = jnp.transpose(feat[:, qk_dim:2 * qk_dim].reshape(n // GDN_CHUNK, GDN_CHUNK, n_qk, GDN_HEAD_DIM),
                               (2, 0, 3, 1))
            scol, srow, srowc, glrow = _gdn_scalars(gates, n_vh)
            o_f = _gdn_scan(feat, kt, scol, srow, srowc, glrow, nct, False)
            o_b = _gdn_scan(feat, kt, scol, srow, srowc, glrow, nct, True)
            acts = (o_f, o_b, p, gdn_norm_w[j], conv_dim // v_dim)
            h1, v, logits = _mixer_out("gdn", acts, gdn_w_out[j].astype(BF16), h, mod, nct, ln_w[i, 0], ln_b[i, 0],
                                       router_w[i], router_b[i], alpha)
        else:
            lambda_init = 0.8 - 0.6 * math.exp(-0.3 * i)
            qkv = _qkv_rope(h, mod, nct, diff_w_in[j].astype(BF16), cos, sin)
            a = _diff_attention(qkv, diff_lambda[j], diff_subln_w[j], nct, l_ctx, lambda_init)
            h1, v, logits = _mixer_out("attn", (a,), diff_w_out[j].astype(BF16), h, mod, nct, ln_w[i, 0], ln_b[i, 0],
                                       router_w[i], router_b[i], alpha)
        h = _moe(v, logits, n_e, h1, mod, nct, ln_w[i, 1], ln_b[i, 1], alpha,
                 moe_w_gate_up[i], moe_b_gate_up[i], moe_w_down[i], moe_b_down[i])
    return h[l_ctx:].reshape(batch, l_lat, d)
```

```python
import functools
import math

import jax
import jax.numpy as jnp
from jax import lax
from jax.experimental import pallas as pl
from jax.experimental.pallas import tpu as pltpu

F32 = jnp.float32
BF16 = jnp.bfloat16
HIGHEST = lax.Precision.HIGHEST

GRID_W = 64
GDN_HEAD_DIM = 128
GDN_CHUNK = 64
DIFF_HEAD_DIM = 64
ROPE_THETA = 10000.0
TOP_K = 4
SWIGLU_LIMIT = 7.0
SWIGLU_ALPHA = 1.702
NORM_EPS = 1e-5
L2_EPS = 1e-6

LANES = 128
SUBLANES = 8
HALO = 16
VMEM_LIMIT = 56 * 1024 * 1024

TM = 256
MOE_BM = 256
ATTN_TK = 1024


def _cparams(sem):
    return pltpu.CompilerParams(dimension_semantics=sem, vmem_limit_bytes=VMEM_LIMIT)


def _silu(x):
    return x * jax.nn.sigmoid(x)


def _dot(a, b):
    return jnp.dot(a, b, preferred_element_type=F32)


def _dot_nt(a, b):
    return lax.dot_general(a, b, (((1,), (1,)), ((), ())), preferred_element_type=F32)


def _modulate(h, mod_ref, shift, scale):
    return h * (1.0 + mod_ref[0, scale:scale + 1, :]) + mod_ref[0, shift:shift + 1, :]


def _layer_norm(x, w, b):
    mu = jnp.mean(x, axis=-1, keepdims=True)
    xc = x - mu
    var = jnp.mean(xc * xc, axis=-1, keepdims=True)
    return xc * lax.rsqrt(var + NORM_EPS) * w + b


def _mod_kernel(s_ref, w_ref, b_ref, o_ref):
    s = _silu(s_ref[...])
    o_ref[0] = jnp.dot(s, w_ref[0], preferred_element_type=F32, precision=HIGHEST) + b_ref[0]


def _modulations(c_ctx, c, ada_w, ada_b):
    depth, d, d6 = ada_w.shape
    tn = d6 // 4
    s = jnp.zeros((SUBLANES, d), F32).at[0].set(c_ctx).at[1].set(c[0])
    out = pl.pallas_call(
        _mod_kernel,
        grid=(depth, d6 // tn),
        in_specs=[pl.BlockSpec((SUBLANES, d), lambda i, n: (0, 0)),
                  pl.BlockSpec((1, d, tn), lambda i, n: (i, 0, n)),
                  pl.BlockSpec((1, 1, tn), lambda i, n: (i, 0, n))],
        out_specs=pl.BlockSpec((1, SUBLANES, tn), lambda i, n: (i, 0, n)),
        out_shape=jax.ShapeDtypeStruct((depth, SUBLANES, d6), F32),
        compiler_params=_cparams(("parallel", "parallel")),
        name="adaln_mod",
    )(s, ada_w, ada_b.reshape(depth, 1, d6))
    return out[:, :2].reshape(depth, 2, 6, d)


def _proj_kernel(h_ref, mod_ref, *refs, n_w, chunk):
    w_refs, o_refs = refs[:n_w], refs[n_w:]
    u = _modulate(h_ref[...], mod_ref, 0, 1).astype(BF16)
    for w_ref, o_ref in zip(w_refs, o_refs):
        n = w_ref.shape[1]
        for j in range(0, n, chunk):
            jc = min(chunk, n - j)
            o_ref[:, j:j + jc] = _dot(u, w_ref[:, j:j + jc]).astype(o_ref.dtype)


def _project(h, mod, nct, weights, out_dtypes):
    n, d = h.shape
    return pl.pallas_call(
        functools.partial(_proj_kernel, n_w=len(weights), chunk=512),
        grid=(n // TM,),
        in_specs=[pl.BlockSpec((TM, d), lambda m: (m, 0)),
                  pl.BlockSpec((1, 6, d), lambda m: (jnp.where(m < nct, 0, 1), 0, 0))]
                 + [pl.BlockSpec(w.shape, lambda m: (0, 0)) for w in weights],
        out_specs=[pl.BlockSpec((TM, w.shape[1]), lambda m: (m, 0)) for w in weights],
        out_shape=[jax.ShapeDtypeStruct((n, w.shape[1]), dt) for w, dt in zip(weights, out_dtypes)],
        compiler_params=_cparams(("parallel",)),
        name="mod_proj",
    )(h, mod, *weights)


def _qkv_rope_kernel(h_ref, mod_ref, w_ref, cos_ref, sin_ref, o_ref, *, chunk, n_rope, n_q, q_scale):
    u = _modulate(h_ref[...], mod_ref, 0, 1).astype(BF16)
    rep = chunk // LANES
    cos = jnp.concatenate([cos_ref[...]] * rep, axis=1)
    sin = jnp.concatenate([sin_ref[...]] * rep, axis=1)
    lane = lax.broadcasted_iota(jnp.int32, (h_ref.shape[0], chunk), 1)
    first = (lane % 32) < 16
    for j in range(w_ref.shape[1] // chunk):
        y = _dot(u, w_ref[:, j * chunk:(j + 1) * chunk])
        if j < n_rope:
            rot = jnp.where(first, -pltpu.roll(y, chunk - 16, 1), pltpu.roll(y, 16, 1))
            y = y * cos + rot * sin
            if j < n_q:
                y = y * q_scale
        o_ref[:, j * chunk:(j + 1) * chunk] = y.astype(o_ref.dtype)


def _qkv_rope(h, mod, nct, w, cos, sin):
    n, d = h.shape
    chunk = 512
    return pl.pallas_call(
        functools.partial(_qkv_rope_kernel, chunk=chunk, n_rope=2 * d // chunk, n_q=d // chunk,
                          q_scale=DIFF_HEAD_DIM ** -0.5 * math.log2(math.e)),
        grid=(n // TM,),
        in_specs=[pl.BlockSpec((TM, d), lambda m: (m, 0)),
                  pl.BlockSpec((1, 6, d), lambda m: (jnp.where(m < nct, 0, 1), 0, 0)),
                  pl.BlockSpec(w.shape, lambda m: (0, 0)),
                  pl.BlockSpec((TM, LANES), lambda m: (m, 0)),
                  pl.BlockSpec((TM, LANES), lambda m: (m, 0))],
        out_specs=pl.BlockSpec((TM, w.shape[1]), lambda m: (m, 0)),
        out_shape=jax.ShapeDtypeStruct((n, w.shape[1]), BF16),
        compiler_params=_cparams(("parallel",)),
        name="qkv_rope",
    )(h, mod, w, cos, sin)


def _gdn_gates_kernel(ab_ref, alog_ref, dtb_ref, o_ref, *, chunk, n_heads):
    x = ab_ref[...]
    t = x + dtb_ref[...]
    softplus = jnp.maximum(t, 0.0) + jnp.log1p(jnp.exp(-jnp.abs(t)))
    g = -jnp.exp(alog_ref[...]) * softplus
    beta = jax.nn.sigmoid(x)
    tm = x.shape[0]
    r = lax.broadcasted_iota(jnp.int32, (tm, tm), 0)
    c = lax.broadcasted_iota(jnp.int32, (tm, tm), 1)
    same = (r // chunk) == (c // chunk)
    t_fwd = jnp.where(same & (r >= c), 1.0, 0.0).astype(F32)
    t_bwd = jnp.where(same & (r <= c), 1.0, 0.0).astype(F32)
    g_fwd = jnp.dot(t_fwd, g, preferred_element_type=F32, precision=HIGHEST)
    g_bwd = jnp.dot(t_bwd, g, preferred_element_type=F32, precision=HIGHEST)
    lane = lax.broadcasted_iota(jnp.int32, x.shape, 1)
    o_ref[...] = jnp.where(lane < n_heads, g_fwd, jnp.where(lane < 2 * n_heads, g_bwd, beta))


def _gdn_gates(ab, a_log, dt_bias):
    n = ab.shape[0]
    nh = a_log.shape[-1]
    pad = lambda v: jnp.zeros((1, LANES), F32).at[0, :2 * nh].set(v.reshape(-1))
    return pl.pallas_call(
        functools.partial(_gdn_gates_kernel, chunk=GDN_CHUNK, n_heads=nh),
        grid=(n // TM,),
        in_specs=[pl.BlockSpec((TM, LANES), lambda m: (m, 0)),
                  pl.BlockSpec((1, LANES), lambda m: (0, 0)),
                  pl.BlockSpec((1, LANES), lambda m: (0, 0))],
        out_specs=pl.BlockSpec((TM, LANES), lambda m: (m, 0)),
        out_shape=jax.ShapeDtypeStruct((n, LANES), F32),
        compiler_params=_cparams(("parallel",)),
        name="gdn_gates",
    )(ab, pad(a_log), pad(dt_bias))


def _gdn_feat_kernel(x_ref, prev_ref, next_ref, w_ref, o_ref, xs_ref, *, nct, n_tiles, n_norm, n_q, q_scale, width):
    m = pl.program_id(0)
    j = pl.program_id(1)
    tm, tn = x_ref.shape
    pad = width // 2
    prev_ok = jnp.logical_and(m != 0, m != nct)
    next_ok = jnp.logical_and(m != nct - 1, m != n_tiles - 1)
    xs_ref[0:HALO, :] = jnp.where(prev_ok, prev_ref[...].astype(F32), 0.0)
    xs_ref[HALO:HALO + tm, :] = x_ref[...].astype(F32)
    xs_ref[HALO + tm:, :] = jnp.where(next_ok, next_ref[...].astype(F32), 0.0)
    acc = jnp.zeros((tm, tn), F32)
    for t in range(width):
        acc = acc + w_ref[t:t + 1, :] * xs_ref[HALO - pad + t:HALO - pad + t + tm, :]
    y = _silu(acc)

    @pl.when(j >= n_norm)
    def _():
        o_ref[...] = y.astype(o_ref.dtype)

    @pl.when(j < n_norm)
    def _():
        scale = jnp.where(j < n_q, q_scale, 1.0).astype(F32)
        for hh in range(tn // GDN_HEAD_DIM):
            sl = slice(hh * GDN_HEAD_DIM, (hh + 1) * GDN_HEAD_DIM)
            yh = y[:, sl]
            inv = lax.rsqrt(jnp.sum(yh * yh, axis=-1, keepdims=True) + L2_EPS)
            o_ref[:, sl] = (yh * (inv * scale)).astype(o_ref.dtype)


def _gdn_features(p, conv_w, nct, qk_dim):
    n = p.shape[0]
    width, conv_dim = conv_w.shape
    tn = 512
    n_tiles = n // TM
    rb = TM // HALO
    last_rb = n // HALO - 1
    return pl.pallas_call(
        functools.partial(_gdn_feat_kernel, nct=nct, n_tiles=n_tiles, n_norm=2 * qk_dim // tn, n_q=qk_dim // tn,
                          q_scale=GDN_HEAD_DIM ** -0.5, width=width),
        grid=(n_tiles, conv_dim // tn),
        in_specs=[pl.BlockSpec((TM, tn), lambda m, j: (m, j)),
                  pl.BlockSpec((HALO, tn), lambda m, j: (jnp.maximum(m * rb - 1, 0), j)),
                  pl.BlockSpec((HALO, tn), lambda m, j: (jnp.minimum((m + 1) * rb, last_rb), j)),
                  pl.BlockSpec((width, tn), lambda m, j: (0, j))],
        out_specs=pl.BlockSpec((TM, tn), lambda m, j: (m, j)),
        out_shape=jax.ShapeDtypeStruct((n, conv_dim), BF16),
        scratch_shapes=[pltpu.VMEM((TM + 2 * HALO, tn), F32)],
        compiler_params=_cparams(("parallel", "parallel")),
        name="gdn_features",
    )(p, p, p, conv_w)


def _gdn_scan_kernel(q_ref, k_ref, kt_ref, v_ref, scol_ref, srow_ref, srowc_ref, glrow_ref, o_ref, s_ref,
                     *, reverse, chunk):
    t = pl.program_id(1)

    @pl.when(t == 0)
    def _():
        s_ref[...] = jnp.zeros_like(s_ref)

    rows = q_ref.shape[0]
    hd = GDN_HEAD_DIM
    n_chunk = rows // chunk
    q = q_ref[...]
    k = k_ref[...]
    v = v_ref[...]
    scol = scol_ref[0, 0]
    srow = srow_ref[0, 0]

    ri = lax.broadcasted_iota(jnp.int32, (rows, rows), 0)
    ci = lax.broadcasted_iota(jnp.int32, (rows, rows), 1)
    same = (ri // chunk) == (ci // chunk)
    if reverse:
        incl, strict = same & (ri <= ci), same & (ri < ci)
    else:
        incl, strict = same & (ri >= ci), same & (ri > ci)
    eye = jnp.where(ri == ci, 1.0, 0.0).astype(F32)
    r64 = lax.broadcasted_iota(jnp.int32, (chunk, chunk), 0)
    c64 = lax.broadcasted_iota(jnp.int32, (chunk, chunk), 1)
    incl64 = (r64 <= c64) if reverse else (r64 >= c64)

    kk = _dot_nt(k, k)
    n_double = int(math.log2(chunk)) - 1
    order = range(n_chunk - 1, -1, -1) if reverse else range(n_chunk)
    pair = range(2)
    gcol = [scol[:, a:a + 1] for a in pair]
    bcol = [scol[:, 2 + a:3 + a] for a in pair]
    e_g = [jnp.exp(g) for g in gcol]
    e_k = [jnp.exp(scol[:, 4 + a:5 + a] - gcol[a]) for a in pair]
    neg_l = [jnp.where(strict, -(bcol[a] * kk * jnp.exp(jnp.where(incl, gcol[a] - srow[a:a + 1, :], -jnp.inf))), 0.0)
             for a in pair]
    inv = [eye + x for x in neg_l]
    pw = [x.astype(BF16) for x in neg_l]
    for _ in range(n_double):
        pw = [_dot(x, x).astype(BF16) for x in pw]
        inv = [inv[a] + _dot(inv[a].astype(BF16), pw[a]) for a in pair]
    rhs = [jnp.concatenate([v[:, a * hd:(a + 1) * hd].astype(F32) * bcol[a],
                            k.astype(F32) * (bcol[a] * e_g[a])], axis=1).astype(BF16) for a in pair]
    uw = [_dot(inv[a].astype(BF16), rhs[a]) for a in pair]

    state = s_ref[...]
    for j in order:
        rs = slice(j * chunk, (j + 1) * chunk)
        s_b = state.astype(BF16)
        qk_c = _dot_nt(q[rs], k[rs])
        q_s = _dot(q[rs], s_b)
        v_new = [uw[a][rs, :hd] - _dot(uw[a][rs, hd:].astype(BF16), s_b[:, a * hd:(a + 1) * hd]) for a in pair]
        qkd = [(qk_c * jnp.exp(jnp.where(incl64, gcol[a][rs] - srowc_ref[0, 0, j, a:a + 1, :], -jnp.inf))).astype(BF16)
               for a in pair]
        for a in pair:
            o = e_g[a][rs] * q_s[:, a * hd:(a + 1) * hd] + _dot(qkd[a], v_new[a].astype(BF16))
            o_ref[0, rs, a * hd:(a + 1) * hd] = o.astype(o_ref.dtype)
        v_s = jnp.concatenate([v_new[a] * e_k[a][rs] for a in pair], axis=1).astype(BF16)
        e_l = jnp.exp(glrow_ref[0, 0, j])
        state = state * e_l + _dot(kt_ref[0, j], v_s)
    s_ref[...] = state


def _gdn_scan(feat, kt, scol, srow, srowc, glrow, nct, reverse):
    n = feat.shape[0]
    hd = GDN_HEAD_DIM
    n_qk = kt.shape[0]
    n_tiles = n // TM
    cpt = TM // GDN_CHUNK

    def tile(t):
        if not reverse:
            return t
        return jnp.where(t < nct, nct - 1 - t, n_tiles - 1 - (t - nct))

    d = 1 if reverse else 0
    return pl.pallas_call(
        functools.partial(_gdn_scan_kernel, reverse=reverse, chunk=GDN_CHUNK),
        grid=(n_qk, n_tiles),
        in_specs=[pl.BlockSpec((TM, hd), lambda h, t: (tile(t), h)),
                  pl.BlockSpec((TM, hd), lambda h, t: (tile(t), n_qk + h)),
                  pl.BlockSpec((1, cpt, hd, GDN_CHUNK), lambda h, t: (h, tile(t), 0, 0)),
                  pl.BlockSpec((TM, 2 * hd), lambda h, t: (tile(t), n_qk + h)),
                  pl.BlockSpec((1, 1, TM, 6), lambda h, t: (d, h, tile(t), 0)),
                  pl.BlockSpec((1, 1, 4, TM), lambda h, t: (d, h, 0, tile(t))),
                  pl.BlockSpec((1, 1, cpt, 4, GDN_CHUNK), lambda h, t: (d, h, tile(t), 0, 0)),
                  pl.BlockSpec((1, 1, cpt, 1, 2 * hd), lambda h, t: (d, h, tile(t), 0, 0))],
        out_specs=pl.BlockSpec((1, TM, 2 * hd), lambda h, t: (0, tile(t), h)),
        out_shape=jax.ShapeDtypeStruct((1, n, 2 * n_qk * hd), BF16),
        scratch_shapes=[pltpu.VMEM((hd, 2 * hd), F32)],
        compiler_params=_cparams(("parallel", "arbitrary")),
        name="gdn_scan_bwd" if reverse else "gdn_scan_fwd",
    )(feat, feat, kt, feat, scol, srow, srowc, glrow)[0]


def _post_mixer(y, h_ref, mod_ref, lnw_ref, lnb_ref, rw_ref, rb_ref, h1_ref, v_ref, lg_ref, alpha):
    x = alpha * h_ref[...] + mod_ref[0, 2:3, :] * y
    h1 = _layer_norm(x, lnw_ref[...], lnb_ref[...])
    h1_ref[...] = h1
    v = _modulate(h1, mod_ref, 3, 4)
    v_ref[...] = v.astype(v_ref.dtype)
    lg_ref[...] = jnp.dot(v, rw_ref[...], preferred_element_type=F32, precision=HIGHEST) + rb_ref[...]


def _gdn_out_kernel(of_ref, ob_ref, z_ref, nw_ref, w_ref, h_ref, mod_ref, lnw_ref, lnb_ref, rw_ref, rb_ref,
                    h1_ref, v_ref, lg_ref, a_ref, *, alpha):
    hd = GDN_HEAD_DIM
    for hh in range(of_ref.shape[1] // hd):
        sl = slice(hh * hd, (hh + 1) * hd)
        o = of_ref[:, sl].astype(F32) + ob_ref[:, sl].astype(F32)
        o = o * lax.rsqrt(jnp.mean(o * o, axis=-1, keepdims=True) + NORM_EPS) * nw_ref[...]
        a_ref[:, sl] = (o * _silu(z_ref[:, sl].astype(F32))).astype(BF16)
    y = _dot(a_ref[...], w_ref[...])
    _post_mixer(y, h_ref, mod_ref, lnw_ref, lnb_ref, rw_ref, rb_ref, h1_ref, v_ref, lg_ref, alpha)


def _attn_out_kernel(a_ref, w_ref, h_ref, mod_ref, lnw_ref, lnb_ref, rw_ref, rb_ref, h1_ref, v_ref, lg_ref, *, alpha):
    y = _dot(a_ref[...], w_ref[...])
    _post_mixer(y, h_ref, mod_ref, lnw_ref, lnb_ref, rw_ref, rb_ref, h1_ref, v_ref, lg_ref, alpha)


def _mixer_out(kind, acts, w_out, h, mod, nct, ln_w, ln_b, router_w, router_b, alpha):
    n, d = h.shape
    row = lambda width: pl.BlockSpec((TM, width), lambda m: (m, 0))
    full = lambda arr: pl.BlockSpec(arr.shape, lambda m: (0,) * arr.ndim)
    n_e = router_w.shape[1]
    rw = jnp.zeros((d, LANES), F32).at[:, :n_e].set(router_w)
    rb = jnp.zeros((1, LANES), F32).at[0, :n_e].set(router_b)
    tail = [w_out, h, mod, ln_w.reshape(1, d), ln_b.reshape(1, d), rw, rb]
    tail_specs = [full(w_out), row(d), pl.BlockSpec((1, 6, d), lambda m: (jnp.where(m < nct, 0, 1), 0, 0)),
                  pl.BlockSpec((1, d), lambda m: (0, 0)), pl.BlockSpec((1, d), lambda m: (0, 0)), full(rw), full(rb)]
    if kind == "gdn":
        o_f, o_b, p, norm_w, z_col = acts
        v_dim = o_f.shape[1]
        body = functools.partial(_gdn_out_kernel, alpha=alpha)
        args = [o_f, o_b, p, norm_w.reshape(1, -1)] + tail
        specs = [row(v_dim), row(v_dim), pl.BlockSpec((TM, v_dim), lambda m: (m, z_col)),
                 pl.BlockSpec((1, norm_w.shape[0]), lambda m: (0, 0))] + tail_specs
        scratch = [pltpu.VMEM((TM, v_dim), BF16)]
    else:
        (a,) = acts
        body = functools.partial(_attn_out_kernel, alpha=alpha)
        args = [a] + tail
        specs = [row(a.shape[1])] + tail_specs
        scratch = []
    return pl.pallas_call(
        body,
        grid=(n // TM,),
        in_specs=specs,
        out_specs=[row(d), row(d), row(LANES)],
        out_shape=[jax.ShapeDtypeStruct((n, d), F32), jax.ShapeDtypeStruct((n, d), BF16),
                   jax.ShapeDtypeStruct((n, LANES), F32)],
        scratch_shapes=scratch,
        compiler_params=_cparams(("parallel",)),
        name=kind + "_out_ln",
    )(*args)


def _attn_kernel(q_ref, k_ref, v_ref, lam_ref, sw_ref, o_ref, acc_ref, m_ref, l_ref, sa_ref, sb_ref,
                 *, tk, nct, l_ctx, n_pairs, lambda_init):
    mt = pl.program_id(1)
    q = q_ref[...]
    lane = lax.broadcasted_iota(jnp.int32, q.shape, 1)
    zero = jnp.zeros_like(q)
    q_maps = (jnp.where(lane < DIFF_HEAD_DIM, q, zero), jnp.where(lane >= DIFF_HEAD_DIM, q, zero))
    acc_ref[...] = jnp.zeros_like(acc_ref)
    m_ref[...] = jnp.full_like(m_ref, -jnp.inf)
    l_ref[...] = jnp.zeros_like(l_ref)

    def scores(start, s_ref):
        kj = k_ref[pl.ds(start, tk), :]
        for i, qm in enumerate(q_maps):
            s_ref[i] = _dot_nt(qm, kj)

    def absorb(score_of_map, vj):
        for i in range(2):
            s = score_of_map(i)
            m_old = m_ref[i]
            m_new = jnp.maximum(m_old, jnp.max(s, axis=-1, keepdims=True))
            alpha = jnp.exp2(m_old - m_new)
            p = jnp.exp2(s - m_new)
            l_ref[i] = alpha * l_ref[i] + jnp.sum(p, axis=-1, keepdims=True)
            acc_ref[i] = alpha * acc_ref[i] + _dot(p.astype(BF16), vj)
            m_ref[i] = m_new

    k_ctx = k_ref[0:l_ctx, :]
    absorb(lambda i: _dot_nt(q_maps[i], k_ctx), v_ref[0:l_ctx, :])

    @pl.when(mt >= nct)
    def _():
        last = l_ctx + (2 * n_pairs - 1) * tk
        scores(l_ctx, sa_ref)

        def body(jj, carry):
            c0 = pl.multiple_of(l_ctx + 2 * jj * tk, LANES)
            c1 = pl.multiple_of(c0 + tk, LANES)
            c2 = pl.multiple_of(jnp.minimum(c1 + tk, last), LANES)
            scores(c1, sb_ref)
            absorb(lambda i: sa_ref[i], v_ref[pl.ds(c0, tk), :])
            scores(c2, sa_ref)
            absorb(lambda i: sb_ref[i], v_ref[pl.ds(c1, tk), :])
            return carry

        lax.fori_loop(0, n_pairs, body, 0)

    lam = lam_ref[...]
    lam_full = (jnp.exp(jnp.sum(lam[0:1] * lam[1:2], axis=-1, keepdims=True))
                - jnp.exp(jnp.sum(lam[2:3] * lam[3:4], axis=-1, keepdims=True)) + lambda_init)
    o = acc_ref[0] / l_ref[0] - lam_full * (acc_ref[1] / l_ref[1])
    o = o * lax.rsqrt(jnp.mean(o * o, axis=-1, keepdims=True) + NORM_EPS) * sw_ref[...]
    o_ref[...] = (o * (1.0 - lambda_init)).astype(o_ref.dtype)


def _diff_attention(qkv, lam, subln_w, nct, l_ctx, lambda_init):
    n = qkv.shape[0]
    d = qkv.shape[1] // 3
    vd = subln_w.shape[0]
    n_heads = d // vd
    l_lat = n - l_ctx
    tk = math.gcd(l_lat // 2, ATTN_TK)
    return pl.pallas_call(
        functools.partial(_attn_kernel, tk=tk, nct=nct, l_ctx=l_ctx, n_pairs=l_lat // (2 * tk),
                          lambda_init=lambda_init),
        grid=(n_heads, n // TM),
        in_specs=[pl.BlockSpec((TM, vd), lambda h, m: (m, h)),
                  pl.BlockSpec((n, vd), lambda h, m: (0, n_heads + h)),
                  pl.BlockSpec((n, vd), lambda h, m: (0, 2 * n_heads + h)),
                  pl.BlockSpec(lam.shape, lambda h, m: (0, 0)),
                  pl.BlockSpec((1, vd), lambda h, m: (0, 0))],
        out_specs=pl.BlockSpec((TM, vd), lambda h, m: (m, h)),
        out_shape=jax.ShapeDtypeStruct((n, d), BF16),
        scratch_shapes=[pltpu.VMEM((2, TM, vd), F32), pltpu.VMEM((2, TM, 1), F32), pltpu.VMEM((2, TM, 1), F32),
                        pltpu.VMEM((2, TM, tk), F32), pltpu.VMEM((2, TM, tk), F32)],
        compiler_params=_cparams(("parallel", "parallel")),
        name="diff_attn",
    )(qkv, qkv, qkv, lam, subln_w.reshape(1, vd))


def _moe_kernel(be_ref, nb_ref, x_ref, wgu_ref, bgu_ref, wdn_ref, bdn_ref, y_ref, wgu_s, wdn_s, *, d_ff):
    b = pl.program_id(0)
    changed = jnp.logical_or(b == 0, be_ref[b] != be_ref[jnp.maximum(b - 1, 0)])

    @pl.when(changed)
    def _():
        wgu_s[...] = wgu_ref[0].astype(BF16)
        wdn_s[...] = wdn_ref[0].astype(BF16)

    @pl.when(b < nb_ref[0])
    def _():
        h = _dot(x_ref[...], wgu_s[...]) + bgu_ref[0]
        gate = jnp.minimum(h[:, :d_ff], SWIGLU_LIMIT)
        up = jnp.clip(h[:, d_ff:], -SWIGLU_LIMIT, SWIGLU_LIMIT)
        glu = gate * jax.nn.sigmoid(gate * SWIGLU_ALPHA)
        act = ((up + 1.0) * glu).astype(BF16)
        y_ref[...] = (_dot(act, wdn_s[...]) + bdn_ref[0]).astype(y_ref.dtype)

    @pl.when(b >= nb_ref[0])
    def _():
        y_ref[...] = jnp.zeros_like(y_ref)


def _moe_experts(x_sorted, block_e, n_used, w_gu, b_gu, w_dn, b_dn):
    n_rows, d = x_sorted.shape
    n_e, _, f2 = w_gu.shape
    grid_spec = pltpu.PrefetchScalarGridSpec(
        num_scalar_prefetch=2,
        grid=(n_rows // MOE_BM,),
        in_specs=[pl.BlockSpec((MOE_BM, d), lambda b, be, nb: (b, 0)),
                  pl.BlockSpec((1, d, f2), lambda b, be, nb: (be[b], 0, 0)),
                  pl.BlockSpec((1, 1, f2), lambda b, be, nb: (be[b], 0, 0)),
                  pl.BlockSpec((1, f2 // 2, d), lambda b, be, nb: (be[b], 0, 0)),
                  pl.BlockSpec((1, 1, d), lambda b, be, nb: (be[b], 0, 0))],
        out_specs=pl.BlockSpec((MOE_BM, d), lambda b, be, nb: (b, 0)),
        scratch_shapes=[pltpu.VMEM((d, f2), BF16), pltpu.VMEM((f2 // 2, d), BF16)],
    )
    return pl.pallas_call(
        functools.partial(_moe_kernel, d_ff=f2 // 2),
        grid_spec=grid_spec,
        out_shape=jax.ShapeDtypeStruct((n_rows, d), BF16),
        compiler_params=_cparams(("arbitrary",)),
        name="moe_experts",
    )(block_e, n_used, x_sorted, w_gu, b_gu.reshape(n_e, 1, f2), w_dn, b_dn.reshape(n_e, 1, d))


def _moe_combine_kernel(y0_ref, y1_ref, y2_ref, y3_ref, g_ref, h_ref, mod_ref, lnw_ref, lnb_ref, o_ref, *, alpha):
    g = g_ref[...]
    f = jnp.zeros(h_ref.shape, F32)
    for i, y_ref in enumerate((y0_ref, y1_ref, y2_ref, y3_ref)):
        f = f + g[:, i:i + 1] * y_ref[...].astype(F32)
    x = alpha * h_ref[...] + mod_ref[0, 5:6, :] * f
    o_ref[...] = _layer_norm(x, lnw_ref[...], lnb_ref[...])


def _moe_combine(ys, gates, h1, mod, nct, ln_w, ln_b, alpha):
    n, d = h1.shape
    row = lambda width: pl.BlockSpec((TM, width), lambda m: (m, 0))
    return pl.pallas_call(
        functools.partial(_moe_combine_kernel, alpha=alpha),
        grid=(n // TM,),
        in_specs=[row(d)] * TOP_K + [row(TOP_K), row(d),
                                     pl.BlockSpec((1, 6, d), lambda m: (jnp.where(m < nct, 0, 1), 0, 0)),
                                     pl.BlockSpec((1, d), lambda m: (0, 0)), pl.BlockSpec((1, d), lambda m: (0, 0))],
        out_specs=row(d),
        out_shape=jax.ShapeDtypeStruct((n, d), F32),
        compiler_params=_cparams(("parallel",)),
        name="moe_combine_ln",
    )(*ys, gates, h1, mod, ln_w.reshape(1, d), ln_b.reshape(1, d))


def _moe(v_bf16, logits, n_e, h1, mod, nct, ln_w, ln_b, alpha, w_gu, b_gu, w_dn, b_dn):
    n, d = v_bf16.shape
    top_val, top_idx = lax.top_k(logits[:, :n_e], TOP_K)
    gates = jax.nn.softmax(top_val, axis=-1)
    nk = n * TOP_K
    flat_e = top_idx.reshape(nk)
    order = jnp.argsort(flat_e)
    sorted_e = flat_e[order]
    counts = jnp.sum(flat_e[:, None] == jnp.arange(n_e, dtype=flat_e.dtype)[None, :], axis=0, dtype=jnp.int32)
    padded = (counts + MOE_BM - 1) // MOE_BM * MOE_BM
    pad_end = jnp.cumsum(padded)
    pad_start = pad_end - padded
    start = jnp.cumsum(counts) - counts
    dest_sorted = pad_start[sorted_e] + (jnp.arange(nk, dtype=jnp.int32) - start[sorted_e])
    dest = jnp.zeros((nk,), jnp.int32).at[order].set(dest_sorted)
    n_blocks = -(-nk // MOE_BM) + n_e
    n_rows = n_blocks * MOE_BM
    row_tok = jnp.zeros((n_rows,), jnp.int32).at[dest].set(jnp.arange(nk, dtype=jnp.int32) // TOP_K)
    block_e = jnp.minimum(jnp.searchsorted(pad_end, jnp.arange(n_blocks, dtype=jnp.int32) * MOE_BM, side='right'),
                          n_e - 1).astype(jnp.int32)
    n_used = (pad_end[-1:] // MOE_BM).astype(jnp.int32)
    x_sorted = v_bf16[row_tok]
    yb = _moe_experts(x_sorted, block_e, n_used, w_gu, b_gu, w_dn, b_dn)
    dest2 = dest.reshape(n, TOP_K)
    ys = [yb[dest2[:, i]] for i in range(TOP_K)]
    return _moe_combine(ys, gates, h1, mod, nct, ln_w, ln_b, alpha)


def _rope_tables(l_ctx, l_lat):
    rows = l_lat // GRID_W
    row = jnp.repeat(jnp.arange(rows, dtype=F32), GRID_W)
    col = jnp.tile(jnp.arange(GRID_W, dtype=F32), rows)
    axis_dim = DIFF_HEAD_DIM // 2
    inv_freq = ROPE_THETA ** (-jnp.arange(0, axis_dim, 2, dtype=F32) / axis_dim)
    ang_r = row[:, None] * inv_freq
    ang_c = col[:, None] * inv_freq
    ang = jnp.concatenate([ang_r, ang_r, ang_c, ang_c], -1)
    ang = jnp.concatenate([jnp.zeros((l_ctx, DIFF_HEAD_DIM), F32), ang], 0)
    ang = jnp.concatenate([ang, ang], -1)
    return jnp.cos(ang), jnp.sin(ang)


def _gdn_scalars(gates, n_vh):
    n = gates.shape[0]
    n_qk = n_vh // 2
    nc = n // GDN_CHUNK
    gc = gates[:, :2 * n_vh].reshape(n, 2, n_qk, 2)
    beta = gates[:, 2 * n_vh:4 * n_vh].reshape(n, 2, n_qk, 2)
    gcc = gc.reshape(nc, GDN_CHUNK, 2, n_qk, 2)
    gl = jnp.stack([gcc[:, -1, 0], gcc[:, 0, 1]], axis=1)
    gl_tok = jnp.repeat(gl, GDN_CHUNK, axis=0)
    col = jnp.concatenate([gc, beta, gl_tok], axis=-1)
    scol = jnp.transpose(col, (1, 2, 0, 3))
    srow = jnp.transpose(col[..., :4], (1, 2, 3, 0))
    srowc = jnp.transpose(col[..., :4].reshape(nc, GDN_CHUNK, 2, n_qk, 4), (2, 3, 0, 4, 1))
    glrow = jnp.broadcast_to(jnp.transpose(gl, (1, 2, 0, 3))[..., None], (2, n_qk, nc, 2, GDN_HEAD_DIM))
    glrow = glrow.reshape(2, n_qk, nc, 1, 2 * GDN_HEAD_DIM)
    return scol, srow, srowc, glrow


def kernel(x, c, ctx, c_ctx, ada_w, ada_b, ln_w, ln_b, gdn_w_in, gdn_conv_w, gdn_a_log, gdn_dt_bias, gdn_norm_w,
           gdn_w_out, diff_w_in, diff_lambda, diff_subln_w, diff_w_out, router_w, router_b, moe_w_gate_up,
           moe_b_gate_up, moe_w_down, moe_b_down):
    batch, l_lat, d = x.shape
    l_ctx = ctx.shape[1]
    depth = ada_w.shape[0]
    assert batch == 1 and l_ctx % TM == 0 and l_lat % TM == 0 and l_lat % GRID_W == 0
    nct = l_ctx // TM
    n_e = router_w.shape[-1]
    alpha = (2 * depth) ** 0.25
    n_vh = gdn_a_log.shape[-1]
    v_dim = n_vh * GDN_HEAD_DIM
    conv_dim = gdn_conv_w.shape[-1]
    qk_dim = (conv_dim - v_dim) // 2
    n_qk = qk_dim // GDN_HEAD_DIM

    h = jnp.concatenate([ctx[0], x[0]], axis=0)
    n = h.shape[0]
    mods = _modulations(c_ctx, c, ada_w, ada_b)
    cos, sin = _rope_tables(l_ctx, l_lat)

    for i in range(depth):
        j = i // 2
        mod = mods[i]
        if i % 2 == 0:
            w_in = gdn_w_in[j]
            w_main = w_in[:, :conv_dim + v_dim].astype(BF16)
            w_ab = jnp.zeros((d, LANES), F32).at[:, :4 * n_vh].set(w_in[:, conv_dim + v_dim:]).astype(BF16)
            p, ab = _project(h, mod, nct, [w_main, w_ab], [BF16, F32])
            gates = _gdn_gates(ab, gdn_a_log[j], gdn_dt_bias[j])
            feat = _gdn_features(p, gdn_conv_w[j], nct, qk_dim)
            kt = jnp.transpose(feat[:, qk_dim:2 * qk_dim].reshape(n // GDN_CHUNK, GDN_CHUNK, n_qk, GDN_HEAD_DIM),
                               (2, 0, 3, 1))
            scol, srow, srowc, glrow = _gdn_scalars(gates, n_vh)
            o_f = _gdn_scan(feat, kt, scol, srow, srowc, glrow, nct, False)
            o_b = _gdn_scan(feat, kt, scol, srow, srowc, glrow, nct, True)
            acts = (o_f, o_b, p, gdn_norm_w[j], conv_dim // v_dim)
            h1, v, logits = _mixer_out("gdn", acts, gdn_w_out[j].astype(BF16), h, mod, nct, ln_w[i, 0], ln_b[i, 0],
                                       router_w[i], router_b[i], alpha)
        else:
            lambda_init = 0.8 - 0.6 * math.exp(-0.3 * i)
            qkv = _qkv_rope(h, mod, nct, diff_w_in[j].astype(BF16), cos, sin)
            a = _diff_attention(qkv, diff_lambda[j], diff_subln_w[j], nct, l_ctx, lambda_init)
            h1, v, logits = _mixer_out("attn", (a,), diff_w_out[j].astype(BF16), h, mod, nct, ln_w[i, 0], ln_b[i, 0],
                                       router_w[i], router_b[i], alpha)
        h = _moe(v, logits, n_e, h1, mod, nct, ln_w[i, 1], ln_b[i, 1], alpha,
                 moe_w_gate_up[i], moe_b_gate_up[i], moe_w_down[i], moe_b_down[i])
    return h[l_ctx:].reshape(batch, l_lat, d)
```

```python
import functools
import math

import jax
import jax.numpy as jnp
from jax import lax
from jax.experimental import pallas as pl
from jax.experimental.pallas import tpu as pltpu
from jax.experimental.pallas import tpu_sc as plsc

F32 = jnp.float32
BF16 = jnp.bfloat16
HIGHEST = lax.Precision.HIGHEST

GRID_W = 64
GDN_HEAD_DIM = 128
GDN_CHUNK = 64
DIFF_HEAD_DIM = 64
ROPE_THETA = 10000.0
TOP_K = 4
SWIGLU_LIMIT = 7.0
SWIGLU_ALPHA = 1.702
NORM_EPS = 1e-5
L2_EPS = 1e-6

LANES = 128
SUBLANES = 8
HALO = 16
VMEM_LIMIT = 56 * 1024 * 1024

TM = 256
MOE_BM = 256
ATTN_TK = 1024

SC_CORES = 2
SC_SUBCORES = 16
SC_GATHER_ROWS = 32


def _cparams(sem):
    return pltpu.CompilerParams(dimension_semantics=sem, vmem_limit_bytes=VMEM_LIMIT)


def _silu(x):
    return x * jax.nn.sigmoid(x)


def _dot(a, b):
    return jnp.dot(a, b, preferred_element_type=F32)


def _dot_nt(a, b):
    return lax.dot_general(a, b, (((1,), (1,)), ((), ())), preferred_element_type=F32)


def _modulate(h, mod_ref, shift, scale):
    return h * (1.0 + mod_ref[0, scale:scale + 1, :]) + mod_ref[0, shift:shift + 1, :]


def _layer_norm(x, w, b):
    mu = jnp.mean(x, axis=-1, keepdims=True)
    xc = x - mu
    var = jnp.mean(xc * xc, axis=-1, keepdims=True)
    return xc * lax.rsqrt(var + NORM_EPS) * w + b


def _mod_kernel(s_ref, w_ref, b_ref, o_ref):
    s = _silu(s_ref[...])
    o_ref[0] = jnp.dot(s, w_ref[0], preferred_element_type=F32, precision=HIGHEST) + b_ref[0]


def _modulations(c_ctx, c, ada_w, ada_b):
    depth, d, d6 = ada_w.shape
    tn = d6 // 4
    s = jnp.zeros((SUBLANES, d), F32).at[0].set(c_ctx).at[1].set(c[0])
    out = pl.pallas_call(
        _mod_kernel,
        grid=(depth, d6 // tn),
        in_specs=[pl.BlockSpec((SUBLANES, d), lambda i, n: (0, 0)),
                  pl.BlockSpec((1, d, tn), lambda i, n: (i, 0, n)),
                  pl.BlockSpec((1, 1, tn), lambda i, n: (i, 0, n))],
        out_specs=pl.BlockSpec((1, SUBLANES, tn), lambda i, n: (i, 0, n)),
        out_shape=jax.ShapeDtypeStruct((depth, SUBLANES, d6), F32),
        compiler_params=_cparams(("parallel", "parallel")),
        name="adaln_mod",
    )(s, ada_w, ada_b.reshape(depth, 1, d6))
    return out[:, :2].reshape(depth, 2, 6, d)


def _proj_kernel(h_ref, mod_ref, *refs, n_w, chunk):
    w_refs, o_refs = refs[:n_w], refs[n_w:]
    u = _modulate(h_ref[...], mod_ref, 0, 1).astype(BF16)
    for w_ref, o_ref in zip(w_refs, o_refs):
        n = w_ref.shape[1]
        for j in range(0, n, chunk):
            jc = min(chunk, n - j)
            o_ref[:, j:j + jc] = _dot(u, w_ref[:, j:j + jc]).astype(o_ref.dtype)


def _project(h, mod, nct, weights, out_dtypes):
    n, d = h.shape
    return pl.pallas_call(
        functools.partial(_proj_kernel, n_w=len(weights), chunk=512),
        grid=(n // TM,),
        in_specs=[pl.BlockSpec((TM, d), lambda m: (m, 0)),
                  pl.BlockSpec((1, 6, d), lambda m: (jnp.where(m < nct, 0, 1), 0, 0))]
                 + [pl.BlockSpec(w.shape, lambda m: (0, 0)) for w in weights],
        out_specs=[pl.BlockSpec((TM, w.shape[1]), lambda m: (m, 0)) for w in weights],
        out_shape=[jax.ShapeDtypeStruct((n, w.shape[1]), dt) for w, dt in zip(weights, out_dtypes)],
        compiler_params=_cparams(("parallel",)),
        name="mod_proj",
    )(h, mod, *weights)


def _qkv_rope_kernel(h_ref, mod_ref, w_ref, cos_ref, sin_ref, o_ref, *, chunk, n_rope, n_q, q_scale):
    u = _modulate(h_ref[...], mod_ref, 0, 1).astype(BF16)
    rep = chunk // LANES
    cos = jnp.concatenate([cos_ref[...]] * rep, axis=1)
    sin = jnp.concatenate([sin_ref[...]] * rep, axis=1)
    lane = lax.broadcasted_iota(jnp.int32, (h_ref.shape[0], chunk), 1)
    first = (lane % 32) < 16
    for j in range(w_ref.shape[1] // chunk):
        y = _dot(u, w_ref[:, j * chunk:(j + 1) * chunk])
        if j < n_rope:
            rot = jnp.where(first, -pltpu.roll(y, chunk - 16, 1), pltpu.roll(y, 16, 1))
            y = y * cos + rot * sin
            if j < n_q:
                y = y * q_scale
        o_ref[:, j * chunk:(j + 1) * chunk] = y.astype(o_ref.dtype)


def _qkv_rope(h, mod, nct, w, cos, sin):
    n, d = h.shape
    chunk = 512
    return pl.pallas_call(
        functools.partial(_qkv_rope_kernel, chunk=chunk, n_rope=2 * d // chunk, n_q=d // chunk,
                          q_scale=DIFF_HEAD_DIM ** -0.5 * math.log2(math.e)),
        grid=(n // TM,),
        in_specs=[pl.BlockSpec((TM, d), lambda m: (m, 0)),
                  pl.BlockSpec((1, 6, d), lambda m: (jnp.where(m < nct, 0, 1), 0, 0)),
                  pl.BlockSpec(w.shape, lambda m: (0, 0)),
                  pl.BlockSpec((TM, LANES), lambda m: (m, 0)),
                  pl.BlockSpec((TM, LANES), lambda m: (m, 0))],
        out_specs=pl.BlockSpec((TM, w.shape[1]), lambda m: (m, 0)),
        out_shape=jax.ShapeDtypeStruct((n, w.shape[1]), BF16),
        compiler_params=_cparams(("parallel",)),
        name="qkv_rope",
    )(h, mod, w, cos, sin)


def _gdn_gates_kernel(ab_ref, alog_ref, dtb_ref, o_ref, *, chunk, n_heads):
    x = ab_ref[...]
    t = x + dtb_ref[...]
    softplus = jnp.maximum(t, 0.0) + jnp.log1p(jnp.exp(-jnp.abs(t)))
    g = -jnp.exp(alog_ref[...]) * softplus
    beta = jax.nn.sigmoid(x)
    tm = x.shape[0]
    r = lax.broadcasted_iota(jnp.int32, (tm, tm), 0)
    c = lax.broadcasted_iota(jnp.int32, (tm, tm), 1)
    same = (r // chunk) == (c // chunk)
    t_fwd = jnp.where(same & (r >= c), 1.0, 0.0).astype(F32)
    t_bwd = jnp.where(same & (r <= c), 1.0, 0.0).astype(F32)
    g_fwd = jnp.dot(t_fwd, g, preferred_element_type=F32, precision=HIGHEST)
    g_bwd = jnp.dot(t_bwd, g, preferred_element_type=F32, precision=HIGHEST)
    lane = lax.broadcasted_iota(jnp.int32, x.shape, 1)
    o_ref[...] = jnp.where(lane < n_heads, g_fwd, jnp.where(lane < 2 * n_heads, g_bwd, beta))


def _gdn_gates(ab, a_log, dt_bias):
    n = ab.shape[0]
    nh = a_log.shape[-1]
    pad = lambda v: jnp.zeros((1, LANES), F32).at[0, :2 * nh].set(v.reshape(-1))
    return pl.pallas_call(
        functools.partial(_gdn_gates_kernel, chunk=GDN_CHUNK, n_heads=nh),
        grid=(n // TM,),
        in_specs=[pl.BlockSpec((TM, LANES), lambda m: (m, 0)),
                  pl.BlockSpec((1, LANES), lambda m: (0, 0)),
                  pl.BlockSpec((1, LANES), lambda m: (0, 0))],
        out_specs=pl.BlockSpec((TM, LANES), lambda m: (m, 0)),
        out_shape=jax.ShapeDtypeStruct((n, LANES), F32),
        compiler_params=_cparams(("parallel",)),
        name="gdn_gates",
    )(ab, pad(a_log), pad(dt_bias))


def _gdn_feat_kernel(x_ref, prev_ref, next_ref, w_ref, o_ref, xs_ref, *, nct, n_tiles, n_norm, n_q, q_scale, width):
    m = pl.program_id(0)
    j = pl.program_id(1)
    tm, tn = x_ref.shape
    pad = width // 2
    prev_ok = jnp.logical_and(m != 0, m != nct)
    next_ok = jnp.logical_and(m != nct - 1, m != n_tiles - 1)
    xs_ref[0:HALO, :] = jnp.where(prev_ok, prev_ref[...].astype(F32), 0.0)
    xs_ref[HALO:HALO + tm, :] = x_ref[...].astype(F32)
    xs_ref[HALO + tm:, :] = jnp.where(next_ok, next_ref[...].astype(F32), 0.0)
    acc = jnp.zeros((tm, tn), F32)
    for t in range(width):
        acc = acc + w_ref[t:t + 1, :] * xs_ref[HALO - pad + t:HALO - pad + t + tm, :]
    y = _silu(acc)

    @pl.when(j >= n_norm)
    def _():
        o_ref[...] = y.astype(o_ref.dtype)

    @pl.when(j < n_norm)
    def _():
        scale = jnp.where(j < n_q, q_scale, 1.0).astype(F32)
        for hh in range(tn // GDN_HEAD_DIM):
            sl = slice(hh * GDN_HEAD_DIM, (hh + 1) * GDN_HEAD_DIM)
            yh = y[:, sl]
            inv = lax.rsqrt(jnp.sum(yh * yh, axis=-1, keepdims=True) + L2_EPS)
            o_ref[:, sl] = (yh * (inv * scale)).astype(o_ref.dtype)


def _gdn_features(p, conv_w, nct, qk_dim):
    n = p.shape[0]
    width, conv_dim = conv_w.shape
    tn = 512
    n_tiles = n // TM
    rb = TM // HALO
    last_rb = n // HALO - 1
    return pl.pallas_call(
        functools.partial(_gdn_feat_kernel, nct=nct, n_tiles=n_tiles, n_norm=2 * qk_dim // tn, n_q=qk_dim // tn,
                          q_scale=GDN_HEAD_DIM ** -0.5, width=width),
        grid=(n_tiles, conv_dim // tn),
        in_specs=[pl.BlockSpec((TM, tn), lambda m, j: (m, j)),
                  pl.BlockSpec((HALO, tn), lambda m, j: (jnp.maximum(m * rb - 1, 0), j)),
                  pl.BlockSpec((HALO, tn), lambda m, j: (jnp.minimum((m + 1) * rb, last_rb), j)),
                  pl.BlockSpec((width, tn), lambda m, j: (0, j))],
        out_specs=pl.BlockSpec((TM, tn), lambda m, j: (m, j)),
        out_shape=jax.ShapeDtypeStruct((n, conv_dim), BF16),
        scratch_shapes=[pltpu.VMEM((TM + 2 * HALO, tn), F32)],
        compiler_params=_cparams(("parallel", "parallel")),
        name="gdn_features",
    )(p, p, p, conv_w)


def _gdn_scan_kernel(q_ref, k_ref, kt_ref, v_ref, scol_ref, srow_ref, srowc_ref, glrow_ref, o_ref, s_ref,
                     *, reverse, chunk):
    t = pl.program_id(1)

    @pl.when(t == 0)
    def _():
        s_ref[...] = jnp.zeros_like(s_ref)

    rows = q_ref.shape[0]
    hd = GDN_HEAD_DIM
    n_chunk = rows // chunk
    q = q_ref[...]
    k = k_ref[...]
    v = v_ref[...]
    scol = scol_ref[0, 0]
    srow = srow_ref[0, 0]

    ri = lax.broadcasted_iota(jnp.int32, (rows, rows), 0)
    ci = lax.broadcasted_iota(jnp.int32, (rows, rows), 1)
    same = (ri // chunk) == (ci // chunk)
    if reverse:
        incl, strict = same & (ri <= ci), same & (ri < ci)
    else:
        incl, strict = same & (ri >= ci), same & (ri > ci)
    eye = jnp.where(ri == ci, 1.0, 0.0).astype(F32)
    r64 = lax.broadcasted_iota(jnp.int32, (chunk, chunk), 0)
    c64 = lax.broadcasted_iota(jnp.int32, (chunk, chunk), 1)
    incl64 = (r64 <= c64) if reverse else (r64 >= c64)

    kk = _dot_nt(k, k)
    n_double = int(math.log2(chunk)) - 1
    order = range(n_chunk - 1, -1, -1) if reverse else range(n_chunk)
    pair = range(2)
    gcol = [scol[:, a:a + 1] for a in pair]
    bcol = [scol[:, 2 + a:3 + a] for a in pair]
    e_g = [jnp.exp(g) for g in gcol]
    e_k = [jnp.exp(scol[:, 4 + a:5 + a] - gcol[a]) for a in pair]
    neg_l = [jnp.where(strict, -(bcol[a] * kk * jnp.exp(jnp.where(incl, gcol[a] - srow[a:a + 1, :], -jnp.inf))), 0.0)
             for a in pair]
    inv = [eye + x for x in neg_l]
    pw = [x.astype(BF16) for x in neg_l]
    for _ in range(n_double):
        pw = [_dot(x, x).astype(BF16) for x in pw]
        inv = [inv[a] + _dot(inv[a].astype(BF16), pw[a]) for a in pair]
    rhs = [jnp.concatenate([v[:, a * hd:(a + 1) * hd].astype(F32) * bcol[a],
                            k.astype(F32) * (bcol[a] * e_g[a])], axis=1).astype(BF16) for a in pair]
    uw = [_dot(inv[a].astype(BF16), rhs[a]) for a in pair]

    state = s_ref[...]
    for j in order:
        rs = slice(j * chunk, (j + 1) * chunk)
        s_b = state.astype(BF16)
        qk_c = _dot_nt(q[rs], k[rs])
        q_s = _dot(q[rs], s_b)
        v_new = [uw[a][rs, :hd] - _dot(uw[a][rs, hd:].astype(BF16), s_b[:, a * hd:(a + 1) * hd]) for a in pair]
        qkd = [(qk_c * jnp.exp(jnp.where(incl64, gcol[a][rs] - srowc_ref[0, 0, j, a:a + 1, :], -jnp.inf))).astype(BF16)
               for a in pair]
        for a in pair:
            o = e_g[a][rs] * q_s[:, a * hd:(a + 1) * hd] + _dot(qkd[a], v_new[a].astype(BF16))
            o_ref[0, rs, a * hd:(a + 1) * hd] = o.astype(o_ref.dtype)
        v_s = jnp.concatenate([v_new[a] * e_k[a][rs] for a in pair], axis=1).astype(BF16)
        e_l = jnp.exp(glrow_ref[0, 0, j])
        state = state * e_l + _dot(kt_ref[0, j], v_s)
    s_ref[...] = state


def _gdn_scan(feat, kt, scol, srow, srowc, glrow, nct, reverse):
    n = feat.shape[0]
    hd = GDN_HEAD_DIM
    n_qk = kt.shape[0]
    n_tiles = n // TM
    cpt = TM // GDN_CHUNK

    def tile(t):
        if not reverse:
            return t
        return jnp.where(t < nct, nct - 1 - t, n_tiles - 1 - (t - nct))

    d = 1 if reverse else 0
    return pl.pallas_call(
        functools.partial(_gdn_scan_kernel, reverse=reverse, chunk=GDN_CHUNK),
        grid=(n_qk, n_tiles),
        in_specs=[pl.BlockSpec((TM, hd), lambda h, t: (tile(t), h)),
                  pl.BlockSpec((TM, hd), lambda h, t: (tile(t), n_qk + h)),
                  pl.BlockSpec((1, cpt, hd, GDN_CHUNK), lambda h, t: (h, tile(t), 0, 0)),
                  pl.BlockSpec((TM, 2 * hd), lambda h, t: (tile(t), n_qk + h)),
                  pl.BlockSpec((1, 1, TM, 6), lambda h, t: (d, h, tile(t), 0)),
                  pl.BlockSpec((1, 1, 4, TM), lambda h, t: (d, h, 0, tile(t))),
                  pl.BlockSpec((1, 1, cpt, 4, GDN_CHUNK), lambda h, t: (d, h, tile(t), 0, 0)),
                  pl.BlockSpec((1, 1, cpt, 1, 2 * hd), lambda h, t: (d, h, tile(t), 0, 0))],
        out_specs=pl.BlockSpec((1, TM, 2 * hd), lambda h, t: (0, tile(t), h)),
        out_shape=jax.ShapeDtypeStruct((1, n, 2 * n_qk * hd), BF16),
        scratch_shapes=[pltpu.VMEM((hd, 2 * hd), F32)],
        compiler_params=_cparams(("parallel", "arbitrary")),
        name="gdn_scan_bwd" if reverse else "gdn_scan_fwd",
    )(feat, feat, kt, feat, scol, srow, srowc, glrow)[0]


def _post_mixer(y, h_ref, mod_ref, lnw_ref, lnb_ref, rw_ref, rb_ref, h1_ref, v_ref, rt_ref, cnt_ref, carry_ref,
                alpha, n_e):
    @pl.when(pl.program_id(0) == 0)
    def _():
        carry_ref[...] = jnp.zeros_like(carry_ref)

    x = alpha * h_ref[...] + mod_ref[0, 2:3, :] * y
    h1 = _layer_norm(x, lnw_ref[...], lnb_ref[...])
    h1_ref[...] = h1
    v = _modulate(h1, mod_ref, 3, 4)
    v_ref[...] = v.astype(v_ref.dtype)
    logits = jnp.dot(v, rw_ref[...], preferred_element_type=F32, precision=HIGHEST) + rb_ref[...]

    tm = logits.shape[0]
    lane = lax.broadcasted_iota(jnp.int32, logits.shape, 1)
    rest = jnp.where(lane < n_e, logits, -jnp.inf)
    top_val, top_idx, picked = [], [], []
    for _ in range(TOP_K):
        mx = jnp.max(rest, axis=-1, keepdims=True)
        idx = jnp.min(jnp.where(rest == mx, lane, LANES), axis=-1, keepdims=True)
        hit = lane == idx
        top_val.append(mx)
        top_idx.append(idx)
        picked.append(hit)
        rest = jnp.where(hit, -jnp.inf, rest)
    e = [jnp.exp(tv - top_val[0]) for tv in top_val]
    denom = functools.reduce(lambda a, b: a + b, e)
    onehot = functools.reduce(lambda a, b: a + b, [jnp.where(hit, 1.0, 0.0) for hit in picked])
    r = lax.broadcasted_iota(jnp.int32, (tm, tm), 0)
    c = lax.broadcasted_iota(jnp.int32, (tm, tm), 1)
    earlier = jnp.where(r > c, 1.0, 0.0).astype(BF16)
    before = _dot(earlier, onehot.astype(BF16)) + carry_ref[0:1, :]
    rt = jnp.zeros(logits.shape, F32)
    for k in range(TOP_K):
        rank = jnp.sum(jnp.where(picked[k], before, 0.0), axis=-1, keepdims=True)
        rt = jnp.where(lane == k, top_idx[k].astype(F32), rt)
        rt = jnp.where(lane == TOP_K + k, rank, rt)
        rt = jnp.where(lane == 2 * TOP_K + k, e[k] / denom, rt)
    rt_ref[...] = rt
    total = carry_ref[0:1, :] + jnp.sum(onehot, axis=0, keepdims=True)
    carry_ref[...] = jnp.broadcast_to(total, carry_ref.shape)
    cnt_ref[...] = jnp.broadcast_to(total, cnt_ref.shape)


def _gdn_out_kernel(of_ref, ob_ref, z_ref, nw_ref, w_ref, h_ref, mod_ref, lnw_ref, lnb_ref, rw_ref, rb_ref,
                    h1_ref, v_ref, rt_ref, cnt_ref, carry_ref, a_ref, *, alpha, n_e):
    hd = GDN_HEAD_DIM
    for hh in range(of_ref.shape[1] // hd):
        sl = slice(hh * hd, (hh + 1) * hd)
        o = of_ref[:, sl].astype(F32) + ob_ref[:, sl].astype(F32)
        o = o * lax.rsqrt(jnp.mean(o * o, axis=-1, keepdims=True) + NORM_EPS) * nw_ref[...]
        a_ref[:, sl] = (o * _silu(z_ref[:, sl].astype(F32))).astype(BF16)
    y = _dot(a_ref[...], w_ref[...])
    _post_mixer(y, h_ref, mod_ref, lnw_ref, lnb_ref, rw_ref, rb_ref, h1_ref, v_ref, rt_ref, cnt_ref, carry_ref,
                alpha, n_e)


def _attn_out_kernel(a_ref, w_ref, h_ref, mod_ref, lnw_ref, lnb_ref, rw_ref, rb_ref, h1_ref, v_ref, rt_ref, cnt_ref,
                     carry_ref, *, alpha, n_e):
    y = _dot(a_ref[...], w_ref[...])
    _post_mixer(y, h_ref, mod_ref, lnw_ref, lnb_ref, rw_ref, rb_ref, h1_ref, v_ref, rt_ref, cnt_ref, carry_ref,
                alpha, n_e)


def _mixer_out(kind, acts, w_out, h, mod, nct, ln_w, ln_b, router_w, router_b, alpha):
    n, d = h.shape
    row = lambda width: pl.BlockSpec((TM, width), lambda m: (m, 0))
    full = lambda arr: pl.BlockSpec(arr.shape, lambda m: (0,) * arr.ndim)
    n_e = router_w.shape[1]
    rw = jnp.zeros((d, LANES), F32).at[:, :n_e].set(router_w)
    rb = jnp.zeros((1, LANES), F32).at[0, :n_e].set(router_b)
    tail = [w_out, h, mod, ln_w.reshape(1, d), ln_b.reshape(1, d), rw, rb]
    tail_specs = [full(w_out), row(d), pl.BlockSpec((1, 6, d), lambda m: (jnp.where(m < nct, 0, 1), 0, 0)),
                  pl.BlockSpec((1, d), lambda m: (0, 0)), pl.BlockSpec((1, d), lambda m: (0, 0)), full(rw), full(rb)]
    if kind == "gdn":
        o_f, o_b, p, norm_w, z_col = acts
        v_dim = o_f.shape[1]
        body = functools.partial(_gdn_out_kernel, alpha=alpha, n_e=n_e)
        args = [o_f, o_b, p, norm_w.reshape(1, -1)] + tail
        specs = [row(v_dim), row(v_dim), pl.BlockSpec((TM, v_dim), lambda m: (m, z_col)),
                 pl.BlockSpec((1, norm_w.shape[0]), lambda m: (0, 0))] + tail_specs
        scratch = [pltpu.VMEM((SUBLANES, LANES), F32), pltpu.VMEM((TM, v_dim), BF16)]
    else:
        (a,) = acts
        body = functools.partial(_attn_out_kernel, alpha=alpha, n_e=n_e)
        args = [a] + tail
        specs = [row(a.shape[1])] + tail_specs
        scratch = [pltpu.VMEM((SUBLANES, LANES), F32)]
    return pl.pallas_call(
        body,
        grid=(n // TM,),
        in_specs=specs,
        out_specs=[row(d), row(d), row(LANES), pl.BlockSpec((SUBLANES, LANES), lambda m: (0, 0))],
        out_shape=[jax.ShapeDtypeStruct((n, d), F32), jax.ShapeDtypeStruct((n, d), F32),
                   jax.ShapeDtypeStruct((n, LANES), F32), jax.ShapeDtypeStruct((SUBLANES, LANES), F32)],
        scratch_shapes=scratch,
        compiler_params=_cparams(("arbitrary",)),
        name=kind + "_out_ln",
    )(*args)


def _attn_kernel(q_ref, k_ref, v_ref, lam_ref, sw_ref, o_ref, acc_ref, m_ref, l_ref, sa_ref, sb_ref,
                 *, tk, nct, l_ctx, n_pairs, lambda_init):
    mt = pl.program_id(1)
    q = q_ref[...]
    lane = lax.broadcasted_iota(jnp.int32, q.shape, 1)
    zero = jnp.zeros_like(q)
    q_maps = (jnp.where(lane < DIFF_HEAD_DIM, q, zero), jnp.where(lane >= DIFF_HEAD_DIM, q, zero))
    acc_ref[...] = jnp.zeros_like(acc_ref)
    m_ref[...] = jnp.full_like(m_ref, -jnp.inf)
    l_ref[...] = jnp.zeros_like(l_ref)

    def scores(start, s_ref):
        kj = k_ref[pl.ds(start, tk), :]
        for i, qm in enumerate(q_maps):
            s_ref[i] = _dot_nt(qm, kj)

    def absorb(score_of_map, vj):
        for i in range(2):
            s = score_of_map(i)
            m_old = m_ref[i]
            m_new = jnp.maximum(m_old, jnp.max(s, axis=-1, keepdims=True))
            alpha = jnp.exp2(m_old - m_new)
            p = jnp.exp2(s - m_new)
            l_ref[i] = alpha * l_ref[i] + jnp.sum(p, axis=-1, keepdims=True)
            acc_ref[i] = alpha * acc_ref[i] + _dot(p.astype(BF16), vj)
            m_ref[i] = m_new

    k_ctx = k_ref[0:l_ctx, :]
    absorb(lambda i: _dot_nt(q_maps[i], k_ctx), v_ref[0:l_ctx, :])

    @pl.when(mt >= nct)
    def _():
        last = l_ctx + (2 * n_pairs - 1) * tk
        scores(l_ctx, sa_ref)

        def body(jj, carry):
            c0 = pl.multiple_of(l_ctx + 2 * jj * tk, LANES)
            c1 = pl.multiple_of(c0 + tk, LANES)
            c2 = pl.multiple_of(jnp.minimum(c1 + tk, last), LANES)
            scores(c1, sb_ref)
            absorb(lambda i: sa_ref[i], v_ref[pl.ds(c0, tk), :])
            scores(c2, sa_ref)
            absorb(lambda i: sb_ref[i], v_ref[pl.ds(c1, tk), :])
            return carry

        lax.fori_loop(0, n_pairs, body, 0)

    lam = lam_ref[...]
    lam_full = (jnp.exp(jnp.sum(lam[0:1] * lam[1:2], axis=-1, keepdims=True))
                - jnp.exp(jnp.sum(lam[2:3] * lam[3:4], axis=-1, keepdims=True)) + lambda_init)
    o = acc_ref[0] / l_ref[0] - lam_full * (acc_ref[1] / l_ref[1])
    o = o * lax.rsqrt(jnp.mean(o * o, axis=-1, keepdims=True) + NORM_EPS) * sw_ref[...]
    o_ref[...] = (o * (1.0 - lambda_init)).astype(o_ref.dtype)


def _diff_attention(qkv, lam, subln_w, nct, l_ctx, lambda_init):
    n = qkv.shape[0]
    d = qkv.shape[1] // 3
    vd = subln_w.shape[0]
    n_heads = d // vd
    l_lat = n - l_ctx
    tk = math.gcd(l_lat // 2, ATTN_TK)
    return pl.pallas_call(
        functools.partial(_attn_kernel, tk=tk, nct=nct, l_ctx=l_ctx, n_pairs=l_lat // (2 * tk),
                          lambda_init=lambda_init),
        grid=(n_heads, n // TM),
        in_specs=[pl.BlockSpec((TM, vd), lambda h, m: (m, h)),
                  pl.BlockSpec((n, vd), lambda h, m: (0, n_heads + h)),
                  pl.BlockSpec((n, vd), lambda h, m: (0, 2 * n_heads + h)),
                  pl.BlockSpec(lam.shape, lambda h, m: (0, 0)),
                  pl.BlockSpec((1, vd), lambda h, m: (0, 0))],
        out_specs=pl.BlockSpec((TM, vd), lambda h, m: (m, h)),
        out_shape=jax.ShapeDtypeStruct((n, d), BF16),
        scratch_shapes=[pltpu.VMEM((2, TM, vd), F32), pltpu.VMEM((2, TM, 1), F32), pltpu.VMEM((2, TM, 1), F32),
                        pltpu.VMEM((2, TM, tk), F32), pltpu.VMEM((2, TM, tk), F32)],
        compiler_params=_cparams(("parallel", "parallel")),
        name="diff_attn",
    )(qkv, qkv, qkv, lam, subln_w.reshape(1, vd))


def _moe_kernel(be_ref, nb_ref, x_ref, wgu_ref, bgu_ref, wdn_ref, bdn_ref, y_ref, wgu_s, wdn_s, *, d_ff):
    b = pl.program_id(0)
    changed = jnp.logical_or(b == 0, be_ref[b] != be_ref[jnp.maximum(b - 1, 0)])

    @pl.when(changed)
    def _():
        wgu_s[...] = wgu_ref[0].astype(BF16)
        wdn_s[...] = wdn_ref[0].astype(BF16)

    @pl.when(b < nb_ref[0])
    def _():
        h = _dot(x_ref[...].astype(BF16), wgu_s[...]) + bgu_ref[0]
        gate = jnp.minimum(h[:, :d_ff], SWIGLU_LIMIT)
        up = jnp.clip(h[:, d_ff:], -SWIGLU_LIMIT, SWIGLU_LIMIT)
        glu = gate * jax.nn.sigmoid(gate * SWIGLU_ALPHA)
        act = ((up + 1.0) * glu).astype(BF16)
        y_ref[...] = (_dot(act, wdn_s[...]) + bdn_ref[0]).astype(y_ref.dtype)

    @pl.when(b >= nb_ref[0])
    def _():
        y_ref[...] = jnp.zeros_like(y_ref)


def _moe_experts(x_sorted, block_e, n_used, w_gu, b_gu, w_dn, b_dn):
    n_rows, d = x_sorted.shape
    n_e, _, f2 = w_gu.shape
    grid_spec = pltpu.PrefetchScalarGridSpec(
        num_scalar_prefetch=2,
        grid=(n_rows // MOE_BM,),
        in_specs=[pl.BlockSpec((MOE_BM, d), lambda b, be, nb: (b, 0)),
                  pl.BlockSpec((1, d, f2), lambda b, be, nb: (be[b], 0, 0)),
                  pl.BlockSpec((1, 1, f2), lambda b, be, nb: (be[b], 0, 0)),
                  pl.BlockSpec((1, f2 // 2, d), lambda b, be, nb: (be[b], 0, 0)),
                  pl.BlockSpec((1, 1, d), lambda b, be, nb: (be[b], 0, 0))],
        out_specs=pl.BlockSpec((MOE_BM, d), lambda b, be, nb: (b, 0)),
        scratch_shapes=[pltpu.VMEM((d, f2), BF16), pltpu.VMEM((f2 // 2, d), BF16)],
    )
    return pl.pallas_call(
        functools.partial(_moe_kernel, d_ff=f2 // 2),
        grid_spec=grid_spec,
        out_shape=jax.ShapeDtypeStruct((n_rows, d), BF16),
        compiler_params=_cparams(("arbitrary",)),
        name="moe_experts",
    )(block_e, n_used, x_sorted, w_gu, b_gu.reshape(n_e, 1, f2), w_dn, b_dn.reshape(n_e, 1, d))


def _moe_combine_kernel(y0_ref, y1_ref, y2_ref, y3_ref, g_ref, h_ref, mod_ref, lnw_ref, lnb_ref, o_ref, *, alpha):
    g = g_ref[...]
    f = jnp.zeros(h_ref.shape, F32)
    for i, y_ref in enumerate((y0_ref, y1_ref, y2_ref, y3_ref)):
        f = f + g[:, i:i + 1] * y_ref[...].astype(F32)
    x = alpha * h_ref[...] + mod_ref[0, 5:6, :] * f
    o_ref[...] = _layer_norm(x, lnw_ref[...], lnb_ref[...])


def _moe_combine(ys, gates, h1, mod, nct, ln_w, ln_b, alpha):
    n, d = h1.shape
    row = lambda width: pl.BlockSpec((TM, width), lambda m: (m, 0))
    return pl.pallas_call(
        functools.partial(_moe_combine_kernel, alpha=alpha),
        grid=(n // TM,),
        in_specs=[row(d)] * TOP_K + [row(TOP_K), row(d),
                                     pl.BlockSpec((1, 6, d), lambda m: (jnp.where(m < nct, 0, 1), 0, 0)),
                                     pl.BlockSpec((1, d), lambda m: (0, 0)), pl.BlockSpec((1, d), lambda m: (0, 0))],
        out_specs=row(d),
        out_shape=jax.ShapeDtypeStruct((n, d), F32),
        compiler_params=_cparams(("parallel",)),
        name="moe_combine_ln",
    )(*ys, gates, h1, mod, ln_w.reshape(1, d), ln_b.reshape(1, d))


def _sc_gather_rows(table, idx):
    n_rows = idx.shape[0]
    d = table.shape[1]
    n_workers = SC_CORES * SC_SUBCORES
    per_worker = n_rows // n_workers
    assert per_worker * n_workers == n_rows and per_worker % SUBLANES == 0
    step = math.gcd(per_worker, SC_GATHER_ROWS)
    mesh = plsc.VectorSubcoreMesh(core_axis_name="c", subcore_axis_name="s",
                                  num_cores=SC_CORES, num_subcores=SC_SUBCORES)

    def body(table_hbm, idx_hbm, out_hbm, idx_v, rows_v, sem):
        base = (lax.axis_index("s") * SC_CORES + lax.axis_index("c")) * per_worker

        @pl.loop(0, per_worker // step)
        def _(i):
            off = pl.multiple_of(base + i * step, SUBLANES)
            pltpu.sync_copy(idx_hbm.at[pl.ds(off, step)], idx_v)
            pltpu.async_copy(table_hbm.at[idx_v], rows_v, sem).wait()
            pltpu.sync_copy(rows_v, out_hbm.at[pl.ds(off, step)])

    return pl.kernel(
        body,
        out_type=jax.ShapeDtypeStruct((n_rows, d), table.dtype),
        mesh=mesh,
        scratch_types=[pltpu.VMEM((step,), jnp.int32), pltpu.VMEM((step, d), table.dtype),
                       pltpu.SemaphoreType.DMA],
        name="sc_gather_rows",
    )(table, idx)


def _moe(v_rows, route, counts, n_e, h1, mod, nct, ln_w, ln_b, alpha, w_gu, b_gu, w_dn, b_dn):
    n, d = v_rows.shape
    top_idx = route[:, :TOP_K].astype(jnp.int32)
    rank = route[:, TOP_K:2 * TOP_K].astype(jnp.int32)
    gates = route[:, 2 * TOP_K:3 * TOP_K]
    counts = counts[0, :n_e].astype(jnp.int32)
    nk = n * TOP_K
    padded = (counts + MOE_BM - 1) // MOE_BM * MOE_BM
    pad_end = jnp.cumsum(padded)
    pad_start = pad_end - padded
    dest = (pad_start[top_idx] + rank).reshape(nk)
    n_blocks = -(-nk // MOE_BM) + n_e
    n_rows = n_blocks * MOE_BM
    row_tok = jnp.zeros((n_rows,), jnp.int32).at[dest].set(jnp.arange(nk, dtype=jnp.int32) // TOP_K)
    block_e = jnp.minimum(jnp.searchsorted(pad_end, jnp.arange(n_blocks, dtype=jnp.int32) * MOE_BM, side='right'),
                          n_e - 1).astype(jnp.int32)
    n_used = (pad_end[-1:] // MOE_BM).astype(jnp.int32)
    x_sorted = _sc_gather_rows(v_rows, row_tok)
    yb = _moe_experts(x_sorted, block_e, n_used, w_gu, b_gu, w_dn, b_dn)
    dest2 = dest.reshape(n, TOP_K)
    ys = [yb[dest2[:, i]] for i in range(TOP_K)]
    return _moe_combine(ys, gates, h1, mod, nct, ln_w, ln_b, alpha)


def _rope_tables(l_ctx, l_lat):
    rows = l_lat // GRID_W
    row = jnp.repeat(jnp.arange(rows, dtype=F32), GRID_W)
    col = jnp.tile(jnp.arange(GRID_W, dtype=F32), rows)
    axis_dim = DIFF_HEAD_DIM // 2
    inv_freq = ROPE_THETA ** (-jnp.arange(0, axis_dim, 2, dtype=F32) / axis_dim)
    ang_r = row[:, None] * inv_freq
    ang_c = col[:, None] * inv_freq
    ang = jnp.concatenate([ang_r, ang_r, ang_c, ang_c], -1)
    ang = jnp.concatenate([jnp.zeros((l_ctx, DIFF_HEAD_DIM), F32), ang], 0)
    ang = jnp.concatenate([ang, ang], -1)
    return jnp.cos(ang), jnp.sin(ang)


def _gdn_scalars(gates, n_vh):
    n = gates.shape[0]
    n_qk = n_vh // 2
    nc = n // GDN_CHUNK
    gc = gates[:, :2 * n_vh].reshape(n, 2, n_qk, 2)
    beta = gates[:, 2 * n_vh:4 * n_vh].reshape(n, 2, n_qk, 2)
    gcc = gc.reshape(nc, GDN_CHUNK, 2, n_qk, 2)
    gl = jnp.stack([gcc[:, -1, 0], gcc[:, 0, 1]], axis=1)
    gl_tok = jnp.repeat(gl, GDN_CHUNK, axis=0)
    col = jnp.concatenate([gc, beta, gl_tok], axis=-1)
    scol = jnp.transpose(col, (1, 2, 0, 3))
    srow = jnp.transpose(col[..., :4], (1, 2, 3, 0))
    srowc = jnp.transpose(col[..., :4].reshape(nc, GDN_CHUNK, 2, n_qk, 4), (2, 3, 0, 4, 1))
    glrow = jnp.broadcast_to(jnp.transpose(gl, (1, 2, 0, 3))[..., None], (2, n_qk, nc, 2, GDN_HEAD_DIM))
    glrow = glrow.reshape(2, n_qk, nc, 1, 2 * GDN_HEAD_DIM)
    return scol, srow, srowc, glrow


def kernel(x, c, ctx, c_ctx, ada_w, ada_b, ln_w, ln_b, gdn_w_in, gdn_conv_w, gdn_a_log, gdn_dt_bias, gdn_norm_w,
           gdn_w_out, diff_w_in, diff_lambda, diff_subln_w, diff_w_out, router_w, router_b, moe_w_gate_up,
           moe_b_gate_up, moe_w_down, moe_b_down):
    batch, l_lat, d = x.shape
    l_ctx = ctx.shape[1]
    depth = ada_w.shape[0]
    assert batch == 1 and l_ctx % TM == 0 and l_lat % TM == 0 and l_lat % GRID_W == 0
    nct = l_ctx // TM
    n_e = router_w.shape[-1]
    alpha = (2 * depth) ** 0.25
    n_vh = gdn_a_log.shape[-1]
    v_dim = n_vh * GDN_HEAD_DIM
    conv_dim = gdn_conv_w.shape[-1]
    qk_dim = (conv_dim - v_dim) // 2
    n_qk = qk_dim // GDN_HEAD_DIM

    h = jnp.concatenate([ctx[0], x[0]], axis=0)
    n = h.shape[0]
    mods = _modulations(c_ctx, c, ada_w, ada_b)
    cos, sin = _rope_tables(l_ctx, l_lat)

    for i in range(depth):
        j = i // 2
        mod = mods[i]
        if i % 2 == 0:
            w_in = gdn_w_in[j]
            w_main = w_in[:, :conv_dim + v_dim].astype(BF16)
            w_ab = jnp.zeros((d, LANES), F32).at[:, :4 * n_vh].set(w_in[:, conv_dim + v_dim:]).astype(BF16)
            p, ab = _project(h, mod, nct, [w_main, w_ab], [BF16, F32])
            gates = _gdn_gates(ab, gdn_a_log[j], gdn_dt_bias[j])
            feat = _gdn_features(p, gdn_conv_w[j], nct, qk_dim)
            kt = jnp.transpose(feat[:, qk_dim:2 * qk_dim].reshape(n // GDN_CHUNK, GDN_CHUNK, n_qk, GDN_HEAD_DIM),
                               (2, 0, 3, 1))
            scol, srow, srowc, glrow = _gdn_scalars(gates, n_vh)
            o_f = _gdn_scan(feat, kt, scol, srow, srowc, glrow, nct, False)
            o_b = _gdn_scan(feat, kt, scol, srow, srowc, glrow, nct, True)
            acts = (o_f, o_b, p, gdn_norm_w[j], conv_dim // v_dim)
            h1, v, route, counts = _mixer_out("gdn", acts, gdn_w_out[j].astype(BF16), h, mod, nct, ln_w[i, 0], ln_b[i, 0],
                                       router_w[i], router_b[i], alpha)
        else:
            lambda_init = 0.8 - 0.6 * math.exp(-0.3 * i)
            qkv = _qkv_rope(h, mod, nct, diff_w_in[j].astype(BF16), cos, sin)
            a = _diff_attention(qkv, diff_lambda[j], diff_subln_w[j], nct, l_ctx, lambda_init)
            h1, v, route, counts = _mixer_out("attn", (a,), diff_w_out[j].astype(BF16), h, mod, nct, ln_w[i, 0], ln_b[i, 0],
                                       router_w[i], router_b[i], alpha)
        h = _moe(v, route, counts, n_e, h1, mod, nct, ln_w[i, 1], ln_b[i, 1], alpha,
                 moe_w_gate_up[i], moe_b_gate_up[i], moe_w_down[i], moe_b_down[i])
    return h[l_ctx:].reshape(batch, l_lat, d)
```

```python
import functools
import math

import jax
import jax.numpy as jnp
from jax import lax
from jax.experimental import pallas as pl
from jax.experimental.pallas import tpu as pltpu
from jax.experimental.pallas import tpu_sc as plsc

F32 = jnp.float32
BF16 = jnp.bfloat16
HIGHEST = lax.Precision.HIGHEST

GRID_W = 64
GDN_HEAD_DIM = 128
GDN_CHUNK = 64
DIFF_HEAD_DIM = 64
ROPE_THETA = 10000.0
TOP_K = 4
SWIGLU_LIMIT = 7.0
SWIGLU_ALPHA = 1.702
NORM_EPS = 1e-5
L2_EPS = 1e-6

LANES = 128
SUBLANES = 8
HALO = 16
VMEM_LIMIT = 56 * 1024 * 1024

TM = 256
MOE_BM = 256
ATTN_TK = 1024

SC_CORES = 2
SC_SUBCORES = 16
SC_GATHER_ROWS = 32


def _cparams(sem):
    return pltpu.CompilerParams(dimension_semantics=sem, vmem_limit_bytes=VMEM_LIMIT)


def _silu(x):
    return x * jax.nn.sigmoid(x)


def _dot(a, b):
    return jnp.dot(a, b, preferred_element_type=F32)


def _dot_nt(a, b):
    return lax.dot_general(a, b, (((1,), (1,)), ((), ())), preferred_element_type=F32)


def _modulate(h, mod_ref, shift, scale):
    return h * (1.0 + mod_ref[0, scale:scale + 1, :]) + mod_ref[0, shift:shift + 1, :]


def _layer_norm(x, w, b):
    mu = jnp.mean(x, axis=-1, keepdims=True)
    xc = x - mu
    var = jnp.mean(xc * xc, axis=-1, keepdims=True)
    return xc * lax.rsqrt(var + NORM_EPS) * w + b


def _mod_kernel(s_ref, w_ref, b_ref, o_ref):
    s = _silu(s_ref[...])
    o_ref[0] = jnp.dot(s, w_ref[0], preferred_element_type=F32, precision=HIGHEST) + b_ref[0]


def _modulations(c_ctx, c, ada_w, ada_b):
    depth, d, d6 = ada_w.shape
    tn = d6 // 4
    s = jnp.zeros((SUBLANES, d), F32).at[0].set(c_ctx).at[1].set(c[0])
    out = pl.pallas_call(
        _mod_kernel,
        grid=(depth, d6 // tn),
        in_specs=[pl.BlockSpec((SUBLANES, d), lambda i, n: (0, 0)),
                  pl.BlockSpec((1, d, tn), lambda i, n: (i, 0, n)),
                  pl.BlockSpec((1, 1, tn), lambda i, n: (i, 0, n))],
        out_specs=pl.BlockSpec((1, SUBLANES, tn), lambda i, n: (i, 0, n)),
        out_shape=jax.ShapeDtypeStruct((depth, SUBLANES, d6), F32),
        compiler_params=_cparams(("parallel", "parallel")),
        name="adaln_mod",
    )(s, ada_w, ada_b.reshape(depth, 1, d6))
    return out[:, :2].reshape(depth, 2, 6, d)


def _proj_kernel(h_ref, mod_ref, *refs, n_w, chunk):
    w_refs, o_refs = refs[:n_w], refs[n_w:]
    u = _modulate(h_ref[...], mod_ref, 0, 1).astype(BF16)
    for w_ref, o_ref in zip(w_refs, o_refs):
        n = w_ref.shape[1]
        for j in range(0, n, chunk):
            jc = min(chunk, n - j)
            o_ref[:, j:j + jc] = _dot(u, w_ref[:, j:j + jc]).astype(o_ref.dtype)


def _project(h, mod, nct, weights, out_dtypes):
    n, d = h.shape
    return pl.pallas_call(
        functools.partial(_proj_kernel, n_w=len(weights), chunk=512),
        grid=(n // TM,),
        in_specs=[pl.BlockSpec((TM, d), lambda m: (m, 0)),
                  pl.BlockSpec((1, 6, d), lambda m: (jnp.where(m < nct, 0, 1), 0, 0))]
                 + [pl.BlockSpec(w.shape, lambda m: (0, 0)) for w in weights],
        out_specs=[pl.BlockSpec((TM, w.shape[1]), lambda m: (m, 0)) for w in weights],
        out_shape=[jax.ShapeDtypeStruct((n, w.shape[1]), dt) for w, dt in zip(weights, out_dtypes)],
        compiler_params=_cparams(("parallel",)),
        name="mod_proj",
    )(h, mod, *weights)


def _qkv_rope_kernel(h_ref, mod_ref, w_ref, cos_ref, sin_ref, o_ref, *, chunk, n_rope, n_q, q_scale):
    u = _modulate(h_ref[...], mod_ref, 0, 1).astype(BF16)
    rep = chunk // LANES
    cos = jnp.concatenate([cos_ref[...]] * rep, axis=1)
    sin = jnp.concatenate([sin_ref[...]] * rep, axis=1)
    lane = lax.broadcasted_iota(jnp.int32, (h_ref.shape[0], chunk), 1)
    first = (lane % 32) < 16
    for j in range(w_ref.shape[1] // chunk):
        y = _dot(u, w_ref[:, j * chunk:(j + 1) * chunk])
        if j < n_rope:
            rot = jnp.where(first, -pltpu.roll(y, chunk - 16, 1), pltpu.roll(y, 16, 1))
            y = y * cos + rot * sin
            if j < n_q:
                y = y * q_scale
        o_ref[:, j * chunk:(j + 1) * chunk] = y.astype(o_ref.dtype)


def _qkv_rope(h, mod, nct, w, cos, sin):
    n, d = h.shape
    chunk = 512
    return pl.pallas_call(
        functools.partial(_qkv_rope_kernel, chunk=chunk, n_rope=2 * d // chunk, n_q=d // chunk,
                          q_scale=DIFF_HEAD_DIM ** -0.5 * math.log2(math.e)),
        grid=(n // TM,),
        in_specs=[pl.BlockSpec((TM, d), lambda m: (m, 0)),
                  pl.BlockSpec((1, 6, d), lambda m: (jnp.where(m < nct, 0, 1), 0, 0)),
                  pl.BlockSpec(w.shape, lambda m: (0, 0)),
                  pl.BlockSpec((TM, LANES), lambda m: (m, 0)),
                  pl.BlockSpec((TM, LANES), lambda m: (m, 0))],
        out_specs=pl.BlockSpec((TM, w.shape[1]), lambda m: (m, 0)),
        out_shape=jax.ShapeDtypeStruct((n, w.shape[1]), BF16),
        compiler_params=_cparams(("parallel",)),
        name="qkv_rope",
    )(h, mod, w, cos, sin)


def _gdn_gates_kernel(ab_ref, alog_ref, dtb_ref, o_ref, *, chunk, n_heads):
    x = ab_ref[...]
    t = x + dtb_ref[...]
    softplus = jnp.maximum(t, 0.0) + jnp.log1p(jnp.exp(-jnp.abs(t)))
    g = -jnp.exp(alog_ref[...]) * softplus
    beta = jax.nn.sigmoid(x)
    tm = x.shape[0]
    r = lax.broadcasted_iota(jnp.int32, (tm, tm), 0)
    c = lax.broadcasted_iota(jnp.int32, (tm, tm), 1)
    same = (r // chunk) == (c // chunk)
    t_fwd = jnp.where(same & (r >= c), 1.0, 0.0).astype(F32)
    t_bwd = jnp.where(same & (r <= c), 1.0, 0.0).astype(F32)
    g_fwd = jnp.dot(t_fwd, g, preferred_element_type=F32, precision=HIGHEST)
    g_bwd = jnp.dot(t_bwd, g, preferred_element_type=F32, precision=HIGHEST)
    lane = lax.broadcasted_iota(jnp.int32, x.shape, 1)
    o_ref[...] = jnp.where(lane < n_heads, g_fwd, jnp.where(lane < 2 * n_heads, g_bwd, beta))


def _gdn_gates(ab, a_log, dt_bias):
    n = ab.shape[0]
    nh = a_log.shape[-1]
    pad = lambda v: jnp.zeros((1, LANES), F32).at[0, :2 * nh].set(v.reshape(-1))
    return pl.pallas_call(
        functools.partial(_gdn_gates_kernel, chunk=GDN_CHUNK, n_heads=nh),
        grid=(n // TM,),
        in_specs=[pl.BlockSpec((TM, LANES), lambda m: (m, 0)),
                  pl.BlockSpec((1, LANES), lambda m: (0, 0)),
                  pl.BlockSpec((1, LANES), lambda m: (0, 0))],
        out_specs=pl.BlockSpec((TM, LANES), lambda m: (m, 0)),
        out_shape=jax.ShapeDtypeStruct((n, LANES), F32),
        compiler_params=_cparams(("parallel",)),
        name="gdn_gates",
    )(ab, pad(a_log), pad(dt_bias))


def _gdn_feat_kernel(x_ref, prev_ref, next_ref, w_ref, o_ref, xs_ref, *, nct, n_tiles, n_norm, n_q, q_scale, width):
    m = pl.program_id(0)
    j = pl.program_id(1)
    tm, tn = x_ref.shape
    pad = width // 2
    prev_ok = jnp.logical_and(m != 0, m != nct)
    next_ok = jnp.logical_and(m != nct - 1, m != n_tiles - 1)
    xs_ref[0:HALO, :] = jnp.where(prev_ok, prev_ref[...].astype(F32), 0.0)
    xs_ref[HALO:HALO + tm, :] = x_ref[...].astype(F32)
    xs_ref[HALO + tm:, :] = jnp.where(next_ok, next_ref[...].astype(F32), 0.0)
    acc = jnp.zeros((tm, tn), F32)
    for t in range(width):
        acc = acc + w_ref[t:t + 1, :] * xs_ref[HALO - pad + t:HALO - pad + t + tm, :]
    y = _silu(acc)

    @pl.when(j >= n_norm)
    def _():
        o_ref[...] = y.astype(o_ref.dtype)

    @pl.when(j < n_norm)
    def _():
        scale = jnp.where(j < n_q, q_scale, 1.0).astype(F32)
        for hh in range(tn // GDN_HEAD_DIM):
            sl = slice(hh * GDN_HEAD_DIM, (hh + 1) * GDN_HEAD_DIM)
            yh = y[:, sl]
            inv = lax.rsqrt(jnp.sum(yh * yh, axis=-1, keepdims=True) + L2_EPS)
            o_ref[:, sl] = (yh * (inv * scale)).astype(o_ref.dtype)


def _gdn_features(p, conv_w, nct, qk_dim):
    n = p.shape[0]
    width, conv_dim = conv_w.shape
    tn = 512
    n_tiles = n // TM
    rb = TM // HALO
    last_rb = n // HALO - 1
    return pl.pallas_call(
        functools.partial(_gdn_feat_kernel, nct=nct, n_tiles=n_tiles, n_norm=2 * qk_dim // tn, n_q=qk_dim // tn,
                          q_scale=GDN_HEAD_DIM ** -0.5, width=width),
        grid=(n_tiles, conv_dim // tn),
        in_specs=[pl.BlockSpec((TM, tn), lambda m, j: (m, j)),
                  pl.BlockSpec((HALO, tn), lambda m, j: (jnp.maximum(m * rb - 1, 0), j)),
                  pl.BlockSpec((HALO, tn), lambda m, j: (jnp.minimum((m + 1) * rb, last_rb), j)),
                  pl.BlockSpec((width, tn), lambda m, j: (0, j))],
        out_specs=pl.BlockSpec((TM, tn), lambda m, j: (m, j)),
        out_shape=jax.ShapeDtypeStruct((n, conv_dim), BF16),
        scratch_shapes=[pltpu.VMEM((TM + 2 * HALO, tn), F32)],
        compiler_params=_cparams(("parallel", "parallel")),
        name="gdn_features",
    )(p, p, p, conv_w)


def _gdn_scan_kernel(*refs, chunk):
    ins, o_refs, s_ref = refs[:16], refs[16:18], refs[18]
    t = pl.program_id(1)

    @pl.when(t == 0)
    def _():
        s_ref[...] = jnp.zeros_like(s_ref)

    rows = ins[0].shape[0]
    hd = GDN_HEAD_DIM
    n_chunk = rows // chunk
    n_double = int(math.log2(chunk)) - 1
    ri = lax.broadcasted_iota(jnp.int32, (rows, rows), 0)
    ci = lax.broadcasted_iota(jnp.int32, (rows, rows), 1)
    same = (ri // chunk) == (ci // chunk)
    eye = jnp.where(ri == ci, 1.0, 0.0).astype(F32)
    r64 = lax.broadcasted_iota(jnp.int32, (chunk, chunk), 0)
    c64 = lax.broadcasted_iota(jnp.int32, (chunk, chunk), 1)

    dirs = []
    for d in range(2):
        q_ref, k_ref, kt_ref, v_ref, scol_ref, srow_ref, srowc_ref, glrow_ref = ins[8 * d:8 * d + 8]
        k = k_ref[...]
        dirs.append(dict(
            q=q_ref[...], k=k, v=v_ref[...], kt=kt_ref[...], srowc_ref=srowc_ref, glrow_ref=glrow_ref,
            scol=scol_ref[0, 0],
            srow=srow_ref[0, 0],
            incl=same & ((ri <= ci) if d else (ri >= ci)), strict=same & ((ri < ci) if d else (ri > ci)),
            incl64=(r64 <= c64) if d else (r64 >= c64),
            kk=_dot_nt(k, k),
            order=list(range(n_chunk - 1, -1, -1)) if d else list(range(n_chunk))))
    chains = [(d, a) for d in range(2) for a in range(2)]
    gcol = {c: dirs[c[0]]["scol"][:, c[1]:c[1] + 1] for c in chains}
    bcol = {c: dirs[c[0]]["scol"][:, 2 + c[1]:3 + c[1]] for c in chains}
    e_g = {c: jnp.exp(gcol[c]) for c in chains}
    e_k = {c: jnp.exp(dirs[c[0]]["scol"][:, 4 + c[1]:5 + c[1]] - gcol[c]) for c in chains}
    neg_l = {c: jnp.where(dirs[c[0]]["strict"],
                          -(bcol[c] * dirs[c[0]]["kk"]
                            * jnp.exp(jnp.where(dirs[c[0]]["incl"], gcol[c] - dirs[c[0]]["srow"][c[1]:c[1] + 1, :],
                                                -jnp.inf))), 0.0) for c in chains}
    inv = {c: eye + neg_l[c] for c in chains}
    pw = {c: neg_l[c].astype(BF16) for c in chains}
    for _ in range(n_double):
        pw = {c: _dot(pw[c], pw[c]).astype(BF16) for c in chains}
        inv = {c: inv[c] + _dot(inv[c].astype(BF16), pw[c]) for c in chains}
    rhs = {c: jnp.concatenate([dirs[c[0]]["v"][:, c[1] * hd:(c[1] + 1) * hd].astype(F32) * bcol[c],
                               dirs[c[0]]["k"].astype(F32) * (bcol[c] * e_g[c])], axis=1).astype(BF16) for c in chains}
    uw = {c: _dot(inv[c].astype(BF16), rhs[c]) for c in chains}

    state = [s_ref[d] for d in range(2)]
    for i in range(n_chunk):
        js = [dirs[d]["order"][i] for d in range(2)]
        rs = [slice(j * chunk, (j + 1) * chunk) for j in js]
        s_b = [st.astype(BF16) for st in state]
        qk_c = [_dot_nt(dirs[d]["q"][rs[d]], dirs[d]["k"][rs[d]]) for d in range(2)]
        q_s = [_dot(dirs[d]["q"][rs[d]], s_b[d]) for d in range(2)]
        v_new = {(d, a): uw[d, a][rs[d], :hd] - _dot(uw[d, a][rs[d], hd:].astype(BF16), s_b[d][:, a * hd:(a + 1) * hd])
                 for d, a in chains}
        qkd = {(d, a): (qk_c[d] * jnp.exp(jnp.where(dirs[d]["incl64"],
                                                    gcol[d, a][rs[d]] - dirs[d]["srowc_ref"][0, 0, js[d], a:a + 1, :],
                                                    -jnp.inf))).astype(BF16) for d, a in chains}
        for d, a in chains:
            o = e_g[d, a][rs[d]] * q_s[d][:, a * hd:(a + 1) * hd] + _dot(qkd[d, a], v_new[d, a].astype(BF16))
            o_refs[d][rs[d], a * hd:(a + 1) * hd] = o.astype(o_refs[d].dtype)
        for d in range(2):
            v_s = jnp.concatenate([v_new[d, a] * e_k[d, a][rs[d]] for a in range(2)], axis=1).astype(BF16)
            parts = [jnp.zeros((js[d] * chunk, 2 * hd), BF16), v_s,
                     jnp.zeros(((n_chunk - 1 - js[d]) * chunk, 2 * hd), BF16)]
            v_pad = jnp.concatenate([p for p in parts if p.shape[0]], axis=0)
            e_l = jnp.exp(dirs[d]["glrow_ref"][0, 0, js[d]])
            state[d] = state[d] * e_l + _dot(dirs[d]["kt"], v_pad)
    for d in range(2):
        s_ref[d] = state[d]


def _transpose_kernel(x_ref, o_ref):
    o_ref[...] = x_ref[...].astype(F32).T.astype(o_ref.dtype)


def _transposed_keys(feat, qk_dim):
    n = feat.shape[0]
    tn = 512
    return pl.pallas_call(
        _transpose_kernel,
        grid=(n // TM, qk_dim // tn),
        in_specs=[pl.BlockSpec((TM, tn), lambda m, j: (m, qk_dim // tn + j))],
        out_specs=pl.BlockSpec((tn, TM), lambda m, j: (j, m)),
        out_shape=jax.ShapeDtypeStruct((qk_dim, n), feat.dtype),
        compiler_params=_cparams(("parallel", "parallel")),
        name="gdn_keys_t",
    )(feat)


def _gdn_scan(feat, kt, scol, srow, srowc, glrow, nct):
    n = feat.shape[0]
    hd = GDN_HEAD_DIM
    n_qk = kt.shape[0] // hd
    n_tiles = n // TM
    cpt = TM // GDN_CHUNK

    def tile(d, t):
        return jnp.where(t < nct, nct - 1 - t, n_tiles - 1 - (t - nct)) if d else t

    def specs(d):
        return [pl.BlockSpec((TM, hd), lambda h, t: (tile(d, t), h)),
                pl.BlockSpec((TM, hd), lambda h, t: (tile(d, t), n_qk + h)),
                pl.BlockSpec((hd, TM), lambda h, t: (h, tile(d, t))),
                pl.BlockSpec((TM, 2 * hd), lambda h, t: (tile(d, t), n_qk + h)),
                pl.BlockSpec((1, 1, TM, 6), lambda h, t: (d, h, tile(d, t), 0)),
                pl.BlockSpec((1, 1, 4, TM), lambda h, t: (d, h, 0, tile(d, t))),
                pl.BlockSpec((1, 1, cpt, 4, GDN_CHUNK), lambda h, t: (d, h, tile(d, t), 0, 0)),
                pl.BlockSpec((1, 1, cpt, 1, 2 * hd), lambda h, t: (d, h, tile(d, t), 0, 0))]

    args = [feat, feat, kt, feat, scol, srow, srowc, glrow]
    out = jax.ShapeDtypeStruct((n, 2 * n_qk * hd), BF16)
    return pl.pallas_call(
        functools.partial(_gdn_scan_kernel, chunk=GDN_CHUNK),
        grid=(n_qk, n_tiles),
        in_specs=specs(0) + specs(1),
        out_specs=[pl.BlockSpec((TM, 2 * hd), lambda h, t: (tile(0, t), h)),
                   pl.BlockSpec((TM, 2 * hd), lambda h, t: (tile(1, t), h))],
        out_shape=[out, out],
        scratch_shapes=[pltpu.VMEM((2, hd, 2 * hd), F32)],
        compiler_params=_cparams(("parallel", "arbitrary")),
        name="gdn_scan",
    )(*args, *args)


def _post_mixer(y, h_ref, mod_ref, lnw_ref, lnb_ref, rw_ref, rb_ref, h1_ref, v_ref, rt_ref, cnt_ref, carry_ref,
                alpha, n_e):
    @pl.when(pl.program_id(0) == 0)
    def _():
        carry_ref[...] = jnp.zeros_like(carry_ref)

    x = alpha * h_ref[...] + mod_ref[0, 2:3, :] * y
    h1 = _layer_norm(x, lnw_ref[...], lnb_ref[...])
    h1_ref[...] = h1
    v = _modulate(h1, mod_ref, 3, 4)
    v_ref[...] = v.astype(v_ref.dtype)
    logits = jnp.dot(v, rw_ref[...], preferred_element_type=F32, precision=HIGHEST) + rb_ref[...]

    tm = logits.shape[0]
    lane = lax.broadcasted_iota(jnp.int32, logits.shape, 1)
    rest = jnp.where(lane < n_e, logits, -jnp.inf)
    top_val, top_idx, picked = [], [], []
    for _ in range(TOP_K):
        mx = jnp.max(rest, axis=-1, keepdims=True)
        idx = jnp.min(jnp.where(rest == mx, lane, LANES), axis=-1, keepdims=True)
        hit = lane == idx
        top_val.append(mx)
        top_idx.append(idx)
        picked.append(hit)
        rest = jnp.where(hit, -jnp.inf, rest)
    e = [jnp.exp(tv - top_val[0]) for tv in top_val]
    denom = functools.reduce(lambda a, b: a + b, e)
    onehot = functools.reduce(lambda a, b: a + b, [jnp.where(hit, 1.0, 0.0) for hit in picked])
    r = lax.broadcasted_iota(jnp.int32, (tm, tm), 0)
    c = lax.broadcasted_iota(jnp.int32, (tm, tm), 1)
    earlier = jnp.where(r > c, 1.0, 0.0).astype(BF16)
    before = _dot(earlier, onehot.astype(BF16)) + carry_ref[0:1, :]
    rt = jnp.zeros(logits.shape, F32)
    for k in range(TOP_K):
        rank = jnp.sum(jnp.where(picked[k], before, 0.0), axis=-1, keepdims=True)
        rt = jnp.where(lane == k, top_idx[k].astype(F32), rt)
        rt = jnp.where(lane == TOP_K + k, rank, rt)
        rt = jnp.where(lane == 2 * TOP_K + k, e[k] / denom, rt)
    rt_ref[...] = rt
    total = carry_ref[0:1, :] + jnp.sum(onehot, axis=0, keepdims=True)
    carry_ref[...] = jnp.broadcast_to(total, carry_ref.shape)
    cnt_ref[...] = jnp.broadcast_to(total, cnt_ref.shape)


def _gdn_out_kernel(of_ref, ob_ref, z_ref, nw_ref, w_ref, h_ref, mod_ref, lnw_ref, lnb_ref, rw_ref, rb_ref,
                    h1_ref, v_ref, rt_ref, cnt_ref, carry_ref, a_ref, *, alpha, n_e):
    hd = GDN_HEAD_DIM
    for hh in range(of_ref.shape[1] // hd):
        sl = slice(hh * hd, (hh + 1) * hd)
        o = of_ref[:, sl].astype(F32) + ob_ref[:, sl].astype(F32)
        o = o * lax.rsqrt(jnp.mean(o * o, axis=-1, keepdims=True) + NORM_EPS) * nw_ref[...]
        a_ref[:, sl] = (o * _silu(z_ref[:, sl].astype(F32))).astype(BF16)
    y = _dot(a_ref[...], w_ref[...])
    _post_mixer(y, h_ref, mod_ref, lnw_ref, lnb_ref, rw_ref, rb_ref, h1_ref, v_ref, rt_ref, cnt_ref, carry_ref,
                alpha, n_e)


def _attn_out_kernel(a_ref, w_ref, h_ref, mod_ref, lnw_ref, lnb_ref, rw_ref, rb_ref, h1_ref, v_ref, rt_ref, cnt_ref,
                     carry_ref, *, alpha, n_e):
    y = _dot(a_ref[...], w_ref[...])
    _post_mixer(y, h_ref, mod_ref, lnw_ref, lnb_ref, rw_ref, rb_ref, h1_ref, v_ref, rt_ref, cnt_ref, carry_ref,
                alpha, n_e)


def _mixer_out(kind, acts, w_out, h, mod, nct, ln_w, ln_b, router_w, router_b, alpha):
    n, d = h.shape
    row = lambda width: pl.BlockSpec((TM, width), lambda m: (m, 0))
    full = lambda arr: pl.BlockSpec(arr.shape, lambda m: (0,) * arr.ndim)
    n_e = router_w.shape[1]
    rw = jnp.zeros((d, LANES), F32).at[:, :n_e].set(router_w)
    rb = jnp.zeros((1, LANES), F32).at[0, :n_e].set(router_b)
    tail = [w_out, h, mod, ln_w.reshape(1, d), ln_b.reshape(1, d), rw, rb]
    tail_specs = [full(w_out), row(d), pl.BlockSpec((1, 6, d), lambda m: (jnp.where(m < nct, 0, 1), 0, 0)),
                  pl.BlockSpec((1, d), lambda m: (0, 0)), pl.BlockSpec((1, d), lambda m: (0, 0)), full(rw), full(rb)]
    if kind == "gdn":
        o_f, o_b, p, norm_w, z_col = acts
        v_dim = o_f.shape[1]
        body = functools.partial(_gdn_out_kernel, alpha=alpha, n_e=n_e)
        args = [o_f, o_b, p, norm_w.reshape(1, -1)] + tail
        specs = [row(v_dim), row(v_dim), pl.BlockSpec((TM, v_dim), lambda m: (m, z_col)),
                 pl.BlockSpec((1, norm_w.shape[0]), lambda m: (0, 0))] + tail_specs
        scratch = [pltpu.VMEM((SUBLANES, LANES), F32), pltpu.VMEM((TM, v_dim), BF16)]
    else:
        (a,) = acts
        body = functools.partial(_attn_out_kernel, alpha=alpha, n_e=n_e)
        args = [a] + tail
        specs = [row(a.shape[1])] + tail_specs
        scratch = [pltpu.VMEM((SUBLANES, LANES), F32)]
    return pl.pallas_call(
        body,
        grid=(n // TM,),
        in_specs=specs,
        out_specs=[row(d), row(d), row(LANES), pl.BlockSpec((SUBLANES, LANES), lambda m: (0, 0))],
        out_shape=[jax.ShapeDtypeStruct((n, d), F32), jax.ShapeDtypeStruct((n, d), F32),
                   jax.ShapeDtypeStruct((n, LANES), F32), jax.ShapeDtypeStruct((SUBLANES, LANES), F32)],
        scratch_shapes=scratch,
        compiler_params=_cparams(("arbitrary",)),
        name=kind + "_out_ln",
    )(*args)


def _attn_kernel(q_ref, k_ref, v_ref, lam_ref, sw_ref, o_ref, acc_ref, m_ref, l_ref, sa_ref, sb_ref,
                 *, tk, nct, l_ctx, n_pairs, lambda_init):
    mt = pl.program_id(1)
    q = q_ref[...]
    lane = lax.broadcasted_iota(jnp.int32, q.shape, 1)
    zero = jnp.zeros_like(q)
    q_maps = (jnp.where(lane < DIFF_HEAD_DIM, q, zero), jnp.where(lane >= DIFF_HEAD_DIM, q, zero))
    acc_ref[...] = jnp.zeros_like(acc_ref)
    m_ref[...] = jnp.full_like(m_ref, -jnp.inf)
    l_ref[...] = jnp.zeros_like(l_ref)

    def scores(start, s_ref):
        kj = k_ref[pl.ds(start, tk), :]
        for i, qm in enumerate(q_maps):
            s_ref[i] = _dot_nt(qm, kj)

    def absorb(score_of_map, vj):
        for i in range(2):
            s = score_of_map(i)
            m_old = m_ref[i]
            m_new = jnp.maximum(m_old, jnp.max(s, axis=-1, keepdims=True))
            alpha = jnp.exp2(m_old - m_new)
            p = jnp.exp2(s - m_new)
            l_ref[i] = alpha * l_ref[i] + jnp.sum(p, axis=-1, keepdims=True)
            acc_ref[i] = alpha * acc_ref[i] + _dot(p.astype(BF16), vj)
            m_ref[i] = m_new

    k_ctx = k_ref[0:l_ctx, :]
    absorb(lambda i: _dot_nt(q_maps[i], k_ctx), v_ref[0:l_ctx, :])

    @pl.when(mt >= nct)
    def _():
        last = l_ctx + (2 * n_pairs - 1) * tk
        scores(l_ctx, sa_ref)

        def body(jj, carry):
            c0 = pl.multiple_of(l_ctx + 2 * jj * tk, LANES)
            c1 = pl.multiple_of(c0 + tk, LANES)
            c2 = pl.multiple_of(jnp.minimum(c1 + tk, last), LANES)
            scores(c1, sb_ref)
            absorb(lambda i: sa_ref[i], v_ref[pl.ds(c0, tk), :])
            scores(c2, sa_ref)
            absorb(lambda i: sb_ref[i], v_ref[pl.ds(c1, tk), :])
            return carry

        lax.fori_loop(0, n_pairs, body, 0)

    lam = lam_ref[...]
    lam_full = (jnp.exp(jnp.sum(lam[0:1] * lam[1:2], axis=-1, keepdims=True))
                - jnp.exp(jnp.sum(lam[2:3] * lam[3:4], axis=-1, keepdims=True)) + lambda_init)
    o = acc_ref[0] / l_ref[0] - lam_full * (acc_ref[1] / l_ref[1])
    o = o * lax.rsqrt(jnp.mean(o * o, axis=-1, keepdims=True) + NORM_EPS) * sw_ref[...]
    o_ref[...] = (o * (1.0 - lambda_init)).astype(o_ref.dtype)


def _diff_attention(qkv, lam, subln_w, nct, l_ctx, lambda_init):
    n = qkv.shape[0]
    d = qkv.shape[1] // 3
    vd = subln_w.shape[0]
    n_heads = d // vd
    l_lat = n - l_ctx
    tk = math.gcd(l_lat // 2, ATTN_TK)
    return pl.pallas_call(
        functools.partial(_attn_kernel, tk=tk, nct=nct, l_ctx=l_ctx, n_pairs=l_lat // (2 * tk),
                          lambda_init=lambda_init),
        grid=(n_heads, n // TM),
        in_specs=[pl.BlockSpec((TM, vd), lambda h, m: (m, h)),
                  pl.BlockSpec((n, vd), lambda h, m: (0, n_heads + h)),
                  pl.BlockSpec((n, vd), lambda h, m: (0, 2 * n_heads + h)),
                  pl.BlockSpec(lam.shape, lambda h, m: (0, 0)),
                  pl.BlockSpec((1, vd), lambda h, m: (0, 0))],
        out_specs=pl.BlockSpec((TM, vd), lambda h, m: (m, h)),
        out_shape=jax.ShapeDtypeStruct((n, d), BF16),
        scratch_shapes=[pltpu.VMEM((2, TM, vd), F32), pltpu.VMEM((2, TM, 1), F32), pltpu.VMEM((2, TM, 1), F32),
                        pltpu.VMEM((2, TM, tk), F32), pltpu.VMEM((2, TM, tk), F32)],
        compiler_params=_cparams(("parallel", "parallel")),
        name="diff_attn",
    )(qkv, qkv, qkv, lam, subln_w.reshape(1, vd))


def _moe_kernel(be_ref, nb_ref, x_ref, wgu_ref, bgu_ref, wdn_ref, bdn_ref, y_ref, wgu_s, wdn_s, *, d_ff):
    b = pl.program_id(0)
    changed = jnp.logical_or(b == 0, be_ref[b] != be_ref[jnp.maximum(b - 1, 0)])

    @pl.when(changed)
    def _():
        wgu_s[...] = wgu_ref[0].astype(BF16)
        wdn_s[...] = wdn_ref[0].astype(BF16)

    @pl.when(b < nb_ref[0])
    def _():
        h = _dot(x_ref[...].astype(BF16), wgu_s[...]) + bgu_ref[0]
        gate = jnp.minimum(h[:, :d_ff], SWIGLU_LIMIT)
        up = jnp.clip(h[:, d_ff:], -SWIGLU_LIMIT, SWIGLU_LIMIT)
        glu = gate * jax.nn.sigmoid(gate * SWIGLU_ALPHA)
        act = ((up + 1.0) * glu).astype(BF16)
        y_ref[...] = (_dot(act, wdn_s[...]) + bdn_ref[0]).astype(y_ref.dtype)

    @pl.when(b >= nb_ref[0])
    def _():
        y_ref[...] = jnp.zeros_like(y_ref)


def _moe_experts(x_sorted, block_e, n_used, w_gu, b_gu, w_dn, b_dn):
    n_rows, d = x_sorted.shape
    n_e, _, f2 = w_gu.shape
    grid_spec = pltpu.PrefetchScalarGridSpec(
        num_scalar_prefetch=2,
        grid=(n_rows // MOE_BM,),
        in_specs=[pl.BlockSpec((MOE_BM, d), lambda b, be, nb: (b, 0)),
                  pl.BlockSpec((1, d, f2), lambda b, be, nb: (be[b], 0, 0)),
                  pl.BlockSpec((1, 1, f2), lambda b, be, nb: (be[b], 0, 0)),
                  pl.BlockSpec((1, f2 // 2, d), lambda b, be, nb: (be[b], 0, 0)),
                  pl.BlockSpec((1, 1, d), lambda b, be, nb: (be[b], 0, 0))],
        out_specs=pl.BlockSpec((MOE_BM, d), lambda b, be, nb: (b, 0)),
        scratch_shapes=[pltpu.VMEM((d, f2), BF16), pltpu.VMEM((f2 // 2, d), BF16)],
    )
    return pl.pallas_call(
        functools.partial(_moe_kernel, d_ff=f2 // 2),
        grid_spec=grid_spec,
        out_shape=jax.ShapeDtypeStruct((n_rows, d), BF16),
        compiler_params=_cparams(("arbitrary",)),
        name="moe_experts",
    )(block_e, n_used, x_sorted, w_gu, b_gu.reshape(n_e, 1, f2), w_dn, b_dn.reshape(n_e, 1, d))


def _moe_combine_kernel(y0_ref, y1_ref, y2_ref, y3_ref, g_ref, h_ref, mod_ref, lnw_ref, lnb_ref, o_ref, *, alpha):
    g = g_ref[...]
    f = jnp.zeros(h_ref.shape, F32)
    for i, y_ref in enumerate((y0_ref, y1_ref, y2_ref, y3_ref)):
        f = f + g[:, i:i + 1] * y_ref[...].astype(F32)
    x = alpha * h_ref[...] + mod_ref[0, 5:6, :] * f
    o_ref[...] = _layer_norm(x, lnw_ref[...], lnb_ref[...])


def _moe_combine(ys, gates, h1, mod, nct, ln_w, ln_b, alpha):
    n, d = h1.shape
    row = lambda width: pl.BlockSpec((TM, width), lambda m: (m, 0))
    return pl.pallas_call(
        functools.partial(_moe_combine_kernel, alpha=alpha),
        grid=(n // TM,),
        in_specs=[row(d)] * TOP_K + [row(TOP_K), row(d),
                                     pl.BlockSpec((1, 6, d), lambda m: (jnp.where(m < nct, 0, 1), 0, 0)),
                                     pl.BlockSpec((1, d), lambda m: (0, 0)), pl.BlockSpec((1, d), lambda m: (0, 0))],
        out_specs=row(d),
        out_shape=jax.ShapeDtypeStruct((n, d), F32),
        compiler_params=_cparams(("parallel",)),
        name="moe_combine_ln",
    )(*ys, gates, h1, mod, ln_w.reshape(1, d), ln_b.reshape(1, d))


def _sc_gather_rows(table, idx):
    n_rows = idx.shape[0]
    d = table.shape[1]
    n_workers = SC_CORES * SC_SUBCORES
    per_worker = n_rows // n_workers
    assert per_worker * n_workers == n_rows and per_worker % SUBLANES == 0
    step = math.gcd(per_worker, SC_GATHER_ROWS)
    mesh = plsc.VectorSubcoreMesh(core_axis_name="c", subcore_axis_name="s",
                                  num_cores=SC_CORES, num_subcores=SC_SUBCORES)

    def body(table_hbm, idx_hbm, out_hbm, idx_v, rows_v, sem):
        base = (lax.axis_index("s") * SC_CORES + lax.axis_index("c")) * per_worker

        @pl.loop(0, per_worker // step)
        def _(i):
            off = pl.multiple_of(base + i * step, SUBLANES)
            pltpu.sync_copy(idx_hbm.at[pl.ds(off, step)], idx_v)
            pltpu.async_copy(table_hbm.at[idx_v], rows_v, sem).wait()
            pltpu.sync_copy(rows_v, out_hbm.at[pl.ds(off, step)])

    return pl.kernel(
        body,
        out_type=jax.ShapeDtypeStruct((n_rows, d), table.dtype),
        mesh=mesh,
        scratch_types=[pltpu.VMEM((step,), jnp.int32), pltpu.VMEM((step, d), table.dtype),
                       pltpu.SemaphoreType.DMA],
        name="sc_gather_rows",
    )(table, idx)


def _moe(v_rows, route, counts, n_e, h1, mod, nct, ln_w, ln_b, alpha, w_gu, b_gu, w_dn, b_dn):
    n, d = v_rows.shape
    top_idx = route[:, :TOP_K].astype(jnp.int32)
    rank = route[:, TOP_K:2 * TOP_K].astype(jnp.int32)
    gates = route[:, 2 * TOP_K:3 * TOP_K]
    counts = counts[0, :n_e].astype(jnp.int32)
    nk = n * TOP_K
    padded = (counts + MOE_BM - 1) // MOE_BM * MOE_BM
    pad_end = jnp.cumsum(padded)
    pad_start = pad_end - padded
    dest = (pad_start[top_idx] + rank).reshape(nk)
    n_blocks = -(-nk // MOE_BM) + n_e
    n_rows = n_blocks * MOE_BM
    row_tok = jnp.zeros((n_rows,), jnp.int32).at[dest].set(jnp.arange(nk, dtype=jnp.int32) // TOP_K)
    block_row0 = jnp.arange(n_blocks, dtype=jnp.int32) * MOE_BM
    block_e = jnp.minimum(jnp.sum(pad_end[None, :] <= block_row0[:, None], axis=1, dtype=jnp.int32), n_e - 1)
    n_used = (pad_end[-1:] // MOE_BM).astype(jnp.int32)
    x_sorted = _sc_gather_rows(v_rows, row_tok)
    yb = _moe_experts(x_sorted, block_e, n_used, w_gu, b_gu, w_dn, b_dn)
    dest2 = dest.reshape(n, TOP_K)
    ys = [yb[dest2[:, i]] for i in range(TOP_K)]
    return _moe_combine(ys, gates, h1, mod, nct, ln_w, ln_b, alpha)


def _rope_tables(l_ctx, l_lat):
    rows = l_lat // GRID_W
    row = jnp.repeat(jnp.arange(rows, dtype=F32), GRID_W)
    col = jnp.tile(jnp.arange(GRID_W, dtype=F32), rows)
    axis_dim = DIFF_HEAD_DIM // 2
    inv_freq = ROPE_THETA ** (-jnp.arange(0, axis_dim, 2, dtype=F32) / axis_dim)
    ang_r = row[:, None] * inv_freq
    ang_c = col[:, None] * inv_freq
    ang = jnp.concatenate([ang_r, ang_r, ang_c, ang_c], -1)
    ang = jnp.concatenate([jnp.zeros((l_ctx, DIFF_HEAD_DIM), F32), ang], 0)
    ang = jnp.concatenate([ang, ang], -1)
    return jnp.cos(ang), jnp.sin(ang)


def _gdn_scalars(gates, n_vh):
    n = gates.shape[0]
    n_qk = n_vh // 2
    nc = n // GDN_CHUNK
    gc = gates[:, :2 * n_vh].reshape(n, 2, n_qk, 2)
    beta = gates[:, 2 * n_vh:4 * n_vh].reshape(n, 2, n_qk, 2)
    gcc = gc.reshape(nc, GDN_CHUNK, 2, n_qk, 2)
    gl = jnp.stack([gcc[:, -1, 0], gcc[:, 0, 1]], axis=1)
    gl_tok = jnp.repeat(gl, GDN_CHUNK, axis=0)
    col = jnp.concatenate([gc, beta, gl_tok], axis=-1)
    scol = jnp.transpose(col, (1, 2, 0, 3))
    srow = jnp.transpose(col[..., :4], (1, 2, 3, 0))
    srowc = jnp.transpose(col[..., :4].reshape(nc, GDN_CHUNK, 2, n_qk, 4), (2, 3, 0, 4, 1))
    glrow = jnp.broadcast_to(jnp.transpose(gl, (1, 2, 0, 3))[..., None], (2, n_qk, nc, 2, GDN_HEAD_DIM))
    glrow = glrow.reshape(2, n_qk, nc, 1, 2 * GDN_HEAD_DIM)
    return scol, srow, srowc, glrow


def kernel(x, c, ctx, c_ctx, ada_w, ada_b, ln_w, ln_b, gdn_w_in, gdn_conv_w, gdn_a_log, gdn_dt_bias, gdn_norm_w,
           gdn_w_out, diff_w_in, diff_lambda, diff_subln_w, diff_w_out, router_w, router_b, moe_w_gate_up,
           moe_b_gate_up, moe_w_down, moe_b_down):
    batch, l_lat, d = x.shape
    l_ctx = ctx.shape[1]
    depth = ada_w.shape[0]
    assert batch == 1 and l_ctx % TM == 0 and l_lat % TM == 0 and l_lat % GRID_W == 0
    nct = l_ctx // TM
    n_e = router_w.shape[-1]
    alpha = (2 * depth) ** 0.25
    n_vh = gdn_a_log.shape[-1]
    v_dim = n_vh * GDN_HEAD_DIM
    conv_dim = gdn_conv_w.shape[-1]
    qk_dim = (conv_dim - v_dim) // 2
    n_qk = qk_dim // GDN_HEAD_DIM

    h = jnp.concatenate([ctx[0], x[0]], axis=0)
    n = h.shape[0]
    mods = _modulations(c_ctx, c, ada_w, ada_b)
    cos, sin = _rope_tables(l_ctx, l_lat)

    for i in range(depth):
        j = i // 2
        mod = mods[i]
        if i % 2 == 0:
            w_in = gdn_w_in[j]
            w_main = w_in[:, :conv_dim + v_dim].astype(BF16)
            w_ab = jnp.zeros((d, LANES), F32).at[:, :4 * n_vh].set(w_in[:, conv_dim + v_dim:]).astype(BF16)
            p, ab = _project(h, mod, nct, [w_main, w_ab], [BF16, F32])
            gates = _gdn_gates(ab, gdn_a_log[j], gdn_dt_bias[j])
            feat = _gdn_features(p, gdn_conv_w[j], nct, qk_dim)
            kt = _transposed_keys(feat, qk_dim)
            scol, srow, srowc, glrow = _gdn_scalars(gates, n_vh)
            o_f, o_b = _gdn_scan(feat, kt, scol, srow, srowc, glrow, nct)
            acts = (o_f, o_b, p, gdn_norm_w[j], conv_dim // v_dim)
            h1, v, route, counts = _mixer_out("gdn", acts, gdn_w_out[j].astype(BF16), h, mod, nct, ln_w[i, 0], ln_b[i, 0],
                                       router_w[i], router_b[i], alpha)
        else:
            lambda_init = 0.8 - 0.6 * math.exp(-0.3 * i)
            qkv = _qkv_rope(h, mod, nct, diff_w_in[j].astype(BF16), cos, sin)
            a = _diff_attention(qkv, diff_lambda[j], diff_subln_w[j], nct, l_ctx, lambda_init)
            h1, v, route, counts = _mixer_out("attn", (a,), diff_w_out[j].astype(BF16), h, mod, nct, ln_w[i, 0], ln_b[i, 0],
                                       router_w[i], router_b[i], alpha)
        h = _moe(v, route, counts, n_e, h1, mod, nct, ln_w[i, 1], ln_b[i, 1], alpha,
                 moe_w_gate_up[i], moe_b_gate_up[i], moe_w_down[i], moe_b_down[i])
    return h[l_ctx:].reshape(batch, l_lat, d)
```

```python
import functools
import math

import jax
import jax.numpy as jnp
from jax import lax
from jax.experimental import pallas as pl
from jax.experimental.pallas import tpu as pltpu
from jax.experimental.pallas import tpu_sc as plsc

F32 = jnp.float32
BF16 = jnp.bfloat16
HIGHEST = lax.Precision.HIGHEST

GRID_W = 64
GDN_HEAD_DIM = 128
GDN_CHUNK = 64
DIFF_HEAD_DIM = 64
ROPE_THETA = 10000.0
TOP_K = 4
SWIGLU_LIMIT = 7.0
SWIGLU_ALPHA = 1.702
NORM_EPS = 1e-5
L2_EPS = 1e-6

LANES = 128
SUBLANES = 8
HALO = 16
VMEM_LIMIT = 56 * 1024 * 1024

TM = 256
MOE_BM = 256
ATTN_TK = 1024

SC_CORES = 2
SC_SUBCORES = 16
SC_GATHER_ROWS = 32


def _cparams(sem):
    return pltpu.CompilerParams(dimension_semantics=sem, vmem_limit_bytes=VMEM_LIMIT)


def _silu(x):
    return x * jax.nn.sigmoid(x)


def _dot(a, b):
    return jnp.dot(a, b, preferred_element_type=F32)


def _dot_nt(a, b):
    return lax.dot_general(a, b, (((1,), (1,)), ((), ())), preferred_element_type=F32)


def _modulate(h, mod_ref, shift, scale):
    return h * (1.0 + mod_ref[0, scale:scale + 1, :]) + mod_ref[0, shift:shift + 1, :]


def _pack_bf16_pairs(x):
    bits = pltpu.bitcast(x.astype(BF16).astype(F32), jnp.uint32)
    half = x.shape[1] // 2
    return (bits[:, half:] & jnp.uint32(0xFFFF0000)) | (bits[:, :half] >> 16)


def _unpack_bf16_pairs(w):
    lo = pltpu.bitcast(w << 16, F32)
    hi = pltpu.bitcast(w & jnp.uint32(0xFFFF0000), F32)
    return jnp.concatenate([lo, hi], axis=1).astype(BF16)


def _layer_norm(x, w, b):
    mu = jnp.mean(x, axis=-1, keepdims=True)
    xc = x - mu
    var = jnp.mean(xc * xc, axis=-1, keepdims=True)
    return xc * lax.rsqrt(var + NORM_EPS) * w + b


def _mod_kernel(s_ref, w_ref, b_ref, o_ref):
    s = _silu(s_ref[...])
    o_ref[0] = jnp.dot(s, w_ref[0], preferred_element_type=F32, precision=HIGHEST) + b_ref[0]


def _modulations(c_ctx, c, ada_w, ada_b):
    depth, d, d6 = ada_w.shape
    tn = d6 // 4
    s = jnp.zeros((SUBLANES, d), F32).at[0].set(c_ctx).at[1].set(c[0])
    out = pl.pallas_call(
        _mod_kernel,
        grid=(depth, d6 // tn),
        in_specs=[pl.BlockSpec((SUBLANES, d), lambda i, n: (0, 0)),
                  pl.BlockSpec((1, d, tn), lambda i, n: (i, 0, n)),
                  pl.BlockSpec((1, 1, tn), lambda i, n: (i, 0, n))],
        out_specs=pl.BlockSpec((1, SUBLANES, tn), lambda i, n: (i, 0, n)),
        out_shape=jax.ShapeDtypeStruct((depth, SUBLANES, d6), F32),
        compiler_params=_cparams(("parallel", "parallel")),
        name="adaln_mod",
    )(s, ada_w, ada_b.reshape(depth, 1, d6))
    return out[:, :2].reshape(depth, 2, 6, d)


def _proj_kernel(h_ref, mod_ref, *refs, n_w, chunk):
    w_refs, o_refs = refs[:n_w], refs[n_w:]
    u = _modulate(h_ref[...], mod_ref, 0, 1).astype(BF16)
    for w_ref, o_ref in zip(w_refs, o_refs):
        n = w_ref.shape[1]
        for j in range(0, n, chunk):
            jc = min(chunk, n - j)
            o_ref[:, j:j + jc] = _dot(u, w_ref[:, j:j + jc]).astype(o_ref.dtype)


def _project(h, mod, nct, weights, out_dtypes):
    n, d = h.shape
    return pl.pallas_call(
        functools.partial(_proj_kernel, n_w=len(weights), chunk=512),
        grid=(n // TM,),
        in_specs=[pl.BlockSpec((TM, d), lambda m: (m, 0)),
                  pl.BlockSpec((1, 6, d), lambda m: (jnp.where(m < nct, 0, 1), 0, 0))]
                 + [pl.BlockSpec(w.shape, lambda m: (0, 0)) for w in weights],
        out_specs=[pl.BlockSpec((TM, w.shape[1]), lambda m: (m, 0)) for w in weights],
        out_shape=[jax.ShapeDtypeStruct((n, w.shape[1]), dt) for w, dt in zip(weights, out_dtypes)],
        compiler_params=_cparams(("parallel",)),
        name="mod_proj",
    )(h, mod, *weights)


def _qkv_rope_kernel(h_ref, mod_ref, w_ref, cos_ref, sin_ref, o_ref, *, chunk, n_rope, n_q, q_scale):
    u = _modulate(h_ref[...], mod_ref, 0, 1).astype(BF16)
    rep = chunk // LANES
    cos = jnp.concatenate([cos_ref[...]] * rep, axis=1)
    sin = jnp.concatenate([sin_ref[...]] * rep, axis=1)
    lane = lax.broadcasted_iota(jnp.int32, (h_ref.shape[0], chunk), 1)
    first = (lane % 32) < 16
    for j in range(w_ref.shape[1] // chunk):
        y = _dot(u, w_ref[:, j * chunk:(j + 1) * chunk])
        if j < n_rope:
            rot = jnp.where(first, -pltpu.roll(y, chunk - 16, 1), pltpu.roll(y, 16, 1))
            y = y * cos + rot * sin
            if j < n_q:
                y = y * q_scale
        o_ref[:, j * chunk:(j + 1) * chunk] = y.astype(o_ref.dtype)


def _qkv_rope(h, mod, nct, w, cos, sin):
    n, d = h.shape
    chunk = 512
    return pl.pallas_call(
        functools.partial(_qkv_rope_kernel, chunk=chunk, n_rope=2 * d // chunk, n_q=d // chunk,
                          q_scale=DIFF_HEAD_DIM ** -0.5 * math.log2(math.e)),
        grid=(n // TM,),
        in_specs=[pl.BlockSpec((TM, d), lambda m: (m, 0)),
                  pl.BlockSpec((1, 6, d), lambda m: (jnp.where(m < nct, 0, 1), 0, 0)),
                  pl.BlockSpec(w.shape, lambda m: (0, 0)),
                  pl.BlockSpec((TM, LANES), lambda m: (m, 0)),
                  pl.BlockSpec((TM, LANES), lambda m: (m, 0))],
        out_specs=pl.BlockSpec((TM, w.shape[1]), lambda m: (m, 0)),
        out_shape=jax.ShapeDtypeStruct((n, w.shape[1]), BF16),
        compiler_params=_cparams(("parallel",)),
        name="qkv_rope",
    )(h, mod, w, cos, sin)


def _gdn_gates_kernel(ab_ref, alog_ref, dtb_ref, o_ref, *, chunk, n_heads):
    x = ab_ref[...]
    t = x + dtb_ref[...]
    softplus = jnp.maximum(t, 0.0) + jnp.log1p(jnp.exp(-jnp.abs(t)))
    g = -jnp.exp(alog_ref[...]) * softplus
    beta = jax.nn.sigmoid(x)
    tm = x.shape[0]
    r = lax.broadcasted_iota(jnp.int32, (tm, tm), 0)
    c = lax.broadcasted_iota(jnp.int32, (tm, tm), 1)
    same = (r // chunk) == (c // chunk)
    t_fwd = jnp.where(same & (r >= c), 1.0, 0.0).astype(F32)
    t_bwd = jnp.where(same & (r <= c), 1.0, 0.0).astype(F32)
    g_fwd = jnp.dot(t_fwd, g, preferred_element_type=F32, precision=HIGHEST)
    g_bwd = jnp.dot(t_bwd, g, preferred_element_type=F32, precision=HIGHEST)
    lane = lax.broadcasted_iota(jnp.int32, x.shape, 1)
    o_ref[...] = jnp.where(lane < n_heads, g_fwd, jnp.where(lane < 2 * n_heads, g_bwd, beta))


def _gdn_gates(ab, a_log, dt_bias):
    n = ab.shape[0]
    nh = a_log.shape[-1]
    pad = lambda v: jnp.zeros((1, LANES), F32).at[0, :2 * nh].set(v.reshape(-1))
    return pl.pallas_call(
        functools.partial(_gdn_gates_kernel, chunk=GDN_CHUNK, n_heads=nh),
        grid=(n // TM,),
        in_specs=[pl.BlockSpec((TM, LANES), lambda m: (m, 0)),
                  pl.BlockSpec((1, LANES), lambda m: (0, 0)),
                  pl.BlockSpec((1, LANES), lambda m: (0, 0))],
        out_specs=pl.BlockSpec((TM, LANES), lambda m: (m, 0)),
        out_shape=jax.ShapeDtypeStruct((n, LANES), F32),
        compiler_params=_cparams(("parallel",)),
        name="gdn_gates",
    )(ab, pad(a_log), pad(dt_bias))


def _gdn_feat_kernel(x_ref, prev_ref, next_ref, w_ref, o_ref, xs_ref, *, nct, n_tiles, n_norm, n_q, q_scale, width):
    m = pl.program_id(0)
    j = pl.program_id(1)
    tm, tn = x_ref.shape
    pad = width // 2
    prev_ok = jnp.logical_and(m != 0, m != nct)
    next_ok = jnp.logical_and(m != nct - 1, m != n_tiles - 1)
    xs_ref[0:HALO, :] = jnp.where(prev_ok, prev_ref[...].astype(F32), 0.0)
    xs_ref[HALO:HALO + tm, :] = x_ref[...].astype(F32)
    xs_ref[HALO + tm:, :] = jnp.where(next_ok, next_ref[...].astype(F32), 0.0)
    acc = jnp.zeros((tm, tn), F32)
    for t in range(width):
        acc = acc + w_ref[t:t + 1, :] * xs_ref[HALO - pad + t:HALO - pad + t + tm, :]
    y = _silu(acc)

    @pl.when(j >= n_norm)
    def _():
        o_ref[...] = y.astype(o_ref.dtype)

    @pl.when(j < n_norm)
    def _():
        scale = jnp.where(j < n_q, q_scale, 1.0).astype(F32)
        for hh in range(tn // GDN_HEAD_DIM):
            sl = slice(hh * GDN_HEAD_DIM, (hh + 1) * GDN_HEAD_DIM)
            yh = y[:, sl]
            inv = lax.rsqrt(jnp.sum(yh * yh, axis=-1, keepdims=True) + L2_EPS)
            o_ref[:, sl] = (yh * (inv * scale)).astype(o_ref.dtype)


def _gdn_features(p, conv_w, nct, qk_dim):
    n = p.shape[0]
    width, conv_dim = conv_w.shape
    tn = 512
    n_tiles = n // TM
    rb = TM // HALO
    last_rb = n // HALO - 1
    return pl.pallas_call(
        functools.partial(_gdn_feat_kernel, nct=nct, n_tiles=n_tiles, n_norm=2 * qk_dim // tn, n_q=qk_dim // tn,
                          q_scale=GDN_HEAD_DIM ** -0.5, width=width),
        grid=(n_tiles, conv_dim // tn),
        in_specs=[pl.BlockSpec((TM, tn), lambda m, j: (m, j)),
                  pl.BlockSpec((HALO, tn), lambda m, j: (jnp.maximum(m * rb - 1, 0), j)),
                  pl.BlockSpec((HALO, tn), lambda m, j: (jnp.minimum((m + 1) * rb, last_rb), j)),
                  pl.BlockSpec((width, tn), lambda m, j: (0, j))],
        out_specs=pl.BlockSpec((TM, tn), lambda m, j: (m, j)),
        out_shape=jax.ShapeDtypeStruct((n, conv_dim), BF16),
        scratch_shapes=[pltpu.VMEM((TM + 2 * HALO, tn), F32)],
        compiler_params=_cparams(("parallel", "parallel")),
        name="gdn_features",
    )(p, p, p, conv_w)


def _gdn_scan_kernel(*refs, chunk):
    ins, o_refs, s_ref = refs[:16], refs[16:18], refs[18]
    t = pl.program_id(1)

    @pl.when(t == 0)
    def _():
        s_ref[...] = jnp.zeros_like(s_ref)

    rows = ins[0].shape[0]
    hd = GDN_HEAD_DIM
    n_chunk = rows // chunk
    n_double = int(math.log2(chunk)) - 1
    ri = lax.broadcasted_iota(jnp.int32, (rows, rows), 0)
    ci = lax.broadcasted_iota(jnp.int32, (rows, rows), 1)
    same = (ri // chunk) == (ci // chunk)
    eye = jnp.where(ri == ci, 1.0, 0.0).astype(F32)
    r64 = lax.broadcasted_iota(jnp.int32, (chunk, chunk), 0)
    c64 = lax.broadcasted_iota(jnp.int32, (chunk, chunk), 1)

    dirs = []
    for d in range(2):
        q_ref, k_ref, kt_ref, v_ref, scol_ref, srow_ref, srowc_ref, glrow_ref = ins[8 * d:8 * d + 8]
        k = k_ref[...]
        dirs.append(dict(
            q=q_ref[...], k=k, v=v_ref[...], kt=kt_ref[...], srowc_ref=srowc_ref, glrow_ref=glrow_ref,
            scol=scol_ref[0, 0],
            srow=srow_ref[0, 0],
            incl=same & ((ri <= ci) if d else (ri >= ci)), strict=same & ((ri < ci) if d else (ri > ci)),
            incl64=(r64 <= c64) if d else (r64 >= c64),
            kk=_dot_nt(k, k),
            order=list(range(n_chunk - 1, -1, -1)) if d else list(range(n_chunk))))
    chains = [(d, a) for d in range(2) for a in range(2)]
    gcol = {c: dirs[c[0]]["scol"][:, c[1]:c[1] + 1] for c in chains}
    bcol = {c: dirs[c[0]]["scol"][:, 2 + c[1]:3 + c[1]] for c in chains}
    e_g = {c: jnp.exp(gcol[c]) for c in chains}
    e_k = {c: jnp.exp(dirs[c[0]]["scol"][:, 4 + c[1]:5 + c[1]] - gcol[c]) for c in chains}
    neg_l = {c: jnp.where(dirs[c[0]]["strict"],
                          -(bcol[c] * dirs[c[0]]["kk"]
                            * jnp.exp(jnp.where(dirs[c[0]]["incl"], gcol[c] - dirs[c[0]]["srow"][c[1]:c[1] + 1, :],
                                                -jnp.inf))), 0.0) for c in chains}
    inv = {c: eye + neg_l[c] for c in chains}
    pw = {c: neg_l[c].astype(BF16) for c in chains}
    for _ in range(n_double):
        pw = {c: _dot(pw[c], pw[c]).astype(BF16) for c in chains}
        inv = {c: inv[c] + _dot(inv[c].astype(BF16), pw[c]) for c in chains}
    rhs = {c: jnp.concatenate([dirs[c[0]]["v"][:, c[1] * hd:(c[1] + 1) * hd].astype(F32) * bcol[c],
                               dirs[c[0]]["k"].astype(F32) * (bcol[c] * e_g[c])], axis=1).astype(BF16) for c in chains}
    uw = {c: _dot(inv[c].astype(BF16), rhs[c]) for c in chains}

    state = [s_ref[d] for d in range(2)]
    for i in range(n_chunk):
        js = [dirs[d]["order"][i] for d in range(2)]
        rs = [slice(j * chunk, (j + 1) * chunk) for j in js]
        s_b = [st.astype(BF16) for st in state]
        qk_c = [_dot_nt(dirs[d]["q"][rs[d]], dirs[d]["k"][rs[d]]) for d in range(2)]
        q_s = [_dot(dirs[d]["q"][rs[d]], s_b[d]) for d in range(2)]
        v_new = {(d, a): uw[d, a][rs[d], :hd] - _dot(uw[d, a][rs[d], hd:].astype(BF16), s_b[d][:, a * hd:(a + 1) * hd])
                 for d, a in chains}
        qkd = {(d, a): (qk_c[d] * jnp.exp(jnp.where(dirs[d]["incl64"],
                                                    gcol[d, a][rs[d]] - dirs[d]["srowc_ref"][0, 0, js[d], a:a + 1, :],
                                                    -jnp.inf))).astype(BF16) for d, a in chains}
        for d, a in chains:
            o = e_g[d, a][rs[d]] * q_s[d][:, a * hd:(a + 1) * hd] + _dot(qkd[d, a], v_new[d, a].astype(BF16))
            o_refs[d][rs[d], a * hd:(a + 1) * hd] = o.astype(o_refs[d].dtype)
        for d in range(2):
            v_s = jnp.concatenate([v_new[d, a] * e_k[d, a][rs[d]] for a in range(2)], axis=1).astype(BF16)
            parts = [jnp.zeros((js[d] * chunk, 2 * hd), BF16), v_s,
                     jnp.zeros(((n_chunk - 1 - js[d]) * chunk, 2 * hd), BF16)]
            v_pad = jnp.concatenate([p for p in parts if p.shape[0]], axis=0)
            e_l = jnp.exp(dirs[d]["glrow_ref"][0, 0, js[d]])
            state[d] = state[d] * e_l + _dot(dirs[d]["kt"], v_pad)
    for d in range(2):
        s_ref[d] = state[d]


def _transpose_kernel(x_ref, o_ref):
    o_ref[...] = x_ref[...].astype(F32).T.astype(o_ref.dtype)


def _transposed_keys(feat, qk_dim):
    n = feat.shape[0]
    tn = 512
    return pl.pallas_call(
        _transpose_kernel,
        grid=(n // TM, qk_dim // tn),
        in_specs=[pl.BlockSpec((TM, tn), lambda m, j: (m, qk_dim // tn + j))],
        out_specs=pl.BlockSpec((tn, TM), lambda m, j: (j, m)),
        out_shape=jax.ShapeDtypeStruct((qk_dim, n), feat.dtype),
        compiler_params=_cparams(("parallel", "parallel")),
        name="gdn_keys_t",
    )(feat)


def _gdn_scan(feat, kt, scol, srow, srowc, glrow, nct):
    n = feat.shape[0]
    hd = GDN_HEAD_DIM
    n_qk = kt.shape[0] // hd
    n_tiles = n // TM
    cpt = TM // GDN_CHUNK

    def tile(d, t):
        return jnp.where(t < nct, nct - 1 - t, n_tiles - 1 - (t - nct)) if d else t

    def specs(d):
        return [pl.BlockSpec((TM, hd), lambda h, t: (tile(d, t), h)),
                pl.BlockSpec((TM, hd), lambda h, t: (tile(d, t), n_qk + h)),
                pl.BlockSpec((hd, TM), lambda h, t: (h, tile(d, t))),
                pl.BlockSpec((TM, 2 * hd), lambda h, t: (tile(d, t), n_qk + h)),
                pl.BlockSpec((1, 1, TM, 6), lambda h, t: (d, h, tile(d, t), 0)),
                pl.BlockSpec((1, 1, 4, TM), lambda h, t: (d, h, 0, tile(d, t))),
                pl.BlockSpec((1, 1, cpt, 4, GDN_CHUNK), lambda h, t: (d, h, tile(d, t), 0, 0)),
                pl.BlockSpec((1, 1, cpt, 1, 2 * hd), lambda h, t: (d, h, tile(d, t), 0, 0))]

    args = [feat, feat, kt, feat, scol, srow, srowc, glrow]
    out = jax.ShapeDtypeStruct((n, 2 * n_qk * hd), BF16)
    return pl.pallas_call(
        functools.partial(_gdn_scan_kernel, chunk=GDN_CHUNK),
        grid=(n_qk, n_tiles),
        in_specs=specs(0) + specs(1),
        out_specs=[pl.BlockSpec((TM, 2 * hd), lambda h, t: (tile(0, t), h)),
                   pl.BlockSpec((TM, 2 * hd), lambda h, t: (tile(1, t), h))],
        out_shape=[out, out],
        scratch_shapes=[pltpu.VMEM((2, hd, 2 * hd), F32)],
        compiler_params=_cparams(("parallel", "arbitrary")),
        name="gdn_scan",
    )(*args, *args)


def _post_mixer(y, h_ref, mod_ref, lnw_ref, lnb_ref, rw_ref, rb_ref, h1_ref, v_ref, rt_ref, cnt_ref, carry_ref,
                alpha, n_e):
    @pl.when(pl.program_id(0) == 0)
    def _():
        carry_ref[...] = jnp.zeros_like(carry_ref)

    x = alpha * h_ref[...] + mod_ref[0, 2:3, :] * y
    h1 = _layer_norm(x, lnw_ref[...], lnb_ref[...])
    h1_ref[...] = h1
    v = _modulate(h1, mod_ref, 3, 4)
    v_ref[...] = _pack_bf16_pairs(v)
    logits = jnp.dot(v, rw_ref[...], preferred_element_type=F32, precision=HIGHEST) + rb_ref[...]

    tm = logits.shape[0]
    lane = lax.broadcasted_iota(jnp.int32, logits.shape, 1)
    rest = jnp.where(lane < n_e, logits, -jnp.inf)
    top_val, top_idx, picked = [], [], []
    for _ in range(TOP_K):
        mx = jnp.max(rest, axis=-1, keepdims=True)
        idx = jnp.min(jnp.where(rest == mx, lane, LANES), axis=-1, keepdims=True)
        hit = lane == idx
        top_val.append(mx)
        top_idx.append(idx)
        picked.append(hit)
        rest = jnp.where(hit, -jnp.inf, rest)
    e = [jnp.exp(tv - top_val[0]) for tv in top_val]
    denom = functools.reduce(lambda a, b: a + b, e)
    onehot = functools.reduce(lambda a, b: a + b, [jnp.where(hit, 1.0, 0.0) for hit in picked])
    r = lax.broadcasted_iota(jnp.int32, (tm, tm), 0)
    c = lax.broadcasted_iota(jnp.int32, (tm, tm), 1)
    earlier = jnp.where(r > c, 1.0, 0.0).astype(BF16)
    before = _dot(earlier, onehot.astype(BF16)) + carry_ref[0:1, :]
    rt = jnp.zeros(logits.shape, F32)
    for k in range(TOP_K):
        rank = jnp.sum(jnp.where(picked[k], before, 0.0), axis=-1, keepdims=True)
        rt = jnp.where(lane == k, top_idx[k].astype(F32), rt)
        rt = jnp.where(lane == TOP_K + k, rank, rt)
        rt = jnp.where(lane == 2 * TOP_K + k, e[k] / denom, rt)
    rt_ref[...] = rt
    total = carry_ref[0:1, :] + jnp.sum(onehot, axis=0, keepdims=True)
    carry_ref[...] = jnp.broadcast_to(total, carry_ref.shape)
    cnt_ref[...] = jnp.broadcast_to(total, cnt_ref.shape)


def _gdn_out_kernel(of_ref, ob_ref, z_ref, nw_ref, w_ref, h_ref, mod_ref, lnw_ref, lnb_ref, rw_ref, rb_ref,
                    h1_ref, v_ref, rt_ref, cnt_ref, carry_ref, a_ref, *, alpha, n_e):
    hd = GDN_HEAD_DIM
    for hh in range(of_ref.shape[1] // hd):
        sl = slice(hh * hd, (hh + 1) * hd)
        o = of_ref[:, sl].astype(F32) + ob_ref[:, sl].astype(F32)
        o = o * lax.rsqrt(jnp.mean(o * o, axis=-1, keepdims=True) + NORM_EPS) * nw_ref[...]
        a_ref[:, sl] = (o * _silu(z_ref[:, sl].astype(F32))).astype(BF16)
    y = _dot(a_ref[...], w_ref[...])
    _post_mixer(y, h_ref, mod_ref, lnw_ref, lnb_ref, rw_ref, rb_ref, h1_ref, v_ref, rt_ref, cnt_ref, carry_ref,
                alpha, n_e)


def _attn_out_kernel(a_ref, w_ref, h_ref, mod_ref, lnw_ref, lnb_ref, rw_ref, rb_ref, h1_ref, v_ref, rt_ref, cnt_ref,
                     carry_ref, *, alpha, n_e):
    y = _dot(a_ref[...], w_ref[...])
    _post_mixer(y, h_ref, mod_ref, lnw_ref, lnb_ref, rw_ref, rb_ref, h1_ref, v_ref, rt_ref, cnt_ref, carry_ref,
                alpha, n_e)


def _mixer_out(kind, acts, w_out, h, mod, nct, ln_w, ln_b, router_w, router_b, alpha):
    n, d = h.shape
    row = lambda width: pl.BlockSpec((TM, width), lambda m: (m, 0))
    full = lambda arr: pl.BlockSpec(arr.shape, lambda m: (0,) * arr.ndim)
    n_e = router_w.shape[1]
    rw = jnp.zeros((d, LANES), F32).at[:, :n_e].set(router_w)
    rb = jnp.zeros((1, LANES), F32).at[0, :n_e].set(router_b)
    tail = [w_out, h, mod, ln_w.reshape(1, d), ln_b.reshape(1, d), rw, rb]
    tail_specs = [full(w_out), row(d), pl.BlockSpec((1, 6, d), lambda m: (jnp.where(m < nct, 0, 1), 0, 0)),
                  pl.BlockSpec((1, d), lambda m: (0, 0)), pl.BlockSpec((1, d), lambda m: (0, 0)), full(rw), full(rb)]
    if kind == "gdn":
        o_f, o_b, p, norm_w, z_col = acts
        v_dim = o_f.shape[1]
        body = functools.partial(_gdn_out_kernel, alpha=alpha, n_e=n_e)
        args = [o_f, o_b, p, norm_w.reshape(1, -1)] + tail
        specs = [row(v_dim), row(v_dim), pl.BlockSpec((TM, v_dim), lambda m: (m, z_col)),
                 pl.BlockSpec((1, norm_w.shape[0]), lambda m: (0, 0))] + tail_specs
        scratch = [pltpu.VMEM((SUBLANES, LANES), F32), pltpu.VMEM((TM, v_dim), BF16)]
    else:
        (a,) = acts
        body = functools.partial(_attn_out_kernel, alpha=alpha, n_e=n_e)
        args = [a] + tail
        specs = [row(a.shape[1])] + tail_specs
        scratch = [pltpu.VMEM((SUBLANES, LANES), F32)]
    return pl.pallas_call(
        body,
        grid=(n // TM,),
        in_specs=specs,
        out_specs=[row(d), row(d // 2), row(LANES), pl.BlockSpec((SUBLANES, LANES), lambda m: (0, 0))],
        out_shape=[jax.ShapeDtypeStruct((n, d), F32), jax.ShapeDtypeStruct((n, d // 2), jnp.uint32),
                   jax.ShapeDtypeStruct((n, LANES), F32), jax.ShapeDtypeStruct((SUBLANES, LANES), F32)],
        scratch_shapes=scratch,
        compiler_params=_cparams(("arbitrary",)),
        name=kind + "_out_ln",
    )(*args)


def _attn_kernel(q_ref, k_ref, v_ref, lam_ref, sw_ref, o_ref, acc_ref, m_ref, l_ref, sa_ref, sb_ref,
                 *, tk, nct, l_ctx, n_pairs, lambda_init):
    mt = pl.program_id(1)
    q = q_ref[...]
    lane = lax.broadcasted_iota(jnp.int32, q.shape, 1)
    zero = jnp.zeros_like(q)
    q_maps = (jnp.where(lane < DIFF_HEAD_DIM, q, zero), jnp.where(lane >= DIFF_HEAD_DIM, q, zero))
    acc_ref[...] = jnp.zeros_like(acc_ref)
    m_ref[...] = jnp.full_like(m_ref, -jnp.inf)
    l_ref[...] = jnp.zeros_like(l_ref)

    def scores(start, s_ref):
        kj = k_ref[pl.ds(start, tk), :]
        for i, qm in enumerate(q_maps):
            s_ref[i] = _dot_nt(qm, kj)

    def absorb(score_of_map, vj):
        for i in range(2):
            s = score_of_map(i)
            m_old = m_ref[i]
            m_new = jnp.maximum(m_old, jnp.max(s, axis=-1, keepdims=True))
            alpha = jnp.exp2(m_old - m_new)
            p = jnp.exp2(s - m_new)
            l_ref[i] = alpha * l_ref[i] + jnp.sum(p, axis=-1, keepdims=True)
            acc_ref[i] = alpha * acc_ref[i] + _dot(p.astype(BF16), vj)
            m_ref[i] = m_new

    k_ctx = k_ref[0:l_ctx, :]
    absorb(lambda i: _dot_nt(q_maps[i], k_ctx), v_ref[0:l_ctx, :])

    @pl.when(mt >= nct)
    def _():
        last = l_ctx + (2 * n_pairs - 1) * tk
        scores(l_ctx, sa_ref)

        def body(jj, carry):
            c0 = pl.multiple_of(l_ctx + 2 * jj * tk, LANES)
            c1 = pl.multiple_of(c0 + tk, LANES)
            c2 = pl.multiple_of(jnp.minimum(c1 + tk, last), LANES)
            scores(c1, sb_ref)
            absorb(lambda i: sa_ref[i], v_ref[pl.ds(c0, tk), :])
            scores(c2, sa_ref)
            absorb(lambda i: sb_ref[i], v_ref[pl.ds(c1, tk), :])
            return carry

        lax.fori_loop(0, n_pairs, body, 0)

    lam = lam_ref[...]
    lam_full = (jnp.exp(jnp.sum(lam[0:1] * lam[1:2], axis=-1, keepdims=True))
                - jnp.exp(jnp.sum(lam[2:3] * lam[3:4], axis=-1, keepdims=True)) + lambda_init)
    o = acc_ref[0] / l_ref[0] - lam_full * (acc_ref[1] / l_ref[1])
    o = o * lax.rsqrt(jnp.mean(o * o, axis=-1, keepdims=True) + NORM_EPS) * sw_ref[...]
    o_ref[...] = (o * (1.0 - lambda_init)).astype(o_ref.dtype)


def _diff_attention(qkv, lam, subln_w, nct, l_ctx, lambda_init):
    n = qkv.shape[0]
    d = qkv.shape[1] // 3
    vd = subln_w.shape[0]
    n_heads = d // vd
    l_lat = n - l_ctx
    tk = math.gcd(l_lat // 2, ATTN_TK)
    return pl.pallas_call(
        functools.partial(_attn_kernel, tk=tk, nct=nct, l_ctx=l_ctx, n_pairs=l_lat // (2 * tk),
                          lambda_init=lambda_init),
        grid=(n_heads, n // TM),
        in_specs=[pl.BlockSpec((TM, vd), lambda h, m: (m, h)),
                  pl.BlockSpec((n, vd), lambda h, m: (0, n_heads + h)),
                  pl.BlockSpec((n, vd), lambda h, m: (0, 2 * n_heads + h)),
                  pl.BlockSpec(lam.shape, lambda h, m: (0, 0)),
                  pl.BlockSpec((1, vd), lambda h, m: (0, 0))],
        out_specs=pl.BlockSpec((TM, vd), lambda h, m: (m, h)),
        out_shape=jax.ShapeDtypeStruct((n, d), BF16),
        scratch_shapes=[pltpu.VMEM((2, TM, vd), F32), pltpu.VMEM((2, TM, 1), F32), pltpu.VMEM((2, TM, 1), F32),
                        pltpu.VMEM((2, TM, tk), F32), pltpu.VMEM((2, TM, tk), F32)],
        compiler_params=_cparams(("parallel", "parallel")),
        name="diff_attn",
    )(qkv, qkv, qkv, lam, subln_w.reshape(1, vd))


def _moe_kernel(be_ref, nb_ref, x_ref, wgu_ref, bgu_ref, wdn_ref, bdn_ref, y_ref, wgu_s, wdn_s, *, d_ff):
    b = pl.program_id(0)
    changed = jnp.logical_or(b == 0, be_ref[b] != be_ref[jnp.maximum(b - 1, 0)])

    @pl.when(changed)
    def _():
        wgu_s[...] = wgu_ref[0, 0].astype(BF16)
        wdn_s[...] = wdn_ref[0, 0].astype(BF16)

    @pl.when(b < nb_ref[0])
    def _():
        h = _dot(_unpack_bf16_pairs(x_ref[...]), wgu_s[...]) + bgu_ref[0, 0]
        gate = jnp.minimum(h[:, :d_ff], SWIGLU_LIMIT)
        up = jnp.clip(h[:, d_ff:], -SWIGLU_LIMIT, SWIGLU_LIMIT)
        glu = gate * jax.nn.sigmoid(gate * SWIGLU_ALPHA)
        act = ((up + 1.0) * glu).astype(BF16)
        y_ref[...] = (_dot(act, wdn_s[...]) + bdn_ref[0, 0]).astype(y_ref.dtype)

    @pl.when(b >= nb_ref[0])
    def _():
        y_ref[...] = jnp.zeros_like(y_ref)


def _moe_experts(x_sorted, block_e, n_used, layer, w_gu, b_gu, w_dn, b_dn):
    n_rows = x_sorted.shape[0]
    depth, n_e, d, f2 = w_gu.shape
    grid_spec = pltpu.PrefetchScalarGridSpec(
        num_scalar_prefetch=2,
        grid=(n_rows // MOE_BM,),
        in_specs=[pl.BlockSpec((MOE_BM, d // 2), lambda b, be, nb: (b, 0)),
                  pl.BlockSpec((1, 1, d, f2), lambda b, be, nb: (layer, be[b], 0, 0)),
                  pl.BlockSpec((1, 1, 1, f2), lambda b, be, nb: (layer, be[b], 0, 0)),
                  pl.BlockSpec((1, 1, f2 // 2, d), lambda b, be, nb: (layer, be[b], 0, 0)),
                  pl.BlockSpec((1, 1, 1, d), lambda b, be, nb: (layer, be[b], 0, 0))],
        out_specs=pl.BlockSpec((MOE_BM, d), lambda b, be, nb: (b, 0)),
        scratch_shapes=[pltpu.VMEM((d, f2), BF16), pltpu.VMEM((f2 // 2, d), BF16)],
    )
    return pl.pallas_call(
        functools.partial(_moe_kernel, d_ff=f2 // 2),
        grid_spec=grid_spec,
        out_shape=jax.ShapeDtypeStruct((n_rows, d), BF16),
        compiler_params=_cparams(("arbitrary",)),
        name="moe_experts",
    )(block_e, n_used, x_sorted, w_gu, b_gu.reshape(depth, n_e, 1, f2), w_dn, b_dn.reshape(depth, n_e, 1, d))


def _moe_combine_kernel(y0_ref, y1_ref, y2_ref, y3_ref, g_ref, h_ref, mod_ref, lnw_ref, lnb_ref, o_ref, *, alpha):
    g = g_ref[...]
    f = jnp.zeros(h_ref.shape, F32)
    for i, y_ref in enumerate((y0_ref, y1_ref, y2_ref, y3_ref)):
        f = f + g[:, i:i + 1] * y_ref[...].astype(F32)
    x = alpha * h_ref[...] + mod_ref[0, 5:6, :] * f
    o_ref[...] = _layer_norm(x, lnw_ref[...], lnb_ref[...])


def _moe_combine(ys, gates, h1, mod, nct, ln_w, ln_b, alpha):
    n, d = h1.shape
    row = lambda width: pl.BlockSpec((TM, width), lambda m: (m, 0))
    return pl.pallas_call(
        functools.partial(_moe_combine_kernel, alpha=alpha),
        grid=(n // TM,),
        in_specs=[row(d)] * TOP_K + [row(TOP_K), row(d),
                                     pl.BlockSpec((1, 6, d), lambda m: (jnp.where(m < nct, 0, 1), 0, 0)),
                                     pl.BlockSpec((1, d), lambda m: (0, 0)), pl.BlockSpec((1, d), lambda m: (0, 0))],
        out_specs=row(d),
        out_shape=jax.ShapeDtypeStruct((n, d), F32),
        compiler_params=_cparams(("parallel",)),
        name="moe_combine_ln",
    )(*ys, gates, h1, mod, ln_w.reshape(1, d), ln_b.reshape(1, d))


def _sc_gather_rows(table, idx):
    n_rows = idx.shape[0]
    d = table.shape[1]
    n_workers = SC_CORES * SC_SUBCORES
    per_worker = n_rows // n_workers
    assert per_worker * n_workers == n_rows and per_worker % SUBLANES == 0
    step = math.gcd(per_worker, SC_GATHER_ROWS)
    mesh = plsc.VectorSubcoreMesh(core_axis_name="c", subcore_axis_name="s",
                                  num_cores=SC_CORES, num_subcores=SC_SUBCORES)

    def body(table_hbm, idx_hbm, out_hbm, idx_v, rows_v, sem):
        base = (lax.axis_index("s") * SC_CORES + lax.axis_index("c")) * per_worker

        @pl.loop(0, per_worker // step)
        def _(i):
            off = pl.multiple_of(base + i * step, SUBLANES)
            pltpu.sync_copy(idx_hbm.at[pl.ds(off, step)], idx_v)
            pltpu.async_copy(table_hbm.at[idx_v], rows_v, sem).wait()
            pltpu.sync_copy(rows_v, out_hbm.at[pl.ds(off, step)])

    return pl.kernel(
        body,
        out_type=jax.ShapeDtypeStruct((n_rows, d), table.dtype),
        mesh=mesh,
        scratch_types=[pltpu.VMEM((step,), jnp.int32), pltpu.VMEM((step, d), table.dtype),
                       pltpu.SemaphoreType.DMA],
        name="sc_gather_rows",
    )(table, idx)


def _moe(v_rows, route, counts, n_e, h1, mod, nct, ln_w, ln_b, alpha, layer, w_gu, b_gu, w_dn, b_dn):
    n = v_rows.shape[0]
    top_idx = route[:, :TOP_K].astype(jnp.int32)
    rank = route[:, TOP_K:2 * TOP_K].astype(jnp.int32)
    gates = route[:, 2 * TOP_K:3 * TOP_K]
    counts = counts[0, :n_e].astype(jnp.int32)
    nk = n * TOP_K
    padded = (counts + MOE_BM - 1) // MOE_BM * MOE_BM
    pad_end = jnp.cumsum(padded)
    pad_start = pad_end - padded
    dest = (pad_start[top_idx] + rank).reshape(nk)
    n_blocks = -(-nk // MOE_BM) + n_e
    n_rows = n_blocks * MOE_BM
    row_tok = jnp.zeros((n_rows,), jnp.int32).at[dest].set(jnp.arange(nk, dtype=jnp.int32) // TOP_K)
    block_row0 = jnp.arange(n_blocks, dtype=jnp.int32) * MOE_BM
    block_e = jnp.minimum(jnp.sum(pad_end[None, :] <= block_row0[:, None], axis=1, dtype=jnp.int32), n_e - 1)
    n_used = (pad_end[-1:] // MOE_BM).astype(jnp.int32)
    x_sorted = _sc_gather_rows(v_rows, row_tok)
    yb = _moe_experts(x_sorted, block_e, n_used, layer, w_gu, b_gu, w_dn, b_dn)
    dest2 = dest.reshape(n, TOP_K)
    ys = [yb[dest2[:, i]] for i in range(TOP_K)]
    return _moe_combine(ys, gates, h1, mod, nct, ln_w, ln_b, alpha)


def _rope_tables(l_ctx, l_lat):
    rows = l_lat // GRID_W
    row = jnp.repeat(jnp.arange(rows, dtype=F32), GRID_W)
    col = jnp.tile(jnp.arange(GRID_W, dtype=F32), rows)
    axis_dim = DIFF_HEAD_DIM // 2
    inv_freq = ROPE_THETA ** (-jnp.arange(0, axis_dim, 2, dtype=F32) / axis_dim)
    ang_r = row[:, None] * inv_freq
    ang_c = col[:, None] * inv_freq
    ang = jnp.concatenate([ang_r, ang_r, ang_c, ang_c], -1)
    ang = jnp.concatenate([jnp.zeros((l_ctx, DIFF_HEAD_DIM), F32), ang], 0)
    ang = jnp.concatenate([ang, ang], -1)
    return jnp.cos(ang), jnp.sin(ang)


def _gdn_scalars(gates, n_vh):
    n = gates.shape[0]
    n_qk = n_vh // 2
    nc = n // GDN_CHUNK
    gc = gates[:, :2 * n_vh].reshape(n, 2, n_qk, 2)
    beta = gates[:, 2 * n_vh:4 * n_vh].reshape(n, 2, n_qk, 2)
    gcc = gc.reshape(nc, GDN_CHUNK, 2, n_qk, 2)
    gl = jnp.stack([gcc[:, -1, 0], gcc[:, 0, 1]], axis=1)
    gl_tok = jnp.repeat(gl, GDN_CHUNK, axis=0)
    col = jnp.concatenate([gc, beta, gl_tok], axis=-1)
    scol = jnp.transpose(col, (1, 2, 0, 3))
    srow = jnp.transpose(col[..., :4], (1, 2, 3, 0))
    srowc = jnp.transpose(col[..., :4].reshape(nc, GDN_CHUNK, 2, n_qk, 4), (2, 3, 0, 4, 1))
    glrow = jnp.broadcast_to(jnp.transpose(gl, (1, 2, 0, 3))[..., None], (2, n_qk, nc, 2, GDN_HEAD_DIM))
    glrow = glrow.reshape(2, n_qk, nc, 1, 2 * GDN_HEAD_DIM)
    return scol, srow, srowc, glrow


def kernel(x, c, ctx, c_ctx, ada_w, ada_b, ln_w, ln_b, gdn_w_in, gdn_conv_w, gdn_a_log, gdn_dt_bias, gdn_norm_w,
           gdn_w_out, diff_w_in, diff_lambda, diff_subln_w, diff_w_out, router_w, router_b, moe_w_gate_up,
           moe_b_gate_up, moe_w_down, moe_b_down):
    batch, l_lat, d = x.shape
    l_ctx = ctx.shape[1]
    depth = ada_w.shape[0]
    assert batch == 1 and l_ctx % TM == 0 and l_lat % TM == 0 and l_lat % GRID_W == 0
    nct = l_ctx // TM
    n_e = router_w.shape[-1]
    alpha = (2 * depth) ** 0.25
    n_vh = gdn_a_log.shape[-1]
    v_dim = n_vh * GDN_HEAD_DIM
    conv_dim = gdn_conv_w.shape[-1]
    qk_dim = (conv_dim - v_dim) // 2
    n_qk = qk_dim // GDN_HEAD_DIM

    h = jnp.concatenate([ctx[0], x[0]], axis=0)
    n = h.shape[0]
    mods = _modulations(c_ctx, c, ada_w, ada_b)
    cos, sin = _rope_tables(l_ctx, l_lat)

    for i in range(depth):
        j = i // 2
        mod = mods[i]
        if i % 2 == 0:
            w_in = gdn_w_in[j]
            w_main = w_in[:, :conv_dim + v_dim].astype(BF16)
            w_ab = jnp.zeros((d, LANES), F32).at[:, :4 * n_vh].set(w_in[:, conv_dim + v_dim:]).astype(BF16)
            p, ab = _project(h, mod, nct, [w_main, w_ab], [BF16, F32])
            gates = _gdn_gates(ab, gdn_a_log[j], gdn_dt_bias[j])
            feat = _gdn_features(p, gdn_conv_w[j], nct, qk_dim)
            kt = _transposed_keys(feat, qk_dim)
            scol, srow, srowc, glrow = _gdn_scalars(gates, n_vh)
            o_f, o_b = _gdn_scan(feat, kt, scol, srow, srowc, glrow, nct)
            acts = (o_f, o_b, p, gdn_norm_w[j], conv_dim // v_dim)
            h1, v, route, counts = _mixer_out("gdn", acts, gdn_w_out[j].astype(BF16), h, mod, nct, ln_w[i, 0], ln_b[i, 0],
                                       router_w[i], router_b[i], alpha)
        else:
            lambda_init = 0.8 - 0.6 * math.exp(-0.3 * i)
            qkv = _qkv_rope(h, mod, nct, diff_w_in[j].astype(BF16), cos, sin)
            a = _diff_attention(qkv, diff_lambda[j], diff_subln_w[j], nct, l_ctx, lambda_init)
            h1, v, route, counts = _mixer_out("attn", (a,), diff_w_out[j].astype(BF16), h, mod, nct, ln_w[i, 0], ln_b[i, 0],
                                       router_w[i], router_b[i], alpha)
        h = _moe(v, route, counts, n_e, h1, mod, nct, ln_w[i, 1], ln_b[i, 1], alpha,
                 i, moe_w_gate_up, moe_b_gate_up, moe_w_down, moe_b_down)
    return h[l_ctx:].reshape(batch, l_lat, d)
```

```python
import functools
import math

import jax
import jax.numpy as jnp
from jax import lax
from jax.experimental import pallas as pl
from jax.experimental.pallas import tpu as pltpu
from jax.experimental.pallas import tpu_sc as plsc

F32 = jnp.float32
BF16 = jnp.bfloat16
HIGHEST = lax.Precision.HIGHEST

GRID_W = 64
GDN_HEAD_DIM = 128
GDN_CHUNK = 64
DIFF_HEAD_DIM = 64
ROPE_THETA = 10000.0
TOP_K = 4
SWIGLU_LIMIT = 7.0
SWIGLU_ALPHA = 1.702
NORM_EPS = 1e-5
L2_EPS = 1e-6

LANES = 128
SUBLANES = 8
HALO = 16
VMEM_LIMIT = 56 * 1024 * 1024

TM = 256
MOE_BM = 256
ATTN_TK = 1024

SC_CORES = 2
SC_SUBCORES = 16
SC_GATHER_ROWS = 32


def _cparams(sem):
    return pltpu.CompilerParams(dimension_semantics=sem, vmem_limit_bytes=VMEM_LIMIT)


def _silu(x):
    return x * jax.nn.sigmoid(x)


def _dot(a, b):
    return jnp.dot(a, b, preferred_element_type=F32)


def _dot_nt(a, b):
    return lax.dot_general(a, b, (((1,), (1,)), ((), ())), preferred_element_type=F32)


def _modulate(h, mod_ref, shift, scale):
    return h * (1.0 + mod_ref[0, scale:scale + 1, :]) + mod_ref[0, shift:shift + 1, :]


def _pack_bf16_pairs(x):
    bits = pltpu.bitcast(x.astype(BF16).astype(F32), jnp.uint32)
    half = x.shape[1] // 2
    return (bits[:, half:] & jnp.uint32(0xFFFF0000)) | (bits[:, :half] >> 16)


def _unpack_bf16_pairs(w):
    lo = pltpu.bitcast(w << 16, F32)
    hi = pltpu.bitcast(w & jnp.uint32(0xFFFF0000), F32)
    return jnp.concatenate([lo, hi], axis=1).astype(BF16)


def _layer_norm(x, w, b):
    mu = jnp.mean(x, axis=-1, keepdims=True)
    xc = x - mu
    var = jnp.mean(xc * xc, axis=-1, keepdims=True)
    return xc * lax.rsqrt(var + NORM_EPS) * w + b


def _mod_kernel(s_ref, w_ref, b_ref, o_ref):
    s = _silu(s_ref[...])
    o_ref[0] = jnp.dot(s, w_ref[0], preferred_element_type=F32, precision=HIGHEST) + b_ref[0]


def _modulations(c_ctx, c, ada_w, ada_b):
    depth, d, d6 = ada_w.shape
    tn = d6 // 4
    s = jnp.zeros((SUBLANES, d), F32).at[0].set(c_ctx).at[1].set(c[0])
    out = pl.pallas_call(
        _mod_kernel,
        grid=(depth, d6 // tn),
        in_specs=[pl.BlockSpec((SUBLANES, d), lambda i, n: (0, 0)),
                  pl.BlockSpec((1, d, tn), lambda i, n: (i, 0, n)),
                  pl.BlockSpec((1, 1, tn), lambda i, n: (i, 0, n))],
        out_specs=pl.BlockSpec((1, SUBLANES, tn), lambda i, n: (i, 0, n)),
        out_shape=jax.ShapeDtypeStruct((depth, SUBLANES, d6), F32),
        compiler_params=_cparams(("parallel", "parallel")),
        name="adaln_mod",
    )(s, ada_w, ada_b.reshape(depth, 1, d6))
    return out[:, :2].reshape(depth, 2, 6, d)


def _proj_kernel(h_ref, mod_ref, *refs, n_w, chunk):
    w_refs, o_refs = refs[:n_w], refs[n_w:]
    u = _modulate(h_ref[...], mod_ref, 0, 1).astype(BF16)
    for w_ref, o_ref in zip(w_refs, o_refs):
        n = w_ref.shape[1]
        for j in range(0, n, chunk):
            jc = min(chunk, n - j)
            o_ref[:, j:j + jc] = _dot(u, w_ref[:, j:j + jc]).astype(o_ref.dtype)


def _project(h, mod, nct, weights, out_dtypes):
    n, d = h.shape
    return pl.pallas_call(
        functools.partial(_proj_kernel, n_w=len(weights), chunk=512),
        grid=(n // TM,),
        in_specs=[pl.BlockSpec((TM, d), lambda m: (m, 0)),
                  pl.BlockSpec((1, 6, d), lambda m: (jnp.where(m < nct, 0, 1), 0, 0))]
                 + [pl.BlockSpec(w.shape, lambda m: (0, 0)) for w in weights],
        out_specs=[pl.BlockSpec((TM, w.shape[1]), lambda m: (m, 0)) for w in weights],
        out_shape=[jax.ShapeDtypeStruct((n, w.shape[1]), dt) for w, dt in zip(weights, out_dtypes)],
        compiler_params=_cparams(("parallel",)),
        name="mod_proj",
    )(h, mod, *weights)


def _qkv_rope_kernel(h_ref, mod_ref, w_ref, cos_ref, sin_ref, o_ref, *, chunk, n_rope, n_q, q_scale):
    u = _modulate(h_ref[...], mod_ref, 0, 1).astype(BF16)
    rep = chunk // LANES
    cos = jnp.concatenate([cos_ref[...]] * rep, axis=1)
    sin = jnp.concatenate([sin_ref[...]] * rep, axis=1)
    lane = lax.broadcasted_iota(jnp.int32, (h_ref.shape[0], chunk), 1)
    first = (lane % 32) < 16
    for j in range(w_ref.shape[1] // chunk):
        y = _dot(u, w_ref[:, j * chunk:(j + 1) * chunk])
        if j < n_rope:
            rot = jnp.where(first, -pltpu.roll(y, chunk - 16, 1), pltpu.roll(y, 16, 1))
            y = y * cos + rot * sin
            if j < n_q:
                y = y * q_scale
        o_ref[:, j * chunk:(j + 1) * chunk] = y.astype(o_ref.dtype)


def _qkv_rope(h, mod, nct, w, cos, sin):
    n, d = h.shape
    chunk = 512
    return pl.pallas_call(
        functools.partial(_qkv_rope_kernel, chunk=chunk, n_rope=2 * d // chunk, n_q=d // chunk,
                          q_scale=DIFF_HEAD_DIM ** -0.5 * math.log2(math.e)),
        grid=(n // TM,),
        in_specs=[pl.BlockSpec((TM, d), lambda m: (m, 0)),
                  pl.BlockSpec((1, 6, d), lambda m: (jnp.where(m < nct, 0, 1), 0, 0)),
                  pl.BlockSpec(w.shape, lambda m: (0, 0)),
                  pl.BlockSpec((TM, LANES), lambda m: (m, 0)),
                  pl.BlockSpec((TM, LANES), lambda m: (m, 0))],
        out_specs=pl.BlockSpec((TM, w.shape[1]), lambda m: (m, 0)),
        out_shape=jax.ShapeDtypeStruct((n, w.shape[1]), BF16),
        compiler_params=_cparams(("parallel",)),
        name="qkv_rope",
    )(h, mod, w, cos, sin)


def _gdn_gates_kernel(ab_ref, alog_ref, dtb_ref, o_ref, *, chunk, n_heads):
    x = ab_ref[...]
    t = x + dtb_ref[...]
    softplus = jnp.maximum(t, 0.0) + jnp.log1p(jnp.exp(-jnp.abs(t)))
    g = -jnp.exp(alog_ref[...]) * softplus
    beta = jax.nn.sigmoid(x)
    tm = x.shape[0]
    r = lax.broadcasted_iota(jnp.int32, (tm, tm), 0)
    c = lax.broadcasted_iota(jnp.int32, (tm, tm), 1)
    same = (r // chunk) == (c // chunk)
    t_fwd = jnp.where(same & (r >= c), 1.0, 0.0).astype(F32)
    t_bwd = jnp.where(same & (r <= c), 1.0, 0.0).astype(F32)
    g_fwd = jnp.dot(t_fwd, g, preferred_element_type=F32, precision=HIGHEST)
    g_bwd = jnp.dot(t_bwd, g, preferred_element_type=F32, precision=HIGHEST)
    lane = lax.broadcasted_iota(jnp.int32, x.shape, 1)
    o_ref[...] = jnp.where(lane < n_heads, g_fwd, jnp.where(lane < 2 * n_heads, g_bwd, beta))


def _gdn_gates(ab, a_log, dt_bias):
    n = ab.shape[0]
    nh = a_log.shape[-1]
    pad = lambda v: jnp.zeros((1, LANES), F32).at[0, :2 * nh].set(v.reshape(-1))
    return pl.pallas_call(
        functools.partial(_gdn_gates_kernel, chunk=GDN_CHUNK, n_heads=nh),
        grid=(n // TM,),
        in_specs=[pl.BlockSpec((TM, LANES), lambda m: (m, 0)),
                  pl.BlockSpec((1, LANES), lambda m: (0, 0)),
                  pl.BlockSpec((1, LANES), lambda m: (0, 0))],
        out_specs=pl.BlockSpec((TM, LANES), lambda m: (m, 0)),
        out_shape=jax.ShapeDtypeStruct((n, LANES), F32),
        compiler_params=_cparams(("parallel",)),
        name="gdn_gates",
    )(ab, pad(a_log), pad(dt_bias))


def _gdn_feat_kernel(x_ref, prev_ref, next_ref, w_ref, o_ref, xs_ref, *, nct, n_tiles, n_norm, n_q, q_scale, width):
    m = pl.program_id(0)
    j = pl.program_id(1)
    tm, tn = x_ref.shape
    pad = width // 2
    prev_ok = jnp.logical_and(m != 0, m != nct)
    next_ok = jnp.logical_and(m != nct - 1, m != n_tiles - 1)
    xs_ref[0:HALO, :] = jnp.where(prev_ok, prev_ref[...].astype(F32), 0.0)
    xs_ref[HALO:HALO + tm, :] = x_ref[...].astype(F32)
    xs_ref[HALO + tm:, :] = jnp.where(next_ok, next_ref[...].astype(F32), 0.0)
    acc = jnp.zeros((tm, tn), F32)
    for t in range(width):
        acc = acc + w_ref[t:t + 1, :] * xs_ref[HALO - pad + t:HALO - pad + t + tm, :]
    y = _silu(acc)

    @pl.when(j >= n_norm)
    def _():
        o_ref[...] = y.astype(o_ref.dtype)

    @pl.when(j < n_norm)
    def _():
        scale = jnp.where(j < n_q, q_scale, 1.0).astype(F32)
        for hh in range(tn // GDN_HEAD_DIM):
            sl = slice(hh * GDN_HEAD_DIM, (hh + 1) * GDN_HEAD_DIM)
            yh = y[:, sl]
            inv = lax.rsqrt(jnp.sum(yh * yh, axis=-1, keepdims=True) + L2_EPS)
            o_ref[:, sl] = (yh * (inv * scale)).astype(o_ref.dtype)


def _gdn_features(p, conv_w, nct, qk_dim):
    n = p.shape[0]
    width, conv_dim = conv_w.shape
    tn = 512
    n_tiles = n // TM
    rb = TM // HALO
    last_rb = n // HALO - 1
    return pl.pallas_call(
        functools.partial(_gdn_feat_kernel, nct=nct, n_tiles=n_tiles, n_norm=2 * qk_dim // tn, n_q=qk_dim // tn,
                          q_scale=GDN_HEAD_DIM ** -0.5, width=width),
        grid=(n_tiles, conv_dim // tn),
        in_specs=[pl.BlockSpec((TM, tn), lambda m, j: (m, j)),
                  pl.BlockSpec((HALO, tn), lambda m, j: (jnp.maximum(m * rb - 1, 0), j)),
                  pl.BlockSpec((HALO, tn), lambda m, j: (jnp.minimum((m + 1) * rb, last_rb), j)),
                  pl.BlockSpec((width, tn), lambda m, j: (0, j))],
        out_specs=pl.BlockSpec((TM, tn), lambda m, j: (m, j)),
        out_shape=jax.ShapeDtypeStruct((n, conv_dim), BF16),
        scratch_shapes=[pltpu.VMEM((TM + 2 * HALO, tn), F32)],
        compiler_params=_cparams(("parallel", "parallel")),
        name="gdn_features",
    )(p, p, p, conv_w)


def _gdn_scan_kernel(*refs, chunk):
    ins, o_refs, s_ref = refs[:16], refs[16:18], refs[18]
    t = pl.program_id(1)

    @pl.when(t == 0)
    def _():
        s_ref[...] = jnp.zeros_like(s_ref)

    rows = ins[0].shape[0]
    hd = GDN_HEAD_DIM
    n_chunk = rows // chunk
    n_double = int(math.log2(chunk)) - 1
    ri = lax.broadcasted_iota(jnp.int32, (rows, rows), 0)
    ci = lax.broadcasted_iota(jnp.int32, (rows, rows), 1)
    same = (ri // chunk) == (ci // chunk)
    eye = jnp.where(ri == ci, 1.0, 0.0).astype(F32)
    r64 = lax.broadcasted_iota(jnp.int32, (chunk, chunk), 0)
    c64 = lax.broadcasted_iota(jnp.int32, (chunk, chunk), 1)

    dirs = []
    for d in range(2):
        q_ref, k_ref, kt_ref, v_ref, scol_ref, srow_ref, srowc_ref, glrow_ref = ins[8 * d:8 * d + 8]
        k = k_ref[...]
        dirs.append(dict(
            q=q_ref[...], k=k, v=v_ref[...], kt=kt_ref[...], srowc_ref=srowc_ref, glrow_ref=glrow_ref,
            scol=scol_ref[0, 0],
            srow=srow_ref[0, 0],
            incl=same & ((ri <= ci) if d else (ri >= ci)), strict=same & ((ri < ci) if d else (ri > ci)),
            incl64=(r64 <= c64) if d else (r64 >= c64),
            kk=_dot_nt(k, k),
            order=list(range(n_chunk - 1, -1, -1)) if d else list(range(n_chunk))))
    chains = [(d, a) for d in range(2) for a in range(2)]
    gcol = {c: dirs[c[0]]["scol"][:, c[1]:c[1] + 1] for c in chains}
    bcol = {c: dirs[c[0]]["scol"][:, 2 + c[1]:3 + c[1]] for c in chains}
    e_g = {c: jnp.exp(gcol[c]) for c in chains}
    e_k = {c: jnp.exp(dirs[c[0]]["scol"][:, 4 + c[1]:5 + c[1]] - gcol[c]) for c in chains}
    neg_l = {c: jnp.where(dirs[c[0]]["strict"],
                          -(bcol[c] * dirs[c[0]]["kk"]
                            * jnp.exp(jnp.where(dirs[c[0]]["incl"], gcol[c] - dirs[c[0]]["srow"][c[1]:c[1] + 1, :],
                                                -jnp.inf))), 0.0) for c in chains}
    inv = {c: eye + neg_l[c] for c in chains}
    pw = {c: neg_l[c].astype(BF16) for c in chains}
    for _ in range(n_double):
        pw = {c: _dot(pw[c], pw[c]).astype(BF16) for c in chains}
        inv = {c: inv[c] + _dot(inv[c].astype(BF16), pw[c]) for c in chains}
    rhs = {c: jnp.concatenate([dirs[c[0]]["v"][:, c[1] * hd:(c[1] + 1) * hd].astype(F32) * bcol[c],
                               dirs[c[0]]["k"].astype(F32) * (bcol[c] * e_g[c])], axis=1).astype(BF16) for c in chains}
    uw = {c: _dot(inv[c].astype(BF16), rhs[c]) for c in chains}

    state = [s_ref[d] for d in range(2)]
    for i in range(n_chunk):
        js = [dirs[d]["order"][i] for d in range(2)]
        rs = [slice(j * chunk, (j + 1) * chunk) for j in js]
        s_b = [st.astype(BF16) for st in state]
        qk_c = [_dot_nt(dirs[d]["q"][rs[d]], dirs[d]["k"][rs[d]]) for d in range(2)]
        q_s = [_dot(dirs[d]["q"][rs[d]], s_b[d]) for d in range(2)]
        v_new = {(d, a): uw[d, a][rs[d], :hd] - _dot(uw[d, a][rs[d], hd:].astype(BF16), s_b[d][:, a * hd:(a + 1) * hd])
                 for d, a in chains}
        qkd = {(d, a): (qk_c[d] * jnp.exp(jnp.where(dirs[d]["incl64"],
                                                    gcol[d, a][rs[d]] - dirs[d]["srowc_ref"][0, 0, js[d], a:a + 1, :],
                                                    -jnp.inf))).astype(BF16) for d, a in chains}
        for d, a in chains:
            o = e_g[d, a][rs[d]] * q_s[d][:, a * hd:(a + 1) * hd] + _dot(qkd[d, a], v_new[d, a].astype(BF16))
            o_refs[d][rs[d], a * hd:(a + 1) * hd] = o.astype(o_refs[d].dtype)
        for d in range(2):
            v_s = jnp.concatenate([v_new[d, a] * e_k[d, a][rs[d]] for a in range(2)], axis=1).astype(BF16)
            parts = [jnp.zeros((js[d] * chunk, 2 * hd), BF16), v_s,
                     jnp.zeros(((n_chunk - 1 - js[d]) * chunk, 2 * hd), BF16)]
            v_pad = jnp.concatenate([p for p in parts if p.shape[0]], axis=0)
            e_l = jnp.exp(dirs[d]["glrow_ref"][0, 0, js[d]])
            state[d] = state[d] * e_l + _dot(dirs[d]["kt"], v_pad)
    for d in range(2):
        s_ref[d] = state[d]


def _transpose_kernel(x_ref, o_ref):
    o_ref[...] = x_ref[...].astype(F32).T.astype(o_ref.dtype)


def _transposed_keys(feat, qk_dim):
    n = feat.shape[0]
    tn = 512
    return pl.pallas_call(
        _transpose_kernel,
        grid=(n // TM, qk_dim // tn),
        in_specs=[pl.BlockSpec((TM, tn), lambda m, j: (m, qk_dim // tn + j))],
        out_specs=pl.BlockSpec((tn, TM), lambda m, j: (j, m)),
        out_shape=jax.ShapeDtypeStruct((qk_dim, n), feat.dtype),
        compiler_params=_cparams(("parallel", "parallel")),
        name="gdn_keys_t",
    )(feat)


def _gdn_scan(feat, kt, scol, srow, srowc, glrow, nct):
    n = feat.shape[0]
    hd = GDN_HEAD_DIM
    n_qk = kt.shape[0] // hd
    n_tiles = n // TM
    cpt = TM // GDN_CHUNK

    def tile(d, t):
        return jnp.where(t < nct, nct - 1 - t, n_tiles - 1 - (t - nct)) if d else t

    def specs(d):
        return [pl.BlockSpec((TM, hd), lambda h, t: (tile(d, t), h)),
                pl.BlockSpec((TM, hd), lambda h, t: (tile(d, t), n_qk + h)),
                pl.BlockSpec((hd, TM), lambda h, t: (h, tile(d, t))),
                pl.BlockSpec((TM, 2 * hd), lambda h, t: (tile(d, t), n_qk + h)),
                pl.BlockSpec((1, 1, TM, 6), lambda h, t: (d, h, tile(d, t), 0)),
                pl.BlockSpec((1, 1, 4, TM), lambda h, t: (d, h, 0, tile(d, t))),
                pl.BlockSpec((1, 1, cpt, 4, GDN_CHUNK), lambda h, t: (d, h, tile(d, t), 0, 0)),
                pl.BlockSpec((1, 1, cpt, 1, 2 * hd), lambda h, t: (d, h, tile(d, t), 0, 0))]

    args = [feat, feat, kt, feat, scol, srow, srowc, glrow]
    out = jax.ShapeDtypeStruct((n, 2 * n_qk * hd), BF16)
    return pl.pallas_call(
        functools.partial(_gdn_scan_kernel, chunk=GDN_CHUNK),
        grid=(n_qk, n_tiles),
        in_specs=specs(0) + specs(1),
        out_specs=[pl.BlockSpec((TM, 2 * hd), lambda h, t: (tile(0, t), h)),
                   pl.BlockSpec((TM, 2 * hd), lambda h, t: (tile(1, t), h))],
        out_shape=[out, out],
        scratch_shapes=[pltpu.VMEM((2, hd, 2 * hd), F32)],
        compiler_params=_cparams(("parallel", "arbitrary")),
        name="gdn_scan",
    )(*args, *args)


def _post_mixer(y, h_ref, mod_ref, lnw_ref, lnb_ref, rw_ref, rb_ref, h1_ref, v_ref, rt_ref, cnt_ref, carry_ref,
                alpha, n_e):
    @pl.when(pl.program_id(0) == 0)
    def _():
        carry_ref[...] = jnp.zeros_like(carry_ref)

    x = alpha * h_ref[...] + mod_ref[0, 2:3, :] * y
    h1 = _layer_norm(x, lnw_ref[...], lnb_ref[...])
    h1_ref[...] = h1
    v = _modulate(h1, mod_ref, 3, 4)
    v_ref[...] = _pack_bf16_pairs(v)
    logits = jnp.dot(v, rw_ref[...], preferred_element_type=F32, precision=HIGHEST) + rb_ref[...]

    tm = logits.shape[0]
    lane = lax.broadcasted_iota(jnp.int32, logits.shape, 1)
    rest = jnp.where(lane < n_e, logits, -jnp.inf)
    top_val, top_idx, picked = [], [], []
    for _ in range(TOP_K):
        mx = jnp.max(rest, axis=-1, keepdims=True)
        idx = jnp.min(jnp.where(rest == mx, lane, LANES), axis=-1, keepdims=True)
        hit = lane == idx
        top_val.append(mx)
        top_idx.append(idx)
        picked.append(hit)
        rest = jnp.where(hit, -jnp.inf, rest)
    e = [jnp.exp(tv - top_val[0]) for tv in top_val]
    denom = functools.reduce(lambda a, b: a + b, e)
    onehot = functools.reduce(lambda a, b: a + b, [jnp.where(hit, 1.0, 0.0) for hit in picked])
    r = lax.broadcasted_iota(jnp.int32, (tm, tm), 0)
    c = lax.broadcasted_iota(jnp.int32, (tm, tm), 1)
    earlier = jnp.where(r > c, 1.0, 0.0).astype(BF16)
    before = _dot(earlier, onehot.astype(BF16)) + carry_ref[0:1, :]
    rt = jnp.zeros(logits.shape, F32)
    for k in range(TOP_K):
        rank = jnp.sum(jnp.where(picked[k], before, 0.0), axis=-1, keepdims=True)
        rt = jnp.where(lane == k, top_idx[k].astype(F32), rt)
        rt = jnp.where(lane == TOP_K + k, rank, rt)
        rt = jnp.where(lane == 2 * TOP_K + k, e[k] / denom, rt)
    rt_ref[...] = rt
    total = carry_ref[0:1, :] + jnp.sum(onehot, axis=0, keepdims=True)
    carry_ref[...] = jnp.broadcast_to(total, carry_ref.shape)
    cnt_ref[...] = jnp.broadcast_to(total, cnt_ref.shape)


def _gdn_out_kernel(of_ref, ob_ref, z_ref, nw_ref, w_ref, h_ref, mod_ref, lnw_ref, lnb_ref, rw_ref, rb_ref,
                    h1_ref, v_ref, rt_ref, cnt_ref, carry_ref, a_ref, *, alpha, n_e):
    hd = GDN_HEAD_DIM
    for hh in range(of_ref.shape[1] // hd):
        sl = slice(hh * hd, (hh + 1) * hd)
        o = of_ref[:, sl].astype(F32) + ob_ref[:, sl].astype(F32)
        o = o * lax.rsqrt(jnp.mean(o * o, axis=-1, keepdims=True) + NORM_EPS) * nw_ref[...]
        a_ref[:, sl] = (o * _silu(z_ref[:, sl].astype(F32))).astype(BF16)
    y = _dot(a_ref[...], w_ref[...])
    _post_mixer(y, h_ref, mod_ref, lnw_ref, lnb_ref, rw_ref, rb_ref, h1_ref, v_ref, rt_ref, cnt_ref, carry_ref,
                alpha, n_e)


def _attn_out_kernel(a_ref, w_ref, h_ref, mod_ref, lnw_ref, lnb_ref, rw_ref, rb_ref, h1_ref, v_ref, rt_ref, cnt_ref,
                     carry_ref, *, alpha, n_e):
    y = _dot(a_ref[...], w_ref[...])
    _post_mixer(y, h_ref, mod_ref, lnw_ref, lnb_ref, rw_ref, rb_ref, h1_ref, v_ref, rt_ref, cnt_ref, carry_ref,
                alpha, n_e)


def _mixer_out(kind, acts, w_out, h, mod, nct, ln_w, ln_b, router_w, router_b, alpha):
    n, d = h.shape
    row = lambda width: pl.BlockSpec((TM, width), lambda m: (m, 0))
    full = lambda arr: pl.BlockSpec(arr.shape, lambda m: (0,) * arr.ndim)
    n_e = router_w.shape[1]
    rw = jnp.zeros((d, LANES), F32).at[:, :n_e].set(router_w)
    rb = jnp.zeros((1, LANES), F32).at[0, :n_e].set(router_b)
    tail = [w_out, h, mod, ln_w.reshape(1, d), ln_b.reshape(1, d), rw, rb]
    tail_specs = [full(w_out), row(d), pl.BlockSpec((1, 6, d), lambda m: (jnp.where(m < nct, 0, 1), 0, 0)),
                  pl.BlockSpec((1, d), lambda m: (0, 0)), pl.BlockSpec((1, d), lambda m: (0, 0)), full(rw), full(rb)]
    if kind == "gdn":
        o_f, o_b, p, norm_w, z_col = acts
        v_dim = o_f.shape[1]
        body = functools.partial(_gdn_out_kernel, alpha=alpha, n_e=n_e)
        args = [o_f, o_b, p, norm_w.reshape(1, -1)] + tail
        specs = [row(v_dim), row(v_dim), pl.BlockSpec((TM, v_dim), lambda m: (m, z_col)),
                 pl.BlockSpec((1, norm_w.shape[0]), lambda m: (0, 0))] + tail_specs
        scratch = [pltpu.VMEM((SUBLANES, LANES), F32), pltpu.VMEM((TM, v_dim), BF16)]
    else:
        (a,) = acts
        body = functools.partial(_attn_out_kernel, alpha=alpha, n_e=n_e)
        args = [a] + tail
        specs = [row(a.shape[1])] + tail_specs
        scratch = [pltpu.VMEM((SUBLANES, LANES), F32)]
    return pl.pallas_call(
        body,
        grid=(n // TM,),
        in_specs=specs,
        out_specs=[row(d), row(d // 2), row(LANES), pl.BlockSpec((SUBLANES, LANES), lambda m: (0, 0))],
        out_shape=[jax.ShapeDtypeStruct((n, d), F32), jax.ShapeDtypeStruct((n, d // 2), jnp.uint32),
                   jax.ShapeDtypeStruct((n, LANES), F32), jax.ShapeDtypeStruct((SUBLANES, LANES), F32)],
        scratch_shapes=scratch,
        compiler_params=_cparams(("arbitrary",)),
        name=kind + "_out_ln",
    )(*args)


def _attn_kernel(q_ref, k_ref, vt_ref, lam_ref, sw_ref, o_ref, acc_ref, m_ref, l_ref, sa_ref, sb_ref,
                 *, tk, nct, l_ctx, n_pairs, lambda_init):
    mt = pl.program_id(1)
    q = q_ref[...]
    lane = lax.broadcasted_iota(jnp.int32, q.shape, 1)
    zero = jnp.zeros_like(q)
    q_maps = (jnp.where(lane < DIFF_HEAD_DIM, q, zero), jnp.where(lane >= DIFF_HEAD_DIM, q, zero))
    kb = vt_ref.shape[3]
    acc_ref[...] = jnp.zeros_like(acc_ref)
    m_ref[...] = jnp.full_like(m_ref, -jnp.inf)
    l_ref[...] = jnp.zeros_like(l_ref)

    def scores(start, s_ref):
        kj = k_ref[pl.ds(start, tk), :]
        for i, qm in enumerate(q_maps):
            s_ref[i] = _dot_nt(kj, qm)

    def absorb(score_of_map, block0, n_blocks):
        for i in range(2):
            s = score_of_map(i)
            m_old = m_ref[i]
            m_new = jnp.maximum(m_old, jnp.max(s, axis=0, keepdims=True))
            alpha = jnp.exp2(m_old - m_new)
            p = jnp.exp2(s - m_new)
            l_ref[i] = alpha * l_ref[i] + jnp.sum(p, axis=0, keepdims=True)
            p = p.astype(BF16)
            pv = _dot(vt_ref[0, block0], p[0:kb])
            for b in range(1, n_blocks):
                pv = pv + _dot(vt_ref[0, block0 + b], p[b * kb:(b + 1) * kb])
            acc_ref[i] = alpha * acc_ref[i] + pv
            m_ref[i] = m_new

    k_ctx = k_ref[0:l_ctx, :]
    absorb(lambda i: _dot_nt(k_ctx, q_maps[i]), 0, l_ctx // kb)

    @pl.when(mt >= nct)
    def _():
        last = l_ctx + (2 * n_pairs - 1) * tk
        scores(l_ctx, sa_ref)

        def body(jj, carry):
            c0 = pl.multiple_of(l_ctx + 2 * jj * tk, LANES)
            c1 = pl.multiple_of(c0 + tk, LANES)
            c2 = pl.multiple_of(jnp.minimum(c1 + tk, last), LANES)
            scores(c1, sb_ref)
            absorb(lambda i: sa_ref[i], c0 // kb, tk // kb)
            scores(c2, sa_ref)
            absorb(lambda i: sb_ref[i], c1 // kb, tk // kb)
            return carry

        lax.fori_loop(0, n_pairs, body, 0)

    lam = lam_ref[...]
    lam_full = (jnp.exp(jnp.sum(lam[0:1] * lam[1:2], axis=-1, keepdims=True))
                - jnp.exp(jnp.sum(lam[2:3] * lam[3:4], axis=-1, keepdims=True)) + lambda_init)
    o = acc_ref[0] / l_ref[0] - lam_full * (acc_ref[1] / l_ref[1])
    o = o * lax.rsqrt(jnp.mean(o * o, axis=0, keepdims=True) + NORM_EPS) * sw_ref[...]
    o_ref[...] = (o * (1.0 - lambda_init)).T.astype(o_ref.dtype)


def _transpose_heads_kernel(x_ref, o_ref):
    hd = o_ref.shape[2]
    for hh in range(o_ref.shape[0]):
        o_ref[hh, 0] = x_ref[:, hh * hd:(hh + 1) * hd].astype(F32).T.astype(o_ref.dtype)


def _transposed_values(qkv, vd):
    n = qkv.shape[0]
    d = qkv.shape[1] // 3
    tn = 512
    return pl.pallas_call(
        _transpose_heads_kernel,
        grid=(n // TM, d // tn),
        in_specs=[pl.BlockSpec((TM, tn), lambda m, j: (m, 2 * d // tn + j))],
        out_specs=pl.BlockSpec((tn // vd, 1, vd, TM), lambda m, j: (j, m, 0, 0)),
        out_shape=jax.ShapeDtypeStruct((d // vd, n // TM, vd, TM), qkv.dtype),
        compiler_params=_cparams(("parallel", "parallel")),
        name="attn_values_t",
    )(qkv)


def _diff_attention(qkv, lam, subln_w, nct, l_ctx, lambda_init):
    n = qkv.shape[0]
    d = qkv.shape[1] // 3
    vd = subln_w.shape[0]
    n_heads = d // vd
    l_lat = n - l_ctx
    tk = math.gcd(l_lat // 2, ATTN_TK)
    assert tk % TM == 0
    vt = _transposed_values(qkv, vd)
    return pl.pallas_call(
        functools.partial(_attn_kernel, tk=tk, nct=nct, l_ctx=l_ctx, n_pairs=l_lat // (2 * tk),
                          lambda_init=lambda_init),
        grid=(n_heads, n // TM),
        in_specs=[pl.BlockSpec((TM, vd), lambda h, m: (m, h)),
                  pl.BlockSpec((n, vd), lambda h, m: (0, n_heads + h)),
                  pl.BlockSpec((1, n // TM, vd, TM), lambda h, m: (h, 0, 0, 0)),
                  pl.BlockSpec(lam.shape, lambda h, m: (0, 0)),
                  pl.BlockSpec((vd, 1), lambda h, m: (0, 0))],
        out_specs=pl.BlockSpec((TM, vd), lambda h, m: (m, h)),
        out_shape=jax.ShapeDtypeStruct((n, d), BF16),
        scratch_shapes=[pltpu.VMEM((2, vd, TM), F32), pltpu.VMEM((2, 1, TM), F32), pltpu.VMEM((2, 1, TM), F32),
                        pltpu.VMEM((2, tk, TM), F32), pltpu.VMEM((2, tk, TM), F32)],
        compiler_params=_cparams(("parallel", "parallel")),
        name="diff_attn",
    )(qkv, qkv, vt, lam, subln_w.reshape(vd, 1))


def _moe_kernel(be_ref, nb_ref, x_ref, wgu_ref, bgu_ref, wdn_ref, bdn_ref, y_ref, wgu_s, wdn_s, *, d_ff):
    b = pl.program_id(0)
    changed = jnp.logical_or(b == 0, be_ref[b] != be_ref[jnp.maximum(b - 1, 0)])

    @pl.when(changed)
    def _():
        wgu_s[...] = wgu_ref[0, 0].astype(BF16)
        wdn_s[...] = wdn_ref[0, 0].astype(BF16)

    @pl.when(b < nb_ref[0])
    def _():
        h = _dot(_unpack_bf16_pairs(x_ref[...]), wgu_s[...]) + bgu_ref[0, 0]
        gate = jnp.minimum(h[:, :d_ff], SWIGLU_LIMIT)
        up = jnp.clip(h[:, d_ff:], -SWIGLU_LIMIT, SWIGLU_LIMIT)
        glu = gate * jax.nn.sigmoid(gate * SWIGLU_ALPHA)
        act = ((up + 1.0) * glu).astype(BF16)
        y_ref[...] = (_dot(act, wdn_s[...]) + bdn_ref[0, 0]).astype(y_ref.dtype)

    @pl.when(b >= nb_ref[0])
    def _():
        y_ref[...] = jnp.zeros_like(y_ref)


def _moe_experts(x_sorted, block_e, n_used, layer, w_gu, b_gu, w_dn, b_dn):
    n_rows = x_sorted.shape[0]
    depth, n_e, d, f2 = w_gu.shape
    grid_spec = pltpu.PrefetchScalarGridSpec(
        num_scalar_prefetch=2,
        grid=(n_rows // MOE_BM,),
        in_specs=[pl.BlockSpec((MOE_BM, d // 2), lambda b, be, nb: (b, 0)),
                  pl.BlockSpec((1, 1, d, f2), lambda b, be, nb: (layer, be[b], 0, 0)),
                  pl.BlockSpec((1, 1, 1, f2), lambda b, be, nb: (layer, be[b], 0, 0)),
                  pl.BlockSpec((1, 1, f2 // 2, d), lambda b, be, nb: (layer, be[b], 0, 0)),
                  pl.BlockSpec((1, 1, 1, d), lambda b, be, nb: (layer, be[b], 0, 0))],
        out_specs=pl.BlockSpec((MOE_BM, d), lambda b, be, nb: (b, 0)),
        scratch_shapes=[pltpu.VMEM((d, f2), BF16), pltpu.VMEM((f2 // 2, d), BF16)],
    )
    return pl.pallas_call(
        functools.partial(_moe_kernel, d_ff=f2 // 2),
        grid_spec=grid_spec,
        out_shape=jax.ShapeDtypeStruct((n_rows, d), BF16),
        compiler_params=_cparams(("arbitrary",)),
        name="moe_experts",
    )(block_e, n_used, x_sorted, w_gu, b_gu.reshape(depth, n_e, 1, f2), w_dn, b_dn.reshape(depth, n_e, 1, d))


def _moe_combine_kernel(y0_ref, y1_ref, y2_ref, y3_ref, g_ref, h_ref, mod_ref, lnw_ref, lnb_ref, o_ref, *, alpha):
    g = g_ref[...]
    f = jnp.zeros(h_ref.shape, F32)
    for i, y_ref in enumerate((y0_ref, y1_ref, y2_ref, y3_ref)):
        f = f + g[:, i:i + 1] * y_ref[...].astype(F32)
    x = alpha * h_ref[...] + mod_ref[0, 5:6, :] * f
    o_ref[...] = _layer_norm(x, lnw_ref[...], lnb_ref[...])


def _moe_combine(ys, gates, h1, mod, nct, ln_w, ln_b, alpha):
    n, d = h1.shape
    row = lambda width: pl.BlockSpec((TM, width), lambda m: (m, 0))
    return pl.pallas_call(
        functools.partial(_moe_combine_kernel, alpha=alpha),
        grid=(n // TM,),
        in_specs=[row(d)] * TOP_K + [row(TOP_K), row(d),
                                     pl.BlockSpec((1, 6, d), lambda m: (jnp.where(m < nct, 0, 1), 0, 0)),
                                     pl.BlockSpec((1, d), lambda m: (0, 0)), pl.BlockSpec((1, d), lambda m: (0, 0))],
        out_specs=row(d),
        out_shape=jax.ShapeDtypeStruct((n, d), F32),
        compiler_params=_cparams(("parallel",)),
        name="moe_combine_ln",
    )(*ys, gates, h1, mod, ln_w.reshape(1, d), ln_b.reshape(1, d))


def _sc_gather_rows(table, idx):
    n_rows = idx.shape[0]
    d = table.shape[1]
    n_workers = SC_CORES * SC_SUBCORES
    per_worker = n_rows // n_workers
    assert per_worker * n_workers == n_rows and per_worker % SUBLANES == 0
    step = math.gcd(per_worker, SC_GATHER_ROWS)
    mesh = plsc.VectorSubcoreMesh(core_axis_name="c", subcore_axis_name="s",
                                  num_cores=SC_CORES, num_subcores=SC_SUBCORES)

    def body(table_hbm, idx_hbm, out_hbm, idx_v, rows_v, sem):
        base = (lax.axis_index("s") * SC_CORES + lax.axis_index("c")) * per_worker

        @pl.loop(0, per_worker // step)
        def _(i):
            off = pl.multiple_of(base + i * step, SUBLANES)
            pltpu.sync_copy(idx_hbm.at[pl.ds(off, step)], idx_v)
            pltpu.async_copy(table_hbm.at[idx_v], rows_v, sem).wait()
            pltpu.sync_copy(rows_v, out_hbm.at[pl.ds(off, step)])

    return pl.kernel(
        body,
        out_type=jax.ShapeDtypeStruct((n_rows, d), table.dtype),
        mesh=mesh,
        scratch_types=[pltpu.VMEM((step,), jnp.int32), pltpu.VMEM((step, d), table.dtype),
                       pltpu.SemaphoreType.DMA],
        name="sc_gather_rows",
    )(table, idx)


def _moe(v_rows, route, counts, n_e, h1, mod, nct, ln_w, ln_b, alpha, layer, w_gu, b_gu, w_dn, b_dn):
    n = v_rows.shape[0]
    top_idx = route[:, :TOP_K].astype(jnp.int32)
    rank = route[:, TOP_K:2 * TOP_K].astype(jnp.int32)
    gates = route[:, 2 * TOP_K:3 * TOP_K]
    counts = counts[0, :n_e].astype(jnp.int32)
    nk = n * TOP_K
    padded = (counts + MOE_BM - 1) // MOE_BM * MOE_BM
    pad_end = jnp.cumsum(padded)
    pad_start = pad_end - padded
    dest = (pad_start[top_idx] + rank).reshape(nk)
    n_blocks = -(-nk // MOE_BM) + n_e
    n_rows = n_blocks * MOE_BM
    row_tok = jnp.zeros((n_rows,), jnp.int32).at[dest].set(jnp.arange(nk, dtype=jnp.int32) // TOP_K)
    block_row0 = jnp.arange(n_blocks, dtype=jnp.int32) * MOE_BM
    block_e = jnp.minimum(jnp.sum(pad_end[None, :] <= block_row0[:, None], axis=1, dtype=jnp.int32), n_e - 1)
    n_used = (pad_end[-1:] // MOE_BM).astype(jnp.int32)
    x_sorted = _sc_gather_rows(v_rows, row_tok)
    yb = _moe_experts(x_sorted, block_e, n_used, layer, w_gu, b_gu, w_dn, b_dn)
    dest2 = dest.reshape(n, TOP_K)
    ys = [yb[dest2[:, i]] for i in range(TOP_K)]
    return _moe_combine(ys, gates, h1, mod, nct, ln_w, ln_b, alpha)


def _rope_tables(l_ctx, l_lat):
    rows = l_lat // GRID_W
    row = jnp.repeat(jnp.arange(rows, dtype=F32), GRID_W)
    col = jnp.tile(jnp.arange(GRID_W, dtype=F32), rows)
    axis_dim = DIFF_HEAD_DIM // 2
    inv_freq = ROPE_THETA ** (-jnp.arange(0, axis_dim, 2, dtype=F32) / axis_dim)
    ang_r = row[:, None] * inv_freq
    ang_c = col[:, None] * inv_freq
    ang = jnp.concatenate([ang_r, ang_r, ang_c, ang_c], -1)
    ang = jnp.concatenate([jnp.zeros((l_ctx, DIFF_HEAD_DIM), F32), ang], 0)
    ang = jnp.concatenate([ang, ang], -1)
    return jnp.cos(ang), jnp.sin(ang)


def _gdn_scalars(gates, n_vh):
    n = gates.shape[0]
    n_qk = n_vh // 2
    nc = n // GDN_CHUNK
    gc = gates[:, :2 * n_vh].reshape(n, 2, n_qk, 2)
    beta = gates[:, 2 * n_vh:4 * n_vh].reshape(n, 2, n_qk, 2)
    gcc = gc.reshape(nc, GDN_CHUNK, 2, n_qk, 2)
    gl = jnp.stack([gcc[:, -1, 0], gcc[:, 0, 1]], axis=1)
    gl_tok = jnp.repeat(gl, GDN_CHUNK, axis=0)
    col = jnp.concatenate([gc, beta, gl_tok], axis=-1)
    scol = jnp.transpose(col, (1, 2, 0, 3))
    srow = jnp.transpose(col[..., :4], (1, 2, 3, 0))
    srowc = jnp.transpose(col[..., :4].reshape(nc, GDN_CHUNK, 2, n_qk, 4), (2, 3, 0, 4, 1))
    glrow = jnp.broadcast_to(jnp.transpose(gl, (1, 2, 0, 3))[..., None], (2, n_qk, nc, 2, GDN_HEAD_DIM))
    glrow = glrow.reshape(2, n_qk, nc, 1, 2 * GDN_HEAD_DIM)
    return scol, srow, srowc, glrow


def kernel(x, c, ctx, c_ctx, ada_w, ada_b, ln_w, ln_b, gdn_w_in, gdn_conv_w, gdn_a_log, gdn_dt_bias, gdn_norm_w,
           gdn_w_out, diff_w_in, diff_lambda, diff_subln_w, diff_w_out, router_w, router_b, moe_w_gate_up,
           moe_b_gate_up, moe_w_down, moe_b_down):
    batch, l_lat, d = x.shape
    l_ctx = ctx.shape[1]
    depth = ada_w.shape[0]
    assert batch == 1 and l_ctx % TM == 0 and l_lat % TM == 0 and l_lat % GRID_W == 0
    nct = l_ctx // TM
    n_e = router_w.shape[-1]
    alpha = (2 * depth) ** 0.25
    n_vh = gdn_a_log.shape[-1]
    v_dim = n_vh * GDN_HEAD_DIM
    conv_dim = gdn_conv_w.shape[-1]
    qk_dim = (conv_dim - v_dim) // 2
    n_qk = qk_dim // GDN_HEAD_DIM

    h = jnp.concatenate([ctx[0], x[0]], axis=0)
    n = h.shape[0]
    mods = _modulations(c_ctx, c, ada_w, ada_b)
    cos, sin = _rope_tables(l_ctx, l_lat)

    for i in range(depth):
        j = i // 2
        mod = mods[i]
        if i % 2 == 0:
            w_in = gdn_w_in[j]
            w_main = w_in[:, :conv_dim + v_dim].astype(BF16)
            w_ab = jnp.zeros((d, LANES), F32).at[:, :4 * n_vh].set(w_in[:, conv_dim + v_dim:]).astype(BF16)
            p, ab = _project(h, mod, nct, [w_main, w_ab], [BF16, F32])
            gates = _gdn_gates(ab, gdn_a_log[j], gdn_dt_bias[j])
            feat = _gdn_features(p, gdn_conv_w[j], nct, qk_dim)
            kt = _transposed_keys(feat, qk_dim)
            scol, srow, srowc, glrow = _gdn_scalars(gates, n_vh)
            o_f, o_b = _gdn_scan(feat, kt, scol, srow, srowc, glrow, nct)
            acts = (o_f, o_b, p, gdn_norm_w[j], conv_dim // v_dim)
            h1, v, route, counts = _mixer_out("gdn", acts, gdn_w_out[j].astype(BF16), h, mod, nct, ln_w[i, 0], ln_b[i, 0],
                                       router_w[i], router_b[i], alpha)
        else:
            lambda_init = 0.8 - 0.6 * math.exp(-0.3 * i)
            qkv = _qkv_rope(h, mod, nct, diff_w_in[j].astype(BF16), cos, sin)
            a = _diff_attention(qkv, diff_lambda[j], diff_subln_w[j], nct, l_ctx, lambda_init)
            h1, v, route, counts = _mixer_out("attn", (a,), diff_w_out[j].astype(BF16), h, mod, nct, ln_w[i, 0], ln_b[i, 0],
                                       router_w[i], router_b[i], alpha)
        h = _moe(v, route, counts, n_e, h1, mod, nct, ln_w[i, 1], ln_b[i, 1], alpha,
                 i, moe_w_gate_up, moe_b_gate_up, moe_w_down, moe_b_down)
    return h[l_ctx:].reshape(batch, l_lat, d)
```

```python
import functools
import math

import jax
import jax.numpy as jnp
from jax import lax
from jax.experimental import pallas as pl
from jax.experimental.pallas import tpu as pltpu
from jax.experimental.pallas import tpu_sc as plsc

F32 = jnp.float32
BF16 = jnp.bfloat16
HIGHEST = lax.Precision.HIGHEST

GRID_W = 64
GDN_HEAD_DIM = 128
GDN_CHUNK = 64
DIFF_HEAD_DIM = 64
ROPE_THETA = 10000.0
TOP_K = 4
SWIGLU_LIMIT = 7.0
SWIGLU_ALPHA = 1.702
NORM_EPS = 1e-5
L2_EPS = 1e-6

LANES = 128
SUBLANES = 8
HALO = 16
VMEM_LIMIT = 56 * 1024 * 1024

TM = 256
MOE_BM = 256
ATTN_TK = 2048

SC_CORES = 2
SC_SUBCORES = 16
SC_GATHER_ROWS = 32


def _cparams(sem):
    return pltpu.CompilerParams(dimension_semantics=sem, vmem_limit_bytes=VMEM_LIMIT)


def _silu(x):
    return x * jax.nn.sigmoid(x)


def _dot(a, b):
    return jnp.dot(a, b, preferred_element_type=F32)


def _dot_nt(a, b):
    return lax.dot_general(a, b, (((1,), (1,)), ((), ())), preferred_element_type=F32)


def _modulate(h, mod_ref, shift, scale):
    return h * (1.0 + mod_ref[0, scale:scale + 1, :]) + mod_ref[0, shift:shift + 1, :]


def _pack_bf16_pairs(x):
    bits = pltpu.bitcast(x.astype(BF16).astype(F32), jnp.uint32)
    half = x.shape[1] // 2
    return (bits[:, half:] & jnp.uint32(0xFFFF0000)) | (bits[:, :half] >> 16)


def _unpack_bf16_pairs(w):
    lo = pltpu.bitcast(w << 16, F32)
    hi = pltpu.bitcast(w & jnp.uint32(0xFFFF0000), F32)
    return jnp.concatenate([lo, hi], axis=1).astype(BF16)


def _layer_norm(x, w, b):
    mu = jnp.mean(x, axis=-1, keepdims=True)
    xc = x - mu
    var = jnp.mean(xc * xc, axis=-1, keepdims=True)
    return xc * lax.rsqrt(var + NORM_EPS) * w + b


def _mod_kernel(s_ref, w_ref, b_ref, o_ref):
    s = _silu(s_ref[...])
    o_ref[0] = jnp.dot(s, w_ref[0], preferred_element_type=F32, precision=HIGHEST) + b_ref[0]


def _modulations(c_ctx, c, ada_w, ada_b):
    depth, d, d6 = ada_w.shape
    tn = d6 // 4
    s = jnp.zeros((SUBLANES, d), F32).at[0].set(c_ctx).at[1].set(c[0])
    out = pl.pallas_call(
        _mod_kernel,
        grid=(depth, d6 // tn),
        in_specs=[pl.BlockSpec((SUBLANES, d), lambda i, n: (0, 0)),
                  pl.BlockSpec((1, d, tn), lambda i, n: (i, 0, n)),
                  pl.BlockSpec((1, 1, tn), lambda i, n: (i, 0, n))],
        out_specs=pl.BlockSpec((1, SUBLANES, tn), lambda i, n: (i, 0, n)),
        out_shape=jax.ShapeDtypeStruct((depth, SUBLANES, d6), F32),
        compiler_params=_cparams(("parallel", "parallel")),
        name="adaln_mod",
    )(s, ada_w, ada_b.reshape(depth, 1, d6))
    return out[:, :2].reshape(depth, 2, 6, d)


def _proj_kernel(h_ref, mod_ref, *refs, n_w, chunk):
    w_refs, o_refs = refs[:n_w], refs[n_w:]
    u = _modulate(h_ref[...], mod_ref, 0, 1).astype(BF16)
    for w_ref, o_ref in zip(w_refs, o_refs):
        n = w_ref.shape[1]
        for j in range(0, n, chunk):
            jc = min(chunk, n - j)
            o_ref[:, j:j + jc] = _dot(u, w_ref[:, j:j + jc]).astype(o_ref.dtype)


def _project(h, mod, nct, weights, out_dtypes):
    n, d = h.shape
    return pl.pallas_call(
        functools.partial(_proj_kernel, n_w=len(weights), chunk=512),
        grid=(n // TM,),
        in_specs=[pl.BlockSpec((TM, d), lambda m: (m, 0)),
                  pl.BlockSpec((1, 6, d), lambda m: (jnp.where(m < nct, 0, 1), 0, 0))]
                 + [pl.BlockSpec(w.shape, lambda m: (0, 0)) for w in weights],
        out_specs=[pl.BlockSpec((TM, w.shape[1]), lambda m: (m, 0)) for w in weights],
        out_shape=[jax.ShapeDtypeStruct((n, w.shape[1]), dt) for w, dt in zip(weights, out_dtypes)],
        compiler_params=_cparams(("parallel",)),
        name="mod_proj",
    )(h, mod, *weights)


def _qkv_rope_kernel(h_ref, mod_ref, w_ref, cos_ref, sin_ref, o_ref, *, chunk, n_rope, n_q, q_scale):
    u = _modulate(h_ref[...], mod_ref, 0, 1).astype(BF16)
    rep = chunk // LANES
    cos = jnp.concatenate([cos_ref[...]] * rep, axis=1)
    sin = jnp.concatenate([sin_ref[...]] * rep, axis=1)
    lane = lax.broadcasted_iota(jnp.int32, (h_ref.shape[0], chunk), 1)
    first = (lane % 32) < 16
    for j in range(w_ref.shape[1] // chunk):
        y = _dot(u, w_ref[:, j * chunk:(j + 1) * chunk])
        if j < n_rope:
            rot = jnp.where(first, -pltpu.roll(y, chunk - 16, 1), pltpu.roll(y, 16, 1))
            y = y * cos + rot * sin
            if j < n_q:
                y = y * q_scale
        o_ref[:, j * chunk:(j + 1) * chunk] = y.astype(o_ref.dtype)


def _qkv_rope(h, mod, nct, w, cos, sin):
    n, d = h.shape
    chunk = 512
    return pl.pallas_call(
        functools.partial(_qkv_rope_kernel, chunk=chunk, n_rope=2 * d // chunk, n_q=d // chunk,
                          q_scale=DIFF_HEAD_DIM ** -0.5 * math.log2(math.e)),
        grid=(n // TM,),
        in_specs=[pl.BlockSpec((TM, d), lambda m: (m, 0)),
                  pl.BlockSpec((1, 6, d), lambda m: (jnp.where(m < nct, 0, 1), 0, 0)),
                  pl.BlockSpec(w.shape, lambda m: (0, 0)),
                  pl.BlockSpec((TM, LANES), lambda m: (m, 0)),
                  pl.BlockSpec((TM, LANES), lambda m: (m, 0))],
        out_specs=pl.BlockSpec((TM, w.shape[1]), lambda m: (m, 0)),
        out_shape=jax.ShapeDtypeStruct((n, w.shape[1]), BF16),
        compiler_params=_cparams(("parallel",)),
        name="qkv_rope",
    )(h, mod, w, cos, sin)


def _gdn_gates_kernel(ab_ref, alog_ref, dtb_ref, o_ref, *, chunk, n_heads):
    x = ab_ref[...]
    t = x + dtb_ref[...]
    softplus = jnp.maximum(t, 0.0) + jnp.log1p(jnp.exp(-jnp.abs(t)))
    g = -jnp.exp(alog_ref[...]) * softplus
    beta = jax.nn.sigmoid(x)
    tm = x.shape[0]
    r = lax.broadcasted_iota(jnp.int32, (tm, tm), 0)
    c = lax.broadcasted_iota(jnp.int32, (tm, tm), 1)
    same = (r // chunk) == (c // chunk)
    t_fwd = jnp.where(same & (r >= c), 1.0, 0.0).astype(F32)
    t_bwd = jnp.where(same & (r <= c), 1.0, 0.0).astype(F32)
    g_fwd = jnp.dot(t_fwd, g, preferred_element_type=F32, precision=HIGHEST)
    g_bwd = jnp.dot(t_bwd, g, preferred_element_type=F32, precision=HIGHEST)
    lane = lax.broadcasted_iota(jnp.int32, x.shape, 1)
    o_ref[...] = jnp.where(lane < n_heads, g_fwd, jnp.where(lane < 2 * n_heads, g_bwd, beta))


def _gdn_gates(ab, a_log, dt_bias):
    n = ab.shape[0]
    nh = a_log.shape[-1]
    pad = lambda v: jnp.zeros((1, LANES), F32).at[0, :2 * nh].set(v.reshape(-1))
    return pl.pallas_call(
        functools.partial(_gdn_gates_kernel, chunk=GDN_CHUNK, n_heads=nh),
        grid=(n // TM,),
        in_specs=[pl.BlockSpec((TM, LANES), lambda m: (m, 0)),
                  pl.BlockSpec((1, LANES), lambda m: (0, 0)),
                  pl.BlockSpec((1, LANES), lambda m: (0, 0))],
        out_specs=pl.BlockSpec((TM, LANES), lambda m: (m, 0)),
        out_shape=jax.ShapeDtypeStruct((n, LANES), F32),
        compiler_params=_cparams(("parallel",)),
        name="gdn_gates",
    )(ab, pad(a_log), pad(dt_bias))


def _gdn_feat_kernel(x_ref, prev_ref, next_ref, w_ref, o_ref, xs_ref, *, nct, n_tiles, n_norm, n_q, q_scale, width):
    m = pl.program_id(0)
    j = pl.program_id(1)
    tm, tn = x_ref.shape
    pad = width // 2
    prev_ok = jnp.logical_and(m != 0, m != nct)
    next_ok = jnp.logical_and(m != nct - 1, m != n_tiles - 1)
    xs_ref[0:HALO, :] = jnp.where(prev_ok, prev_ref[...].astype(F32), 0.0)
    xs_ref[HALO:HALO + tm, :] = x_ref[...].astype(F32)
    xs_ref[HALO + tm:, :] = jnp.where(next_ok, next_ref[...].astype(F32), 0.0)
    acc = jnp.zeros((tm, tn), F32)
    for t in range(width):
        acc = acc + w_ref[t:t + 1, :] * xs_ref[HALO - pad + t:HALO - pad + t + tm, :]
    y = _silu(acc)

    @pl.when(j >= n_norm)
    def _():
        o_ref[...] = y.astype(o_ref.dtype)

    @pl.when(j < n_norm)
    def _():
        scale = jnp.where(j < n_q, q_scale, 1.0).astype(F32)
        for hh in range(tn // GDN_HEAD_DIM):
            sl = slice(hh * GDN_HEAD_DIM, (hh + 1) * GDN_HEAD_DIM)
            yh = y[:, sl]
            inv = lax.rsqrt(jnp.sum(yh * yh, axis=-1, keepdims=True) + L2_EPS)
            o_ref[:, sl] = (yh * (inv * scale)).astype(o_ref.dtype)


def _gdn_features(p, conv_w, nct, qk_dim):
    n = p.shape[0]
    width, conv_dim = conv_w.shape
    tn = 512
    n_tiles = n // TM
    rb = TM // HALO
    last_rb = n // HALO - 1
    return pl.pallas_call(
        functools.partial(_gdn_feat_kernel, nct=nct, n_tiles=n_tiles, n_norm=2 * qk_dim // tn, n_q=qk_dim // tn,
                          q_scale=GDN_HEAD_DIM ** -0.5, width=width),
        grid=(n_tiles, conv_dim // tn),
        in_specs=[pl.BlockSpec((TM, tn), lambda m, j: (m, j)),
                  pl.BlockSpec((HALO, tn), lambda m, j: (jnp.maximum(m * rb - 1, 0), j)),
                  pl.BlockSpec((HALO, tn), lambda m, j: (jnp.minimum((m + 1) * rb, last_rb), j)),
                  pl.BlockSpec((width, tn), lambda m, j: (0, j))],
        out_specs=pl.BlockSpec((TM, tn), lambda m, j: (m, j)),
        out_shape=jax.ShapeDtypeStruct((n, conv_dim), BF16),
        scratch_shapes=[pltpu.VMEM((TM + 2 * HALO, tn), F32)],
        compiler_params=_cparams(("parallel", "parallel")),
        name="gdn_features",
    )(p, p, p, conv_w)


def _gdn_scan_kernel(*refs, chunk):
    ins, o_refs, s_ref = refs[:16], refs[16:18], refs[18]
    t = pl.program_id(1)

    @pl.when(t == 0)
    def _():
        s_ref[...] = jnp.zeros_like(s_ref)

    rows = ins[0].shape[0]
    hd = GDN_HEAD_DIM
    n_chunk = rows // chunk
    n_double = int(math.log2(chunk)) - 1
    ri = lax.broadcasted_iota(jnp.int32, (rows, rows), 0)
    ci = lax.broadcasted_iota(jnp.int32, (rows, rows), 1)
    same = (ri // chunk) == (ci // chunk)
    eye = jnp.where(ri == ci, 1.0, 0.0).astype(F32)
    r64 = lax.broadcasted_iota(jnp.int32, (chunk, chunk), 0)
    c64 = lax.broadcasted_iota(jnp.int32, (chunk, chunk), 1)

    dirs = []
    for d in range(2):
        q_ref, k_ref, kt_ref, v_ref, scol_ref, srow_ref, srowc_ref, glrow_ref = ins[8 * d:8 * d + 8]
        k = k_ref[...]
        dirs.append(dict(
            q=q_ref[...], k=k, v=v_ref[...], kt=kt_ref[...], srowc_ref=srowc_ref, glrow_ref=glrow_ref,
            scol=scol_ref[0, 0],
            srow=srow_ref[0, 0],
            incl=same & ((ri <= ci) if d else (ri >= ci)), strict=same & ((ri < ci) if d else (ri > ci)),
            incl64=(r64 <= c64) if d else (r64 >= c64),
            kk=_dot_nt(k, k),
            order=list(range(n_chunk - 1, -1, -1)) if d else list(range(n_chunk))))
    chains = [(d, a) for d in range(2) for a in range(2)]
    gcol = {c: dirs[c[0]]["scol"][:, c[1]:c[1] + 1] for c in chains}
    bcol = {c: dirs[c[0]]["scol"][:, 2 + c[1]:3 + c[1]] for c in chains}
    e_g = {c: jnp.exp(gcol[c]) for c in chains}
    e_k = {c: jnp.exp(dirs[c[0]]["scol"][:, 4 + c[1]:5 + c[1]] - gcol[c]) for c in chains}
    neg_l = {c: jnp.where(dirs[c[0]]["strict"],
                          -(bcol[c] * dirs[c[0]]["kk"]
                            * jnp.exp(jnp.where(dirs[c[0]]["incl"], gcol[c] - dirs[c[0]]["srow"][c[1]:c[1] + 1, :],
                                                -jnp.inf))), 0.0) for c in chains}
    inv = {c: eye + neg_l[c] for c in chains}
    pw = {c: neg_l[c].astype(BF16) for c in chains}
    for _ in range(n_double):
        pw = {c: _dot(pw[c], pw[c]).astype(BF16) for c in chains}
        inv = {c: inv[c] + _dot(inv[c].astype(BF16), pw[c]) for c in chains}
    rhs = {c: jnp.concatenate([dirs[c[0]]["v"][:, c[1] * hd:(c[1] + 1) * hd].astype(F32) * bcol[c],
                               dirs[c[0]]["k"].astype(F32) * (bcol[c] * e_g[c])], axis=1).astype(BF16) for c in chains}
    uw = {c: _dot(inv[c].astype(BF16), rhs[c]) for c in chains}

    state = [s_ref[d] for d in range(2)]
    for i in range(n_chunk):
        js = [dirs[d]["order"][i] for d in range(2)]
        rs = [slice(j * chunk, (j + 1) * chunk) for j in js]
        s_b = [st.astype(BF16) for st in state]
        qk_c = [_dot_nt(dirs[d]["q"][rs[d]], dirs[d]["k"][rs[d]]) for d in range(2)]
        q_s = [_dot(dirs[d]["q"][rs[d]], s_b[d]) for d in range(2)]
        v_new = {(d, a): uw[d, a][rs[d], :hd] - _dot(uw[d, a][rs[d], hd:].astype(BF16), s_b[d][:, a * hd:(a + 1) * hd])
                 for d, a in chains}
        qkd = {(d, a): (qk_c[d] * jnp.exp(jnp.where(dirs[d]["incl64"],
                                                    gcol[d, a][rs[d]] - dirs[d]["srowc_ref"][0, 0, js[d], a:a + 1, :],
                                                    -jnp.inf))).astype(BF16) for d, a in chains}
        for d, a in chains:
            o = e_g[d, a][rs[d]] * q_s[d][:, a * hd:(a + 1) * hd] + _dot(qkd[d, a], v_new[d, a].astype(BF16))
            o_refs[d][rs[d], a * hd:(a + 1) * hd] = o.astype(o_refs[d].dtype)
        for d in range(2):
            v_s = jnp.concatenate([v_new[d, a] * e_k[d, a][rs[d]] for a in range(2)], axis=1).astype(BF16)
            parts = [jnp.zeros((js[d] * chunk, 2 * hd), BF16), v_s,
                     jnp.zeros(((n_chunk - 1 - js[d]) * chunk, 2 * hd), BF16)]
            v_pad = jnp.concatenate([p for p in parts if p.shape[0]], axis=0)
            e_l = jnp.exp(dirs[d]["glrow_ref"][0, 0, js[d]])
            state[d] = state[d] * e_l + _dot(dirs[d]["kt"], v_pad)
    for d in range(2):
        s_ref[d] = state[d]


def _transpose_kernel(x_ref, o_ref):
    o_ref[...] = x_ref[...].astype(F32).T.astype(o_ref.dtype)


def _transposed_keys(feat, qk_dim):
    n = feat.shape[0]
    tn = 512
    return pl.pallas_call(
        _transpose_kernel,
        grid=(n // TM, qk_dim // tn),
        in_specs=[pl.BlockSpec((TM, tn), lambda m, j: (m, qk_dim // tn + j))],
        out_specs=pl.BlockSpec((tn, TM), lambda m, j: (j, m)),
        out_shape=jax.ShapeDtypeStruct((qk_dim, n), feat.dtype),
        compiler_params=_cparams(("parallel", "parallel")),
        name="gdn_keys_t",
    )(feat)


def _gdn_scan(feat, kt, scol, srow, srowc, glrow, nct):
    n = feat.shape[0]
    hd = GDN_HEAD_DIM
    n_qk = kt.shape[0] // hd
    n_tiles = n // TM
    cpt = TM // GDN_CHUNK

    def tile(d, t):
        return jnp.where(t < nct, nct - 1 - t, n_tiles - 1 - (t - nct)) if d else t

    def specs(d):
        return [pl.BlockSpec((TM, hd), lambda h, t: (tile(d, t), h)),
                pl.BlockSpec((TM, hd), lambda h, t: (tile(d, t), n_qk + h)),
                pl.BlockSpec((hd, TM), lambda h, t: (h, tile(d, t))),
                pl.BlockSpec((TM, 2 * hd), lambda h, t: (tile(d, t), n_qk + h)),
                pl.BlockSpec((1, 1, TM, 6), lambda h, t: (d, h, tile(d, t), 0)),
                pl.BlockSpec((1, 1, 4, TM), lambda h, t: (d, h, 0, tile(d, t))),
                pl.BlockSpec((1, 1, cpt, 4, GDN_CHUNK), lambda h, t: (d, h, tile(d, t), 0, 0)),
                pl.BlockSpec((1, 1, cpt, 1, 2 * hd), lambda h, t: (d, h, tile(d, t), 0, 0))]

    args = [feat, feat, kt, feat, scol, srow, srowc, glrow]
    out = jax.ShapeDtypeStruct((n, 2 * n_qk * hd), BF16)
    return pl.pallas_call(
        functools.partial(_gdn_scan_kernel, chunk=GDN_CHUNK),
        grid=(n_qk, n_tiles),
        in_specs=specs(0) + specs(1),
        out_specs=[pl.BlockSpec((TM, 2 * hd), lambda h, t: (tile(0, t), h)),
                   pl.BlockSpec((TM, 2 * hd), lambda h, t: (tile(1, t), h))],
        out_shape=[out, out],
        scratch_shapes=[pltpu.VMEM((2, hd, 2 * hd), F32)],
        compiler_params=_cparams(("parallel", "arbitrary")),
        name="gdn_scan",
    )(*args, *args)


def _post_mixer(y, h_ref, mod_ref, lnw_ref, lnb_ref, rw_ref, rb_ref, h1_ref, v_ref, rt_ref, cnt_ref, carry_ref,
                alpha, n_e):
    @pl.when(pl.program_id(0) == 0)
    def _():
        carry_ref[...] = jnp.zeros_like(carry_ref)

    x = alpha * h_ref[...] + mod_ref[0, 2:3, :] * y
    h1 = _layer_norm(x, lnw_ref[...], lnb_ref[...])
    h1_ref[...] = h1
    v = _modulate(h1, mod_ref, 3, 4)
    v_ref[...] = _pack_bf16_pairs(v)
    logits = jnp.dot(v, rw_ref[...], preferred_element_type=F32, precision=HIGHEST) + rb_ref[...]

    tm = logits.shape[0]
    lane = lax.broadcasted_iota(jnp.int32, logits.shape, 1)
    rest = jnp.where(lane < n_e, logits, -jnp.inf)
    top_val, top_idx, picked = [], [], []
    for _ in range(TOP_K):
        mx = jnp.max(rest, axis=-1, keepdims=True)
        idx = jnp.min(jnp.where(rest == mx, lane, LANES), axis=-1, keepdims=True)
        hit = lane == idx
        top_val.append(mx)
        top_idx.append(idx)
        picked.append(hit)
        rest = jnp.where(hit, -jnp.inf, rest)
    e = [jnp.exp(tv - top_val[0]) for tv in top_val]
    denom = functools.reduce(lambda a, b: a + b, e)
    onehot = functools.reduce(lambda a, b: a + b, [jnp.where(hit, 1.0, 0.0) for hit in picked])
    r = lax.broadcasted_iota(jnp.int32, (tm, tm), 0)
    c = lax.broadcasted_iota(jnp.int32, (tm, tm), 1)
    earlier = jnp.where(r > c, 1.0, 0.0).astype(BF16)
    before = _dot(earlier, onehot.astype(BF16)) + carry_ref[0:1, :]
    rt = jnp.zeros(logits.shape, F32)
    for k in range(TOP_K):
        rank = jnp.sum(jnp.where(picked[k], before, 0.0), axis=-1, keepdims=True)
        rt = jnp.where(lane == k, top_idx[k].astype(F32), rt)
        rt = jnp.where(lane == TOP_K + k, rank, rt)
        rt = jnp.where(lane == 2 * TOP_K + k, e[k] / denom, rt)
    rt_ref[...] = rt
    total = carry_ref[0:1, :] + jnp.sum(onehot, axis=0, keepdims=True)
    carry_ref[...] = jnp.broadcast_to(total, carry_ref.shape)
    cnt_ref[...] = jnp.broadcast_to(total, cnt_ref.shape)


def _gdn_out_kernel(of_ref, ob_ref, z_ref, nw_ref, w_ref, h_ref, mod_ref, lnw_ref, lnb_ref, rw_ref, rb_ref,
                    h1_ref, v_ref, rt_ref, cnt_ref, carry_ref, a_ref, *, alpha, n_e):
    hd = GDN_HEAD_DIM
    for hh in range(of_ref.shape[1] // hd):
        sl = slice(hh * hd, (hh + 1) * hd)
        o = of_ref[:, sl].astype(F32) + ob_ref[:, sl].astype(F32)
        o = o * lax.rsqrt(jnp.mean(o * o, axis=-1, keepdims=True) + NORM_EPS) * nw_ref[...]
        a_ref[:, sl] = (o * _silu(z_ref[:, sl].astype(F32))).astype(BF16)
    y = _dot(a_ref[...], w_ref[...])
    _post_mixer(y, h_ref, mod_ref, lnw_ref, lnb_ref, rw_ref, rb_ref, h1_ref, v_ref, rt_ref, cnt_ref, carry_ref,
                alpha, n_e)


def _attn_out_kernel(a_ref, w_ref, h_ref, mod_ref, lnw_ref, lnb_ref, rw_ref, rb_ref, h1_ref, v_ref, rt_ref, cnt_ref,
                     carry_ref, *, alpha, n_e):
    y = _dot(a_ref[...], w_ref[...])
    _post_mixer(y, h_ref, mod_ref, lnw_ref, lnb_ref, rw_ref, rb_ref, h1_ref, v_ref, rt_ref, cnt_ref, carry_ref,
                alpha, n_e)


def _mixer_out(kind, acts, w_out, h, mod, nct, ln_w, ln_b, router_w, router_b, alpha):
    n, d = h.shape
    row = lambda width: pl.BlockSpec((TM, width), lambda m: (m, 0))
    full = lambda arr: pl.BlockSpec(arr.shape, lambda m: (0,) * arr.ndim)
    n_e = router_w.shape[1]
    rw = jnp.zeros((d, LANES), F32).at[:, :n_e].set(router_w)
    rb = jnp.zeros((1, LANES), F32).at[0, :n_e].set(router_b)
    tail = [w_out, h, mod, ln_w.reshape(1, d), ln_b.reshape(1, d), rw, rb]
    tail_specs = [full(w_out), row(d), pl.BlockSpec((1, 6, d), lambda m: (jnp.where(m < nct, 0, 1), 0, 0)),
                  pl.BlockSpec((1, d), lambda m: (0, 0)), pl.BlockSpec((1, d), lambda m: (0, 0)), full(rw), full(rb)]
    if kind == "gdn":
        o_f, o_b, p, norm_w, z_col = acts
        v_dim = o_f.shape[1]
        body = functools.partial(_gdn_out_kernel, alpha=alpha, n_e=n_e)
        args = [o_f, o_b, p, norm_w.reshape(1, -1)] + tail
        specs = [row(v_dim), row(v_dim), pl.BlockSpec((TM, v_dim), lambda m: (m, z_col)),
                 pl.BlockSpec((1, norm_w.shape[0]), lambda m: (0, 0))] + tail_specs
        scratch = [pltpu.VMEM((SUBLANES, LANES), F32), pltpu.VMEM((TM, v_dim), BF16)]
    else:
        (a,) = acts
        body = functools.partial(_attn_out_kernel, alpha=alpha, n_e=n_e)
        args = [a] + tail
        specs = [row(a.shape[1])] + tail_specs
        scratch = [pltpu.VMEM((SUBLANES, LANES), F32)]
    return pl.pallas_call(
        body,
        grid=(n // TM,),
        in_specs=specs,
        out_specs=[row(d), row(d // 2), row(LANES), pl.BlockSpec((SUBLANES, LANES), lambda m: (0, 0))],
        out_shape=[jax.ShapeDtypeStruct((n, d), F32), jax.ShapeDtypeStruct((n, d // 2), jnp.uint32),
                   jax.ShapeDtypeStruct((n, LANES), F32), jax.ShapeDtypeStruct((SUBLANES, LANES), F32)],
        scratch_shapes=scratch,
        compiler_params=_cparams(("arbitrary",)),
        name=kind + "_out_ln",
    )(*args)


def _attn_kernel(q_ref, k_ref, vt_ref, lam_ref, sw_ref, o_ref, acc_ref, m_ref, l_ref, sa_ref, sb_ref,
                 *, tk, nct, l_ctx, n_pairs, lambda_init):
    mt = pl.program_id(1)
    q = q_ref[...]
    lane = lax.broadcasted_iota(jnp.int32, q.shape, 1)
    zero = jnp.zeros_like(q)
    q_maps = (jnp.where(lane < DIFF_HEAD_DIM, q, zero), jnp.where(lane >= DIFF_HEAD_DIM, q, zero))
    kb = vt_ref.shape[3]
    acc_ref[...] = jnp.zeros_like(acc_ref)
    m_ref[...] = jnp.full_like(m_ref, -jnp.inf)
    l_ref[...] = jnp.zeros_like(l_ref)

    def scores(start, s_ref):
        kj = k_ref[pl.ds(start, tk), :]
        for i, qm in enumerate(q_maps):
            s_ref[i] = _dot_nt(kj, qm)

    def absorb(score_of_map, block0, n_blocks):
        for i in range(2):
            s = score_of_map(i)
            m_old = m_ref[i]
            m_new = jnp.maximum(m_old, jnp.max(s, axis=0, keepdims=True))
            alpha = jnp.exp2(m_old - m_new)
            p = jnp.exp2(s - m_new)
            l_ref[i] = alpha * l_ref[i] + jnp.sum(p, axis=0, keepdims=True)
            p = p.astype(BF16)
            pv = _dot(vt_ref[0, block0], p[0:kb])
            for b in range(1, n_blocks):
                pv = pv + _dot(vt_ref[0, block0 + b], p[b * kb:(b + 1) * kb])
            acc_ref[i] = alpha * acc_ref[i] + pv
            m_ref[i] = m_new

    k_ctx = k_ref[0:l_ctx, :]
    absorb(lambda i: _dot_nt(k_ctx, q_maps[i]), 0, l_ctx // kb)

    @pl.when(mt >= nct)
    def _():
        last = l_ctx + (2 * n_pairs - 1) * tk
        scores(l_ctx, sa_ref)

        def body(jj, carry):
            c0 = pl.multiple_of(l_ctx + 2 * jj * tk, LANES)
            c1 = pl.multiple_of(c0 + tk, LANES)
            c2 = pl.multiple_of(jnp.minimum(c1 + tk, last), LANES)
            scores(c1, sb_ref)
            absorb(lambda i: sa_ref[i], c0 // kb, tk // kb)
            scores(c2, sa_ref)
            absorb(lambda i: sb_ref[i], c1 // kb, tk // kb)
            return carry

        lax.fori_loop(0, n_pairs, body, 0)

    lam = lam_ref[...]
    lam_full = (jnp.exp(jnp.sum(lam[0:1] * lam[1:2], axis=-1, keepdims=True))
                - jnp.exp(jnp.sum(lam[2:3] * lam[3:4], axis=-1, keepdims=True)) + lambda_init)
    o = acc_ref[0] / l_ref[0] - lam_full * (acc_ref[1] / l_ref[1])
    o = o * lax.rsqrt(jnp.mean(o * o, axis=0, keepdims=True) + NORM_EPS) * sw_ref[...]
    o_ref[...] = (o * (1.0 - lambda_init)).T.astype(o_ref.dtype)


def _transpose_heads_kernel(x_ref, o_ref):
    hd = o_ref.shape[2]
    for hh in range(o_ref.shape[0]):
        o_ref[hh, 0] = x_ref[:, hh * hd:(hh + 1) * hd].astype(F32).T.astype(o_ref.dtype)


def _transposed_values(qkv, vd):
    n = qkv.shape[0]
    d = qkv.shape[1] // 3
    tn = 512
    return pl.pallas_call(
        _transpose_heads_kernel,
        grid=(n // TM, d // tn),
        in_specs=[pl.BlockSpec((TM, tn), lambda m, j: (m, 2 * d // tn + j))],
        out_specs=pl.BlockSpec((tn // vd, 1, vd, TM), lambda m, j: (j, m, 0, 0)),
        out_shape=jax.ShapeDtypeStruct((d // vd, n // TM, vd, TM), qkv.dtype),
        compiler_params=_cparams(("parallel", "parallel")),
        name="attn_values_t",
    )(qkv)


def _diff_attention(qkv, lam, subln_w, nct, l_ctx, lambda_init):
    n = qkv.shape[0]
    d = qkv.shape[1] // 3
    vd = subln_w.shape[0]
    n_heads = d // vd
    l_lat = n - l_ctx
    tk = math.gcd(l_lat // 2, ATTN_TK)
    assert tk % TM == 0
    vt = _transposed_values(qkv, vd)
    return pl.pallas_call(
        functools.partial(_attn_kernel, tk=tk, nct=nct, l_ctx=l_ctx, n_pairs=l_lat // (2 * tk),
                          lambda_init=lambda_init),
        grid=(n_heads, n // TM),
        in_specs=[pl.BlockSpec((TM, vd), lambda h, m: (m, h)),
                  pl.BlockSpec((n, vd), lambda h, m: (0, n_heads + h)),
                  pl.BlockSpec((1, n // TM, vd, TM), lambda h, m: (h, 0, 0, 0)),
                  pl.BlockSpec(lam.shape, lambda h, m: (0, 0)),
                  pl.BlockSpec((vd, 1), lambda h, m: (0, 0))],
        out_specs=pl.BlockSpec((TM, vd), lambda h, m: (m, h)),
        out_shape=jax.ShapeDtypeStruct((n, d), BF16),
        scratch_shapes=[pltpu.VMEM((2, vd, TM), F32), pltpu.VMEM((2, 1, TM), F32), pltpu.VMEM((2, 1, TM), F32),
                        pltpu.VMEM((2, tk, TM), F32), pltpu.VMEM((2, tk, TM), F32)],
        compiler_params=_cparams(("parallel", "parallel")),
        name="diff_attn",
    )(qkv, qkv, vt, lam, subln_w.reshape(vd, 1))


def _moe_kernel(be_ref, nb_ref, x_ref, wgu_ref, bgu_ref, wdn_ref, bdn_ref, y_ref, wgu_s, wdn_s, *, d_ff):
    b = pl.program_id(0)
    changed = jnp.logical_or(b == 0, be_ref[b] != be_ref[jnp.maximum(b - 1, 0)])

    @pl.when(changed)
    def _():
        wgu_s[...] = wgu_ref[0, 0].astype(BF16)
        wdn_s[...] = wdn_ref[0, 0].astype(BF16)

    @pl.when(b < nb_ref[0])
    def _():
        h = _dot(_unpack_bf16_pairs(x_ref[...]), wgu_s[...]) + bgu_ref[0, 0]
        gate = jnp.minimum(h[:, :d_ff], SWIGLU_LIMIT)
        up = jnp.clip(h[:, d_ff:], -SWIGLU_LIMIT, SWIGLU_LIMIT)
        glu = gate * jax.nn.sigmoid(gate * SWIGLU_ALPHA)
        act = ((up + 1.0) * glu).astype(BF16)
        y_ref[...] = (_dot(act, wdn_s[...]) + bdn_ref[0, 0]).astype(y_ref.dtype)

    @pl.when(b >= nb_ref[0])
    def _():
        y_ref[...] = jnp.zeros_like(y_ref)


def _moe_experts(x_sorted, block_e, n_used, layer, w_gu, b_gu, w_dn, b_dn):
    n_rows = x_sorted.shape[0]
    depth, n_e, d, f2 = w_gu.shape
    grid_spec = pltpu.PrefetchScalarGridSpec(
        num_scalar_prefetch=2,
        grid=(n_rows // MOE_BM,),
        in_specs=[pl.BlockSpec((MOE_BM, d // 2), lambda b, be, nb: (b, 0)),
                  pl.BlockSpec((1, 1, d, f2), lambda b, be, nb: (layer, be[b], 0, 0)),
                  pl.BlockSpec((1, 1, 1, f2), lambda b, be, nb: (layer, be[b], 0, 0)),
                  pl.BlockSpec((1, 1, f2 // 2, d), lambda b, be, nb: (layer, be[b], 0, 0)),
                  pl.BlockSpec((1, 1, 1, d), lambda b, be, nb: (layer, be[b], 0, 0))],
        out_specs=pl.BlockSpec((MOE_BM, d), lambda b, be, nb: (b, 0)),
        scratch_shapes=[pltpu.VMEM((d, f2), BF16), pltpu.VMEM((f2 // 2, d), BF16)],
    )
    return pl.pallas_call(
        functools.partial(_moe_kernel, d_ff=f2 // 2),
        grid_spec=grid_spec,
        out_shape=jax.ShapeDtypeStruct((n_rows, d), BF16),
        compiler_params=_cparams(("arbitrary",)),
        name="moe_experts",
    )(block_e, n_used, x_sorted, w_gu, b_gu.reshape(depth, n_e, 1, f2), w_dn, b_dn.reshape(depth, n_e, 1, d))


def _moe_combine_kernel(y0_ref, y1_ref, y2_ref, y3_ref, g_ref, h_ref, mod_ref, lnw_ref, lnb_ref, o_ref, *, alpha):
    g = g_ref[...]
    f = jnp.zeros(h_ref.shape, F32)
    for i, y_ref in enumerate((y0_ref, y1_ref, y2_ref, y3_ref)):
        f = f + g[:, i:i + 1] * y_ref[...].astype(F32)
    x = alpha * h_ref[...] + mod_ref[0, 5:6, :] * f
    o_ref[...] = _layer_norm(x, lnw_ref[...], lnb_ref[...])


def _moe_combine(ys, gates, h1, mod, nct, ln_w, ln_b, alpha):
    n, d = h1.shape
    row = lambda width: pl.BlockSpec((TM, width), lambda m: (m, 0))
    return pl.pallas_call(
        functools.partial(_moe_combine_kernel, alpha=alpha),
        grid=(n // TM,),
        in_specs=[row(d)] * TOP_K + [row(TOP_K), row(d),
                                     pl.BlockSpec((1, 6, d), lambda m: (jnp.where(m < nct, 0, 1), 0, 0)),
                                     pl.BlockSpec((1, d), lambda m: (0, 0)), pl.BlockSpec((1, d), lambda m: (0, 0))],
        out_specs=row(d),
        out_shape=jax.ShapeDtypeStruct((n, d), F32),
        compiler_params=_cparams(("parallel",)),
        name="moe_combine_ln",
    )(*ys, gates, h1, mod, ln_w.reshape(1, d), ln_b.reshape(1, d))


def _sc_gather_rows(table, idx):
    n_rows = idx.shape[0]
    d = table.shape[1]
    n_workers = SC_CORES * SC_SUBCORES
    per_worker = n_rows // n_workers
    step = SC_GATHER_ROWS
    assert per_worker * n_workers == n_rows and per_worker % (2 * step) == 0
    mesh = plsc.VectorSubcoreMesh(core_axis_name="c", subcore_axis_name="s",
                                  num_cores=SC_CORES, num_subcores=SC_SUBCORES)

    def body(table_hbm, idx_hbm, out_hbm, idx_a, idx_b, rows_a, rows_b, sem_a, sem_b):
        base = (lax.axis_index("s") * SC_CORES + lax.axis_index("c")) * per_worker

        @pl.loop(0, per_worker // (2 * step))
        def _(i):
            off_a = pl.multiple_of(base + 2 * i * step, SUBLANES)
            off_b = pl.multiple_of(off_a + step, SUBLANES)
            pltpu.sync_copy(idx_hbm.at[pl.ds(off_a, step)], idx_a)
            pltpu.sync_copy(idx_hbm.at[pl.ds(off_b, step)], idx_b)
            copy_a = pltpu.async_copy(table_hbm.at[idx_a], rows_a, sem_a)
            copy_b = pltpu.async_copy(table_hbm.at[idx_b], rows_b, sem_b)
            copy_a.wait()
            pltpu.sync_copy(rows_a, out_hbm.at[pl.ds(off_a, step)])
            copy_b.wait()
            pltpu.sync_copy(rows_b, out_hbm.at[pl.ds(off_b, step)])

    return pl.kernel(
        body,
        out_type=jax.ShapeDtypeStruct((n_rows, d), table.dtype),
        mesh=mesh,
        scratch_types=[pltpu.VMEM((step,), jnp.int32), pltpu.VMEM((step,), jnp.int32),
                       pltpu.VMEM((step, d), table.dtype), pltpu.VMEM((step, d), table.dtype),
                       pltpu.SemaphoreType.DMA, pltpu.SemaphoreType.DMA],
        name="sc_gather_rows",
    )(table, idx)


def _moe(v_rows, route, counts, n_e, h1, mod, nct, ln_w, ln_b, alpha, layer, w_gu, b_gu, w_dn, b_dn):
    n = v_rows.shape[0]
    top_idx = route[:, :TOP_K].astype(jnp.int32)
    rank = route[:, TOP_K:2 * TOP_K].astype(jnp.int32)
    gates = route[:, 2 * TOP_K:3 * TOP_K]
    counts = counts[0, :n_e].astype(jnp.int32)
    nk = n * TOP_K
    padded = (counts + MOE_BM - 1) // MOE_BM * MOE_BM
    pad_end = jnp.cumsum(padded)
    pad_start = pad_end - padded
    dest = (pad_start[top_idx] + rank).reshape(nk)
    n_blocks = -(-nk // MOE_BM) + n_e
    rows_unit = 2 * SC_GATHER_ROWS * SC_CORES * SC_SUBCORES
    n_rows = -(-n_blocks * MOE_BM // rows_unit) * rows_unit
    n_blocks = n_rows // MOE_BM
    row_tok = jnp.zeros((n_rows,), jnp.int32).at[dest].set(jnp.arange(nk, dtype=jnp.int32) // TOP_K)
    block_row0 = jnp.arange(n_blocks, dtype=jnp.int32) * MOE_BM
    block_e = jnp.minimum(jnp.sum(pad_end[None, :] <= block_row0[:, None], axis=1, dtype=jnp.int32), n_e - 1)
    n_used = (pad_end[-1:] // MOE_BM).astype(jnp.int32)
    x_sorted = _sc_gather_rows(v_rows, row_tok)
    yb = _moe_experts(x_sorted, block_e, n_used, layer, w_gu, b_gu, w_dn, b_dn)
    dest2 = dest.reshape(n, TOP_K)
    ys = [yb[dest2[:, i]] for i in range(TOP_K)]
    return _moe_combine(ys, gates, h1, mod, nct, ln_w, ln_b, alpha)


def _rope_tables(l_ctx, l_lat):
    rows = l_lat // GRID_W
    row = jnp.repeat(jnp.arange(rows, dtype=F32), GRID_W)
    col = jnp.tile(jnp.arange(GRID_W, dtype=F32), rows)
    axis_dim = DIFF_HEAD_DIM // 2
    inv_freq = ROPE_THETA ** (-jnp.arange(0, axis_dim, 2, dtype=F32) / axis_dim)
    ang_r = row[:, None] * inv_freq
    ang_c = col[:, None] * inv_freq
    ang = jnp.concatenate([ang_r, ang_r, ang_c, ang_c], -1)
    ang = jnp.concatenate([jnp.zeros((l_ctx, DIFF_HEAD_DIM), F32), ang], 0)
    ang = jnp.concatenate([ang, ang], -1)
    return jnp.cos(ang), jnp.sin(ang)


def _gdn_scalars(gates, n_vh):
    n = gates.shape[0]
    n_qk = n_vh // 2
    nc = n // GDN_CHUNK
    gc = gates[:, :2 * n_vh].reshape(n, 2, n_qk, 2)
    beta = gates[:, 2 * n_vh:4 * n_vh].reshape(n, 2, n_qk, 2)
    gcc = gc.reshape(nc, GDN_CHUNK, 2, n_qk, 2)
    gl = jnp.stack([gcc[:, -1, 0], gcc[:, 0, 1]], axis=1)
    gl_tok = jnp.repeat(gl, GDN_CHUNK, axis=0)
    col = jnp.concatenate([gc, beta, gl_tok], axis=-1)
    scol = jnp.transpose(col, (1, 2, 0, 3))
    srow = jnp.transpose(col[..., :4], (1, 2, 3, 0))
    srowc = jnp.transpose(col[..., :4].reshape(nc, GDN_CHUNK, 2, n_qk, 4), (2, 3, 0, 4, 1))
    glrow = jnp.broadcast_to(jnp.transpose(gl, (1, 2, 0, 3))[..., None], (2, n_qk, nc, 2, GDN_HEAD_DIM))
    glrow = glrow.reshape(2, n_qk, nc, 1, 2 * GDN_HEAD_DIM)
    return scol, srow, srowc, glrow


def kernel(x, c, ctx, c_ctx, ada_w, ada_b, ln_w, ln_b, gdn_w_in, gdn_conv_w, gdn_a_log, gdn_dt_bias, gdn_norm_w,
           gdn_w_out, diff_w_in, diff_lambda, diff_subln_w, diff_w_out, router_w, router_b, moe_w_gate_up,
           moe_b_gate_up, moe_w_down, moe_b_down):
    batch, l_lat, d = x.shape
    l_ctx = ctx.shape[1]
    depth = ada_w.shape[0]
    assert batch == 1 and l_ctx % TM == 0 and l_lat % TM == 0 and l_lat % GRID_W == 0
    nct = l_ctx // TM
    n_e = router_w.shape[-1]
    alpha = (2 * depth) ** 0.25
    n_vh = gdn_a_log.shape[-1]
    v_dim = n_vh * GDN_HEAD_DIM
    conv_dim = gdn_conv_w.shape[-1]
    qk_dim = (conv_dim - v_dim) // 2
    n_qk = qk_dim // GDN_HEAD_DIM

    h = jnp.concatenate([ctx[0], x[0]], axis=0)
    n = h.shape[0]
    mods = _modulations(c_ctx, c, ada_w, ada_b)
    cos, sin = _rope_tables(l_ctx, l_lat)

    for i in range(depth):
        j = i // 2
        mod = mods[i]
        if i % 2 == 0:
            w_in = gdn_w_in[j]
            w_main = w_in[:, :conv_dim + v_dim].astype(BF16)
            w_ab = jnp.zeros((d, LANES), F32).at[:, :4 * n_vh].set(w_in[:, conv_dim + v_dim:]).astype(BF16)
            p, ab = _project(h, mod, nct, [w_main, w_ab], [BF16, F32])
            gates = _gdn_gates(ab, gdn_a_log[j], gdn_dt_bias[j])
            feat = _gdn_features(p, gdn_conv_w[j], nct, qk_dim)
            kt = _transposed_keys(feat, qk_dim)
            scol, srow, srowc, glrow = _gdn_scalars(gates, n_vh)
            o_f, o_b = _gdn_scan(feat, kt, scol, srow, srowc, glrow, nct)
            acts = (o_f, o_b, p, gdn_norm_w[j], conv_dim // v_dim)
            h1, v, route, counts = _mixer_out("gdn", acts, gdn_w_out[j].astype(BF16), h, mod, nct, ln_w[i, 0], ln_b[i, 0],
                                       router_w[i], router_b[i], alpha)
        else:
            lambda_init = 0.8 - 0.6 * math.exp(-0.3 * i)
            qkv = _qkv_rope(h, mod, nct, diff_w_in[j].astype(BF16), cos, sin)
            a = _diff_attention(qkv, diff_lambda[j], diff_subln_w[j], nct, l_ctx, lambda_init)
            h1, v, route, counts = _mixer_out("attn", (a,), diff_w_out[j].astype(BF16), h, mod, nct, ln_w[i, 0], ln_b[i, 0],
                                       router_w[i], router_b[i], alpha)
        h = _moe(v, route, counts, n_e, h1, mod, nct, ln_w[i, 1], ln_b[i, 1], alpha,
                 i, moe_w_gate_up, moe_b_gate_up, moe_w_down, moe_b_down)
    return h[l_ctx:].reshape(batch, l_lat, d)
```

```python
import functools
import math

import jax
import jax.numpy as jnp
from jax import lax
from jax.experimental import pallas as pl
from jax.experimental.pallas import tpu as pltpu
from jax.experimental.pallas import tpu_sc as plsc

F32 = jnp.float32
BF16 = jnp.bfloat16
HIGHEST = lax.Precision.HIGHEST

GRID_W = 64
GDN_HEAD_DIM = 128
GDN_CHUNK = 64
DIFF_HEAD_DIM = 64
ROPE_THETA = 10000.0
TOP_K = 4
SWIGLU_LIMIT = 7.0
SWIGLU_ALPHA = 1.702
NORM_EPS = 1e-5
L2_EPS = 1e-6

LANES = 128
SUBLANES = 8
HALO = 16
VMEM_LIMIT = 56 * 1024 * 1024

TM = 256
MOE_BM = 256
ATTN_TK = 2048
GDN_PAIRS_PER_STEP = 2

SC_CORES = 2
SC_SUBCORES = 16
SC_GATHER_ROWS = 32


def _cparams(sem):
    return pltpu.CompilerParams(dimension_semantics=sem, vmem_limit_bytes=VMEM_LIMIT)


def _silu(x):
    return x * jax.nn.sigmoid(x)


def _dot(a, b):
    return jnp.dot(a, b, preferred_element_type=F32)


def _dot_nt(a, b):
    return lax.dot_general(a, b, (((1,), (1,)), ((), ())), preferred_element_type=F32)


def _modulate(h, mod_ref, shift, scale):
    return h * (1.0 + mod_ref[0, scale:scale + 1, :]) + mod_ref[0, shift:shift + 1, :]


def _pack_bf16_pairs(x):
    bits = pltpu.bitcast(x.astype(BF16).astype(F32), jnp.uint32)
    half = x.shape[1] // 2
    return (bits[:, half:] & jnp.uint32(0xFFFF0000)) | (bits[:, :half] >> 16)


def _unpack_bf16_pairs(w):
    lo = pltpu.bitcast(w << 16, F32)
    hi = pltpu.bitcast(w & jnp.uint32(0xFFFF0000), F32)
    return jnp.concatenate([lo, hi], axis=1).astype(BF16)


def _layer_norm(x, w, b):
    mu = jnp.mean(x, axis=-1, keepdims=True)
    xc = x - mu
    var = jnp.mean(xc * xc, axis=-1, keepdims=True)
    return xc * lax.rsqrt(var + NORM_EPS) * w + b


def _mod_kernel(s_ref, w_ref, b_ref, o_ref):
    s = _silu(s_ref[...])
    o_ref[0] = jnp.dot(s, w_ref[0], preferred_element_type=F32, precision=HIGHEST) + b_ref[0]


def _modulations(c_ctx, c, ada_w, ada_b):
    depth, d, d6 = ada_w.shape
    tn = d6 // 4
    s = jnp.zeros((SUBLANES, d), F32).at[0].set(c_ctx).at[1].set(c[0])
    out = pl.pallas_call(
        _mod_kernel,
        grid=(depth, d6 // tn),
        in_specs=[pl.BlockSpec((SUBLANES, d), lambda i, n: (0, 0)),
                  pl.BlockSpec((1, d, tn), lambda i, n: (i, 0, n)),
                  pl.BlockSpec((1, 1, tn), lambda i, n: (i, 0, n))],
        out_specs=pl.BlockSpec((1, SUBLANES, tn), lambda i, n: (i, 0, n)),
        out_shape=jax.ShapeDtypeStruct((depth, SUBLANES, d6), F32),
        compiler_params=_cparams(("parallel", "parallel")),
        name="adaln_mod",
    )(s, ada_w, ada_b.reshape(depth, 1, d6))
    return out[:, :2].reshape(depth, 2, 6, d)


def _proj_kernel(h_ref, mod_ref, *refs, n_w, chunk):
    w_refs, o_refs = refs[:n_w], refs[n_w:]
    u = _modulate(h_ref[...], mod_ref, 0, 1).astype(BF16)
    for w_ref, o_ref in zip(w_refs, o_refs):
        n = w_ref.shape[1]
        for j in range(0, n, chunk):
            jc = min(chunk, n - j)
            o_ref[:, j:j + jc] = _dot(u, w_ref[:, j:j + jc]).astype(o_ref.dtype)


def _project(h, mod, nct, weights, out_dtypes):
    n, d = h.shape
    return pl.pallas_call(
        functools.partial(_proj_kernel, n_w=len(weights), chunk=512),
        grid=(n // TM,),
        in_specs=[pl.BlockSpec((TM, d), lambda m: (m, 0)),
                  pl.BlockSpec((1, 6, d), lambda m: (jnp.where(m < nct, 0, 1), 0, 0))]
                 + [pl.BlockSpec(w.shape, lambda m: (0, 0)) for w in weights],
        out_specs=[pl.BlockSpec((TM, w.shape[1]), lambda m: (m, 0)) for w in weights],
        out_shape=[jax.ShapeDtypeStruct((n, w.shape[1]), dt) for w, dt in zip(weights, out_dtypes)],
        compiler_params=_cparams(("parallel",)),
        name="mod_proj",
    )(h, mod, *weights)


def _qkv_rope_kernel(h_ref, mod_ref, w_ref, cos_ref, sin_ref, o_ref, *, chunk, n_rope, n_q, q_scale):
    u = _modulate(h_ref[...], mod_ref, 0, 1).astype(BF16)
    rep = chunk // LANES
    cos = jnp.concatenate([cos_ref[...]] * rep, axis=1)
    sin = jnp.concatenate([sin_ref[...]] * rep, axis=1)
    lane = lax.broadcasted_iota(jnp.int32, (h_ref.shape[0], chunk), 1)
    first = (lane % 32) < 16
    for j in range(w_ref.shape[1] // chunk):
        y = _dot(u, w_ref[:, j * chunk:(j + 1) * chunk])
        if j < n_rope:
            rot = jnp.where(first, -pltpu.roll(y, chunk - 16, 1), pltpu.roll(y, 16, 1))
            y = y * cos + rot * sin
            if j < n_q:
                y = y * q_scale
        o_ref[:, j * chunk:(j + 1) * chunk] = y.astype(o_ref.dtype)


def _qkv_rope(h, mod, nct, w, cos, sin):
    n, d = h.shape
    chunk = 512
    return pl.pallas_call(
        functools.partial(_qkv_rope_kernel, chunk=chunk, n_rope=2 * d // chunk, n_q=d // chunk,
                          q_scale=DIFF_HEAD_DIM ** -0.5 * math.log2(math.e)),
        grid=(n // TM,),
        in_specs=[pl.BlockSpec((TM, d), lambda m: (m, 0)),
                  pl.BlockSpec((1, 6, d), lambda m: (jnp.where(m < nct, 0, 1), 0, 0)),
                  pl.BlockSpec(w.shape, lambda m: (0, 0)),
                  pl.BlockSpec((TM, LANES), lambda m: (m, 0)),
                  pl.BlockSpec((TM, LANES), lambda m: (m, 0))],
        out_specs=pl.BlockSpec((TM, w.shape[1]), lambda m: (m, 0)),
        out_shape=jax.ShapeDtypeStruct((n, w.shape[1]), BF16),
        compiler_params=_cparams(("parallel",)),
        name="qkv_rope",
    )(h, mod, w, cos, sin)


def _gdn_gates_kernel(ab_ref, alog_ref, dtb_ref, o_ref, *, chunk, n_heads):
    x = ab_ref[...]
    t = x + dtb_ref[...]
    softplus = jnp.maximum(t, 0.0) + jnp.log1p(jnp.exp(-jnp.abs(t)))
    g = -jnp.exp(alog_ref[...]) * softplus
    beta = jax.nn.sigmoid(x)
    tm = x.shape[0]
    r = lax.broadcasted_iota(jnp.int32, (tm, tm), 0)
    c = lax.broadcasted_iota(jnp.int32, (tm, tm), 1)
    same = (r // chunk) == (c // chunk)
    t_fwd = jnp.where(same & (r >= c), 1.0, 0.0).astype(F32)
    t_bwd = jnp.where(same & (r <= c), 1.0, 0.0).astype(F32)
    g_fwd = jnp.dot(t_fwd, g, preferred_element_type=F32, precision=HIGHEST)
    g_bwd = jnp.dot(t_bwd, g, preferred_element_type=F32, precision=HIGHEST)
    lane = lax.broadcasted_iota(jnp.int32, x.shape, 1)
    o_ref[...] = jnp.where(lane < n_heads, g_fwd, jnp.where(lane < 2 * n_heads, g_bwd, beta))


def _gdn_gates(ab, a_log, dt_bias):
    n = ab.shape[0]
    nh = a_log.shape[-1]
    pad = lambda v: jnp.zeros((1, LANES), F32).at[0, :2 * nh].set(v.reshape(-1))
    return pl.pallas_call(
        functools.partial(_gdn_gates_kernel, chunk=GDN_CHUNK, n_heads=nh),
        grid=(n // TM,),
        in_specs=[pl.BlockSpec((TM, LANES), lambda m: (m, 0)),
                  pl.BlockSpec((1, LANES), lambda m: (0, 0)),
                  pl.BlockSpec((1, LANES), lambda m: (0, 0))],
        out_specs=pl.BlockSpec((TM, LANES), lambda m: (m, 0)),
        out_shape=jax.ShapeDtypeStruct((n, LANES), F32),
        compiler_params=_cparams(("parallel",)),
        name="gdn_gates",
    )(ab, pad(a_log), pad(dt_bias))


def _gdn_feat_kernel(x_ref, prev_ref, next_ref, w_ref, o_ref, xs_ref, *, nct, n_tiles, n_norm, n_q, q_scale, width):
    m = pl.program_id(0)
    j = pl.program_id(1)
    tm, tn = x_ref.shape
    pad = width // 2
    prev_ok = jnp.logical_and(m != 0, m != nct)
    next_ok = jnp.logical_and(m != nct - 1, m != n_tiles - 1)
    xs_ref[0:HALO, :] = jnp.where(prev_ok, prev_ref[...].astype(F32), 0.0)
    xs_ref[HALO:HALO + tm, :] = x_ref[...].astype(F32)
    xs_ref[HALO + tm:, :] = jnp.where(next_ok, next_ref[...].astype(F32), 0.0)
    acc = jnp.zeros((tm, tn), F32)
    for t in range(width):
        acc = acc + w_ref[t:t + 1, :] * xs_ref[HALO - pad + t:HALO - pad + t + tm, :]
    y = _silu(acc)

    @pl.when(j >= n_norm)
    def _():
        o_ref[...] = y.astype(o_ref.dtype)

    @pl.when(j < n_norm)
    def _():
        scale = jnp.where(j < n_q, q_scale, 1.0).astype(F32)
        for hh in range(tn // GDN_HEAD_DIM):
            sl = slice(hh * GDN_HEAD_DIM, (hh + 1) * GDN_HEAD_DIM)
            yh = y[:, sl]
            inv = lax.rsqrt(jnp.sum(yh * yh, axis=-1, keepdims=True) + L2_EPS)
            o_ref[:, sl] = (yh * (inv * scale)).astype(o_ref.dtype)


def _gdn_features(p, conv_w, nct, qk_dim):
    n = p.shape[0]
    width, conv_dim = conv_w.shape
    tn = 512
    n_tiles = n // TM
    rb = TM // HALO
    last_rb = n // HALO - 1
    return pl.pallas_call(
        functools.partial(_gdn_feat_kernel, nct=nct, n_tiles=n_tiles, n_norm=2 * qk_dim // tn, n_q=qk_dim // tn,
                          q_scale=GDN_HEAD_DIM ** -0.5, width=width),
        grid=(n_tiles, conv_dim // tn),
        in_specs=[pl.BlockSpec((TM, tn), lambda m, j: (m, j)),
                  pl.BlockSpec((HALO, tn), lambda m, j: (jnp.maximum(m * rb - 1, 0), j)),
                  pl.BlockSpec((HALO, tn), lambda m, j: (jnp.minimum((m + 1) * rb, last_rb), j)),
                  pl.BlockSpec((width, tn), lambda m, j: (0, j))],
        out_specs=pl.BlockSpec((TM, tn), lambda m, j: (m, j)),
        out_shape=jax.ShapeDtypeStruct((n, conv_dim), BF16),
        scratch_shapes=[pltpu.VMEM((TM + 2 * HALO, tn), F32)],
        compiler_params=_cparams(("parallel", "parallel")),
        name="gdn_features",
    )(p, p, p, conv_w)


def _gdn_scan_kernel(*refs, chunk, groups):
    ins, o_refs, s_ref = refs[:16], refs[16:18], refs[18]
    t = pl.program_id(1)

    @pl.when(t == 0)
    def _():
        s_ref[...] = jnp.zeros_like(s_ref)

    rows = ins[0].shape[0]
    hd = GDN_HEAD_DIM
    n_chunk = rows // chunk
    n_double = int(math.log2(chunk)) - 1
    ri = lax.broadcasted_iota(jnp.int32, (rows, rows), 0)
    ci = lax.broadcasted_iota(jnp.int32, (rows, rows), 1)
    same = (ri // chunk) == (ci // chunk)
    eye = jnp.where(ri == ci, 1.0, 0.0).astype(F32)
    r64 = lax.broadcasted_iota(jnp.int32, (chunk, chunk), 0)
    c64 = lax.broadcasted_iota(jnp.int32, (chunk, chunk), 1)

    dirs = []
    for d in range(2):
        q_ref, k_ref, kt_ref, v_ref, scol_ref, srow_ref, srowc_ref, glrow_ref = ins[8 * d:8 * d + 8]
        for g in range(groups):
            k = k_ref[:, g * hd:(g + 1) * hd]
            dirs.append(dict(
                q=q_ref[:, g * hd:(g + 1) * hd], k=k, v=v_ref[:, 2 * g * hd:2 * (g + 1) * hd],
                kt=kt_ref[g * hd:(g + 1) * hd, :], srowc_ref=srowc_ref, glrow_ref=glrow_ref, g=g, o_ref=o_refs[d],
                scol=scol_ref[0, g],
                srow=srow_ref[0, g],
                incl=same & ((ri <= ci) if d else (ri >= ci)), strict=same & ((ri < ci) if d else (ri > ci)),
                incl64=(r64 <= c64) if d else (r64 >= c64),
                kk=_dot_nt(k, k),
                order=list(range(n_chunk - 1, -1, -1)) if d else list(range(n_chunk))))
    n_streams = len(dirs)
    chains = [(d, a) for d in range(n_streams) for a in range(2)]
    gcol = {c: dirs[c[0]]["scol"][:, c[1]:c[1] + 1] for c in chains}
    bcol = {c: dirs[c[0]]["scol"][:, 2 + c[1]:3 + c[1]] for c in chains}
    e_g = {c: jnp.exp(gcol[c]) for c in chains}
    e_k = {c: jnp.exp(dirs[c[0]]["scol"][:, 4 + c[1]:5 + c[1]] - gcol[c]) for c in chains}
    neg_l = {c: jnp.where(dirs[c[0]]["strict"],
                          -(bcol[c] * dirs[c[0]]["kk"]
                            * jnp.exp(jnp.where(dirs[c[0]]["incl"], gcol[c] - dirs[c[0]]["srow"][c[1]:c[1] + 1, :],
                                                -jnp.inf))), 0.0) for c in chains}
    inv = {c: eye + neg_l[c] for c in chains}
    pw = {c: neg_l[c].astype(BF16) for c in chains}
    for _ in range(n_double):
        pw = {c: _dot(pw[c], pw[c]).astype(BF16) for c in chains}
        inv = {c: inv[c] + _dot(inv[c].astype(BF16), pw[c]) for c in chains}
    rhs = {c: jnp.concatenate([dirs[c[0]]["v"][:, c[1] * hd:(c[1] + 1) * hd].astype(F32) * bcol[c],
                               dirs[c[0]]["k"].astype(F32) * (bcol[c] * e_g[c])], axis=1).astype(BF16) for c in chains}
    uw = {c: _dot(inv[c].astype(BF16), rhs[c]) for c in chains}

    state = [s_ref[d] for d in range(n_streams)]
    for i in range(n_chunk):
        js = [dirs[d]["order"][i] for d in range(n_streams)]
        rs = [slice(j * chunk, (j + 1) * chunk) for j in js]
        s_b = [st.astype(BF16) for st in state]
        qk_c = [_dot_nt(dirs[d]["q"][rs[d]], dirs[d]["k"][rs[d]]) for d in range(n_streams)]
        q_s = [_dot(dirs[d]["q"][rs[d]], s_b[d]) for d in range(n_streams)]
        v_new = {(d, a): uw[d, a][rs[d], :hd] - _dot(uw[d, a][rs[d], hd:].astype(BF16), s_b[d][:, a * hd:(a + 1) * hd])
                 for d, a in chains}
        qkd = {(d, a): (qk_c[d] * jnp.exp(jnp.where(dirs[d]["incl64"],
                                                    gcol[d, a][rs[d]]
                                                    - dirs[d]["srowc_ref"][0, dirs[d]["g"], js[d], a:a + 1, :],
                                                    -jnp.inf))).astype(BF16) for d, a in chains}
        for d, a in chains:
            o = e_g[d, a][rs[d]] * q_s[d][:, a * hd:(a + 1) * hd] + _dot(qkd[d, a], v_new[d, a].astype(BF16))
            col = (2 * dirs[d]["g"] + a) * hd
            dirs[d]["o_ref"][rs[d], col:col + hd] = o.astype(dirs[d]["o_ref"].dtype)
        for d in range(n_streams):
            v_s = jnp.concatenate([v_new[d, a] * e_k[d, a][rs[d]] for a in range(2)], axis=1).astype(BF16)
            parts = [jnp.zeros((js[d] * chunk, 2 * hd), BF16), v_s,
                     jnp.zeros(((n_chunk - 1 - js[d]) * chunk, 2 * hd), BF16)]
            v_pad = jnp.concatenate([p for p in parts if p.shape[0]], axis=0)
            e_l = jnp.exp(dirs[d]["glrow_ref"][0, dirs[d]["g"], js[d]])
            state[d] = state[d] * e_l + _dot(dirs[d]["kt"], v_pad)
    for d in range(n_streams):
        s_ref[d] = state[d]


def _transpose_kernel(x_ref, o_ref):
    o_ref[...] = x_ref[...].astype(F32).T.astype(o_ref.dtype)


def _transposed_keys(feat, qk_dim):
    n = feat.shape[0]
    tn = 512
    return pl.pallas_call(
        _transpose_kernel,
        grid=(n // TM, qk_dim // tn),
        in_specs=[pl.BlockSpec((TM, tn), lambda m, j: (m, qk_dim // tn + j))],
        out_specs=pl.BlockSpec((tn, TM), lambda m, j: (j, m)),
        out_shape=jax.ShapeDtypeStruct((qk_dim, n), feat.dtype),
        compiler_params=_cparams(("parallel", "parallel")),
        name="gdn_keys_t",
    )(feat)


def _gdn_scan(feat, kt, scol, srow, srowc, glrow, nct):
    n = feat.shape[0]
    hd = GDN_HEAD_DIM
    n_qk = kt.shape[0] // hd
    n_tiles = n // TM
    cpt = TM // GDN_CHUNK

    def tile(d, t):
        return jnp.where(t < nct, nct - 1 - t, n_tiles - 1 - (t - nct)) if d else t

    g = GDN_PAIRS_PER_STEP
    n_grp = n_qk // g
    assert n_grp * g == n_qk

    def specs(d):
        return [pl.BlockSpec((TM, g * hd), lambda h, t: (tile(d, t), h)),
                pl.BlockSpec((TM, g * hd), lambda h, t: (tile(d, t), n_grp + h)),
                pl.BlockSpec((g * hd, TM), lambda h, t: (h, tile(d, t))),
                pl.BlockSpec((TM, 2 * g * hd), lambda h, t: (tile(d, t), n_grp + h)),
                pl.BlockSpec((1, g, TM, 6), lambda h, t: (d, h, tile(d, t), 0)),
                pl.BlockSpec((1, g, 4, TM), lambda h, t: (d, h, 0, tile(d, t))),
                pl.BlockSpec((1, g, cpt, 4, GDN_CHUNK), lambda h, t: (d, h, tile(d, t), 0, 0)),
                pl.BlockSpec((1, g, cpt, 1, 2 * hd), lambda h, t: (d, h, tile(d, t), 0, 0))]

    args = [feat, feat, kt, feat, scol, srow, srowc, glrow]
    out = jax.ShapeDtypeStruct((n, 2 * n_qk * hd), BF16)
    return pl.pallas_call(
        functools.partial(_gdn_scan_kernel, chunk=GDN_CHUNK, groups=g),
        grid=(n_grp, n_tiles),
        in_specs=specs(0) + specs(1),
        out_specs=[pl.BlockSpec((TM, 2 * g * hd), lambda h, t: (tile(0, t), h)),
                   pl.BlockSpec((TM, 2 * g * hd), lambda h, t: (tile(1, t), h))],
        out_shape=[out, out],
        scratch_shapes=[pltpu.VMEM((2 * g, hd, 2 * hd), F32)],
        compiler_params=_cparams(("parallel", "arbitrary")),
        name="gdn_scan",
    )(*args, *args)


def _post_mixer(y, h_ref, mod_ref, lnw_ref, lnb_ref, rw_ref, rb_ref, h1_ref, v_ref, rt_ref, cnt_ref, carry_ref,
                alpha, n_e):
    @pl.when(pl.program_id(0) == 0)
    def _():
        carry_ref[...] = jnp.zeros_like(carry_ref)

    x = alpha * h_ref[...] + mod_ref[0, 2:3, :] * y
    h1 = _layer_norm(x, lnw_ref[...], lnb_ref[...])
    h1_ref[...] = h1
    v = _modulate(h1, mod_ref, 3, 4)
    v_ref[...] = _pack_bf16_pairs(v)
    logits = jnp.dot(v, rw_ref[...], preferred_element_type=F32, precision=HIGHEST) + rb_ref[...]

    tm = logits.shape[0]
    lane = lax.broadcasted_iota(jnp.int32, logits.shape, 1)
    rest = jnp.where(lane < n_e, logits, -jnp.inf)
    top_val, top_idx, picked = [], [], []
    for _ in range(TOP_K):
        mx = jnp.max(rest, axis=-1, keepdims=True)
        idx = jnp.min(jnp.where(rest == mx, lane, LANES), axis=-1, keepdims=True)
        hit = lane == idx
        top_val.append(mx)
        top_idx.append(idx)
        picked.append(hit)
        rest = jnp.where(hit, -jnp.inf, rest)
    e = [jnp.exp(tv - top_val[0]) for tv in top_val]
    denom = functools.reduce(lambda a, b: a + b, e)
    onehot = functools.reduce(lambda a, b: a + b, [jnp.where(hit, 1.0, 0.0) for hit in picked])
    r = lax.broadcasted_iota(jnp.int32, (tm, tm), 0)
    c = lax.broadcasted_iota(jnp.int32, (tm, tm), 1)
    earlier = jnp.where(r > c, 1.0, 0.0).astype(BF16)
    before = _dot(earlier, onehot.astype(BF16)) + carry_ref[0:1, :]
    rt = jnp.zeros(logits.shape, F32)
    for k in range(TOP_K):
        rank = jnp.sum(jnp.where(picked[k], before, 0.0), axis=-1, keepdims=True)
        rt = jnp.where(lane == k, top_idx[k].astype(F32), rt)
        rt = jnp.where(lane == TOP_K + k, rank, rt)
        rt = jnp.where(lane == 2 * TOP_K + k, e[k] / denom, rt)
    rt_ref[...] = rt
    total = carry_ref[0:1, :] + jnp.sum(onehot, axis=0, keepdims=True)
    carry_ref[...] = jnp.broadcast_to(total, carry_ref.shape)
    cnt_ref[...] = jnp.broadcast_to(total, cnt_ref.shape)


def _gdn_out_kernel(of_ref, ob_ref, z_ref, nw_ref, w_ref, h_ref, mod_ref, lnw_ref, lnb_ref, rw_ref, rb_ref,
                    h1_ref, v_ref, rt_ref, cnt_ref, carry_ref, a_ref, *, alpha, n_e):
    hd = GDN_HEAD_DIM
    for hh in range(of_ref.shape[1] // hd):
        sl = slice(hh * hd, (hh + 1) * hd)
        o = of_ref[:, sl].astype(F32) + ob_ref[:, sl].astype(F32)
        o = o * lax.rsqrt(jnp.mean(o * o, axis=-1, keepdims=True) + NORM_EPS) * nw_ref[...]
        a_ref[:, sl] = (o * _silu(z_ref[:, sl].astype(F32))).astype(BF16)
    y = _dot(a_ref[...], w_ref[...])
    _post_mixer(y, h_ref, mod_ref, lnw_ref, lnb_ref, rw_ref, rb_ref, h1_ref, v_ref, rt_ref, cnt_ref, carry_ref,
                alpha, n_e)


def _attn_out_kernel(a_ref, w_ref, h_ref, mod_ref, lnw_ref, lnb_ref, rw_ref, rb_ref, h1_ref, v_ref, rt_ref, cnt_ref,
                     carry_ref, *, alpha, n_e):
    y = _dot(a_ref[...], w_ref[...])
    _post_mixer(y, h_ref, mod_ref, lnw_ref, lnb_ref, rw_ref, rb_ref, h1_ref, v_ref, rt_ref, cnt_ref, carry_ref,
                alpha, n_e)


def _mixer_out(kind, acts, w_out, h, mod, nct, ln_w, ln_b, router_w, router_b, alpha):
    n, d = h.shape
    row = lambda width: pl.BlockSpec((TM, width), lambda m: (m, 0))
    full = lambda arr: pl.BlockSpec(arr.shape, lambda m: (0,) * arr.ndim)
    n_e = router_w.shape[1]
    rw = jnp.zeros((d, LANES), F32).at[:, :n_e].set(router_w)
    rb = jnp.zeros((1, LANES), F32).at[0, :n_e].set(router_b)
    tail = [w_out, h, mod, ln_w.reshape(1, d), ln_b.reshape(1, d), rw, rb]
    tail_specs = [full(w_out), row(d), pl.BlockSpec((1, 6, d), lambda m: (jnp.where(m < nct, 0, 1), 0, 0)),
                  pl.BlockSpec((1, d), lambda m: (0, 0)), pl.BlockSpec((1, d), lambda m: (0, 0)), full(rw), full(rb)]
    if kind == "gdn":
        o_f, o_b, p, norm_w, z_col = acts
        v_dim = o_f.shape[1]
        body = functools.partial(_gdn_out_kernel, alpha=alpha, n_e=n_e)
        args = [o_f, o_b, p, norm_w.reshape(1, -1)] + tail
        specs = [row(v_dim), row(v_dim), pl.BlockSpec((TM, v_dim), lambda m: (m, z_col)),
                 pl.BlockSpec((1, norm_w.shape[0]), lambda m: (0, 0))] + tail_specs
        scratch = [pltpu.VMEM((SUBLANES, LANES), F32), pltpu.VMEM((TM, v_dim), BF16)]
    else:
        (a,) = acts
        body = functools.partial(_attn_out_kernel, alpha=alpha, n_e=n_e)
        args = [a] + tail
        specs = [row(a.shape[1])] + tail_specs
        scratch = [pltpu.VMEM((SUBLANES, LANES), F32)]
    return pl.pallas_call(
        body,
        grid=(n // TM,),
        in_specs=specs,
        out_specs=[row(d), row(d // 2), row(LANES), pl.BlockSpec((SUBLANES, LANES), lambda m: (0, 0))],
        out_shape=[jax.ShapeDtypeStruct((n, d), F32), jax.ShapeDtypeStruct((n, d // 2), jnp.uint32),
                   jax.ShapeDtypeStruct((n, LANES), F32), jax.ShapeDtypeStruct((SUBLANES, LANES), F32)],
        scratch_shapes=scratch,
        compiler_params=_cparams(("arbitrary",)),
        name=kind + "_out_ln",
    )(*args)


def _attn_kernel(q_ref, k_ref, vt_ref, lam_ref, sw_ref, o_ref, acc_ref, m_ref, l_ref, sa_ref, sb_ref,
                 *, tk, nct, l_ctx, n_pairs, lambda_init):
    mt = pl.program_id(1)
    q = q_ref[...]
    lane = lax.broadcasted_iota(jnp.int32, q.shape, 1)
    zero = jnp.zeros_like(q)
    q_maps = (jnp.where(lane < DIFF_HEAD_DIM, q, zero), jnp.where(lane >= DIFF_HEAD_DIM, q, zero))
    kb = vt_ref.shape[3]
    acc_ref[...] = jnp.zeros_like(acc_ref)
    m_ref[...] = jnp.full_like(m_ref, -jnp.inf)
    l_ref[...] = jnp.zeros_like(l_ref)

    def scores(start, s_ref):
        kj = k_ref[pl.ds(start, tk), :]
        for i, qm in enumerate(q_maps):
            s_ref[i] = _dot_nt(kj, qm)

    def absorb(score_of_map, block0, n_blocks):
        for i in range(2):
            s = score_of_map(i)
            m_old = m_ref[i]
            m_new = jnp.maximum(m_old, jnp.max(s, axis=0, keepdims=True))
            alpha = jnp.exp2(m_old - m_new)
            p = jnp.exp2(s - m_new)
            l_ref[i] = alpha * l_ref[i] + jnp.sum(p, axis=0, keepdims=True)
            p = p.astype(BF16)
            pv = _dot(vt_ref[0, block0], p[0:kb])
            for b in range(1, n_blocks):
                pv = pv + _dot(vt_ref[0, block0 + b], p[b * kb:(b + 1) * kb])
            acc_ref[i] = alpha * acc_ref[i] + pv
            m_ref[i] = m_new

    k_ctx = k_ref[0:l_ctx, :]
    absorb(lambda i: _dot_nt(k_ctx, q_maps[i]), 0, l_ctx // kb)

    @pl.when(mt >= nct)
    def _():
        last = l_ctx + (2 * n_pairs - 1) * tk
        scores(l_ctx, sa_ref)

        def body(jj, carry):
            c0 = pl.multiple_of(l_ctx + 2 * jj * tk, LANES)
            c1 = pl.multiple_of(c0 + tk, LANES)
            c2 = pl.multiple_of(jnp.minimum(c1 + tk, last), LANES)
            scores(c1, sb_ref)
            absorb(lambda i: sa_ref[i], c0 // kb, tk // kb)
            scores(c2, sa_ref)
            absorb(lambda i: sb_ref[i], c1 // kb, tk // kb)
            return carry

        lax.fori_loop(0, n_pairs, body, 0)

    lam = lam_ref[...]
    lam_full = (jnp.exp(jnp.sum(lam[0:1] * lam[1:2], axis=-1, keepdims=True))
                - jnp.exp(jnp.sum(lam[2:3] * lam[3:4], axis=-1, keepdims=True)) + lambda_init)
    o = acc_ref[0] / l_ref[0] - lam_full * (acc_ref[1] / l_ref[1])
    o = o * lax.rsqrt(jnp.mean(o * o, axis=0, keepdims=True) + NORM_EPS) * sw_ref[...]
    o_ref[...] = (o * (1.0 - lambda_init)).T.astype(o_ref.dtype)


def _transpose_heads_kernel(x_ref, o_ref):
    hd = o_ref.shape[2]
    for hh in range(o_ref.shape[0]):
        o_ref[hh, 0] = x_ref[:, hh * hd:(hh + 1) * hd].astype(F32).T.astype(o_ref.dtype)


def _transposed_values(qkv, vd):
    n = qkv.shape[0]
    d = qkv.shape[1] // 3
    tn = 512
    return pl.pallas_call(
        _transpose_heads_kernel,
        grid=(n // TM, d // tn),
        in_specs=[pl.BlockSpec((TM, tn), lambda m, j: (m, 2 * d // tn + j))],
        out_specs=pl.BlockSpec((tn // vd, 1, vd, TM), lambda m, j: (j, m, 0, 0)),
        out_shape=jax.ShapeDtypeStruct((d // vd, n // TM, vd, TM), qkv.dtype),
        compiler_params=_cparams(("parallel", "parallel")),
        name="attn_values_t",
    )(qkv)


def _diff_attention(qkv, lam, subln_w, nct, l_ctx, lambda_init):
    n = qkv.shape[0]
    d = qkv.shape[1] // 3
    vd = subln_w.shape[0]
    n_heads = d // vd
    l_lat = n - l_ctx
    tk = math.gcd(l_lat // 2, ATTN_TK)
    assert tk % TM == 0
    vt = _transposed_values(qkv, vd)
    return pl.pallas_call(
        functools.partial(_attn_kernel, tk=tk, nct=nct, l_ctx=l_ctx, n_pairs=l_lat // (2 * tk),
                          lambda_init=lambda_init),
        grid=(n_heads, n // TM),
        in_specs=[pl.BlockSpec((TM, vd), lambda h, m: (m, h)),
                  pl.BlockSpec((n, vd), lambda h, m: (0, n_heads + h)),
                  pl.BlockSpec((1, n // TM, vd, TM), lambda h, m: (h, 0, 0, 0)),
                  pl.BlockSpec(lam.shape, lambda h, m: (0, 0)),
                  pl.BlockSpec((vd, 1), lambda h, m: (0, 0))],
        out_specs=pl.BlockSpec((TM, vd), lambda h, m: (m, h)),
        out_shape=jax.ShapeDtypeStruct((n, d), BF16),
        scratch_shapes=[pltpu.VMEM((2, vd, TM), F32), pltpu.VMEM((2, 1, TM), F32), pltpu.VMEM((2, 1, TM), F32),
                        pltpu.VMEM((2, tk, TM), F32), pltpu.VMEM((2, tk, TM), F32)],
        compiler_params=_cparams(("parallel", "parallel")),
        name="diff_attn",
    )(qkv, qkv, vt, lam, subln_w.reshape(vd, 1))


def _moe_kernel(be_ref, nb_ref, x_ref, wgu_ref, bgu_ref, wdn_ref, bdn_ref, y_ref, wgu_s, wdn_s, *, d_ff):
    b = pl.program_id(0)
    changed = jnp.logical_or(b == 0, be_ref[b] != be_ref[jnp.maximum(b - 1, 0)])

    @pl.when(changed)
    def _():
        wgu_s[...] = wgu_ref[0, 0].astype(BF16)
        wdn_s[...] = wdn_ref[0, 0].astype(BF16)

    @pl.when(b < nb_ref[0])
    def _():
        h = _dot(_unpack_bf16_pairs(x_ref[...]), wgu_s[...]) + bgu_ref[0, 0]
        gate = jnp.minimum(h[:, :d_ff], SWIGLU_LIMIT)
        up = jnp.clip(h[:, d_ff:], -SWIGLU_LIMIT, SWIGLU_LIMIT)
        glu = gate * jax.nn.sigmoid(gate * SWIGLU_ALPHA)
        act = ((up + 1.0) * glu).astype(BF16)
        y_ref[...] = (_dot(act, wdn_s[...]) + bdn_ref[0, 0]).astype(y_ref.dtype)

    @pl.when(b >= nb_ref[0])
    def _():
        y_ref[...] = jnp.zeros_like(y_ref)


def _moe_experts(x_sorted, block_e, n_used, layer, w_gu, b_gu, w_dn, b_dn):
    n_rows = x_sorted.shape[0]
    depth, n_e, d, f2 = w_gu.shape
    grid_spec = pltpu.PrefetchScalarGridSpec(
        num_scalar_prefetch=2,
        grid=(n_rows // MOE_BM,),
        in_specs=[pl.BlockSpec((MOE_BM, d // 2), lambda b, be, nb: (b, 0)),
                  pl.BlockSpec((1, 1, d, f2), lambda b, be, nb: (layer, be[b], 0, 0)),
                  pl.BlockSpec((1, 1, 1, f2), lambda b, be, nb: (layer, be[b], 0, 0)),
                  pl.BlockSpec((1, 1, f2 // 2, d), lambda b, be, nb: (layer, be[b], 0, 0)),
                  pl.BlockSpec((1, 1, 1, d), lambda b, be, nb: (layer, be[b], 0, 0))],
        out_specs=pl.BlockSpec((MOE_BM, d), lambda b, be, nb: (b, 0)),
        scratch_shapes=[pltpu.VMEM((d, f2), BF16), pltpu.VMEM((f2 // 2, d), BF16)],
    )
    return pl.pallas_call(
        functools.partial(_moe_kernel, d_ff=f2 // 2),
        grid_spec=grid_spec,
        out_shape=jax.ShapeDtypeStruct((n_rows, d), BF16),
        compiler_params=_cparams(("arbitrary",)),
        name="moe_experts",
    )(block_e, n_used, x_sorted, w_gu, b_gu.reshape(depth, n_e, 1, f2), w_dn, b_dn.reshape(depth, n_e, 1, d))


def _moe_combine_kernel(y0_ref, y1_ref, y2_ref, y3_ref, g_ref, h_ref, mod_ref, lnw_ref, lnb_ref, o_ref, *, alpha):
    g = g_ref[...]
    f = jnp.zeros(h_ref.shape, F32)
    for i, y_ref in enumerate((y0_ref, y1_ref, y2_ref, y3_ref)):
        f = f + g[:, i:i + 1] * y_ref[...].astype(F32)
    x = alpha * h_ref[...] + mod_ref[0, 5:6, :] * f
    o_ref[...] = _layer_norm(x, lnw_ref[...], lnb_ref[...])


def _moe_combine(ys, gates, h1, mod, nct, ln_w, ln_b, alpha):
    n, d = h1.shape
    row = lambda width: pl.BlockSpec((TM, width), lambda m: (m, 0))
    return pl.pallas_call(
        functools.partial(_moe_combine_kernel, alpha=alpha),
        grid=(n // TM,),
        in_specs=[row(d)] * TOP_K + [row(TOP_K), row(d),
                                     pl.BlockSpec((1, 6, d), lambda m: (jnp.where(m < nct, 0, 1), 0, 0)),
                                     pl.BlockSpec((1, d), lambda m: (0, 0)), pl.BlockSpec((1, d), lambda m: (0, 0))],
        out_specs=row(d),
        out_shape=jax.ShapeDtypeStruct((n, d), F32),
        compiler_params=_cparams(("parallel",)),
        name="moe_combine_ln",
    )(*ys, gates, h1, mod, ln_w.reshape(1, d), ln_b.reshape(1, d))


def _sc_gather_rows(table, idx):
    n_rows = idx.shape[0]
    d = table.shape[1]
    n_workers = SC_CORES * SC_SUBCORES
    per_worker = n_rows // n_workers
    step = SC_GATHER_ROWS
    assert per_worker * n_workers == n_rows and per_worker % (2 * step) == 0
    mesh = plsc.VectorSubcoreMesh(core_axis_name="c", subcore_axis_name="s",
                                  num_cores=SC_CORES, num_subcores=SC_SUBCORES)

    def body(table_hbm, idx_hbm, out_hbm, idx_a, idx_b, rows_a, rows_b, sem_a, sem_b):
        base = (lax.axis_index("s") * SC_CORES + lax.axis_index("c")) * per_worker

        @pl.loop(0, per_worker // (2 * step))
        def _(i):
            off_a = pl.multiple_of(base + 2 * i * step, SUBLANES)
            off_b = pl.multiple_of(off_a + step, SUBLANES)
            pltpu.sync_copy(idx_hbm.at[pl.ds(off_a, step)], idx_a)
            pltpu.sync_copy(idx_hbm.at[pl.ds(off_b, step)], idx_b)
            copy_a = pltpu.async_copy(table_hbm.at[idx_a], rows_a, sem_a)
            copy_b = pltpu.async_copy(table_hbm.at[idx_b], rows_b, sem_b)
            copy_a.wait()
            pltpu.sync_copy(rows_a, out_hbm.at[pl.ds(off_a, step)])
            copy_b.wait()
            pltpu.sync_copy(rows_b, out_hbm.at[pl.ds(off_b, step)])

    return pl.kernel(
        body,
        out_type=jax.ShapeDtypeStruct((n_rows, d), table.dtype),
        mesh=mesh,
        scratch_types=[pltpu.VMEM((step,), jnp.int32), pltpu.VMEM((step,), jnp.int32),
                       pltpu.VMEM((step, d), table.dtype), pltpu.VMEM((step, d), table.dtype),
                       pltpu.SemaphoreType.DMA, pltpu.SemaphoreType.DMA],
        name="sc_gather_rows",
    )(table, idx)


def _moe(v_rows, route, counts, n_e, h1, mod, nct, ln_w, ln_b, alpha, layer, w_gu, b_gu, w_dn, b_dn):
    n = v_rows.shape[0]
    top_idx = route[:, :TOP_K].astype(jnp.int32)
    rank = route[:, TOP_K:2 * TOP_K].astype(jnp.int32)
    gates = route[:, 2 * TOP_K:3 * TOP_K]
    counts = counts[0, :n_e].astype(jnp.int32)
    nk = n * TOP_K
    padded = (counts + MOE_BM - 1) // MOE_BM * MOE_BM
    pad_end = jnp.cumsum(padded)
    pad_start = pad_end - padded
    dest = (pad_start[top_idx] + rank).reshape(nk)
    n_blocks = -(-nk // MOE_BM) + n_e
    rows_unit = 2 * SC_GATHER_ROWS * SC_CORES * SC_SUBCORES
    n_rows = -(-n_blocks * MOE_BM // rows_unit) * rows_unit
    n_blocks = n_rows // MOE_BM
    row_tok = jnp.zeros((n_rows,), jnp.int32).at[dest].set(jnp.arange(nk, dtype=jnp.int32) // TOP_K)
    block_row0 = jnp.arange(n_blocks, dtype=jnp.int32) * MOE_BM
    block_e = jnp.minimum(jnp.sum(pad_end[None, :] <= block_row0[:, None], axis=1, dtype=jnp.int32), n_e - 1)
    n_used = (pad_end[-1:] // MOE_BM).astype(jnp.int32)
    x_sorted = _sc_gather_rows(v_rows, row_tok)
    yb = _moe_experts(x_sorted, block_e, n_used, layer, w_gu, b_gu, w_dn, b_dn)
    dest2 = dest.reshape(n, TOP_K)
    ys = [yb[dest2[:, i]] for i in range(TOP_K)]
    return _moe_combine(ys, gates, h1, mod, nct, ln_w, ln_b, alpha)


def _rope_tables(l_ctx, l_lat):
    rows = l_lat // GRID_W
    row = jnp.repeat(jnp.arange(rows, dtype=F32), GRID_W)
    col = jnp.tile(jnp.arange(GRID_W, dtype=F32), rows)
    axis_dim = DIFF_HEAD_DIM // 2
    inv_freq = ROPE_THETA ** (-jnp.arange(0, axis_dim, 2, dtype=F32) / axis_dim)
    ang_r = row[:, None] * inv_freq
    ang_c = col[:, None] * inv_freq
    ang = jnp.concatenate([ang_r, ang_r, ang_c, ang_c], -1)
    ang = jnp.concatenate([jnp.zeros((l_ctx, DIFF_HEAD_DIM), F32), ang], 0)
    ang = jnp.concatenate([ang, ang], -1)
    return jnp.cos(ang), jnp.sin(ang)


def _gdn_scalars(gates, n_vh):
    n = gates.shape[0]
    n_qk = n_vh // 2
    nc = n // GDN_CHUNK
    gc = gates[:, :2 * n_vh].reshape(n, 2, n_qk, 2)
    beta = gates[:, 2 * n_vh:4 * n_vh].reshape(n, 2, n_qk, 2)
    gcc = gc.reshape(nc, GDN_CHUNK, 2, n_qk, 2)
    gl = jnp.stack([gcc[:, -1, 0], gcc[:, 0, 1]], axis=1)
    gl_tok = jnp.repeat(gl, GDN_CHUNK, axis=0)
    col = jnp.concatenate([gc, beta, gl_tok], axis=-1)
    scol = jnp.transpose(col, (1, 2, 0, 3))
    srow = jnp.transpose(col[..., :4], (1, 2, 3, 0))
    srowc = jnp.transpose(col[..., :4].reshape(nc, GDN_CHUNK, 2, n_qk, 4), (2, 3, 0, 4, 1))
    glrow = jnp.broadcast_to(jnp.transpose(gl, (1, 2, 0, 3))[..., None], (2, n_qk, nc, 2, GDN_HEAD_DIM))
    glrow = glrow.reshape(2, n_qk, nc, 1, 2 * GDN_HEAD_DIM)
    return scol, srow, srowc, glrow


def kernel(x, c, ctx, c_ctx, ada_w, ada_b, ln_w, ln_b, gdn_w_in, gdn_conv_w, gdn_a_log, gdn_dt_bias, gdn_norm_w,
           gdn_w_out, diff_w_in, diff_lambda, diff_subln_w, diff_w_out, router_w, router_b, moe_w_gate_up,
           moe_b_gate_up, moe_w_down, moe_b_down):
    batch, l_lat, d = x.shape
    l_ctx = ctx.shape[1]
    depth = ada_w.shape[0]
    assert batch == 1 and l_ctx % TM == 0 and l_lat % TM == 0 and l_lat % GRID_W == 0
    nct = l_ctx // TM
    n_e = router_w.shape[-1]
    alpha = (2 * depth) ** 0.25
    n_vh = gdn_a_log.shape[-1]
    v_dim = n_vh * GDN_HEAD_DIM
    conv_dim = gdn_conv_w.shape[-1]
    qk_dim = (conv_dim - v_dim) // 2
    n_qk = qk_dim // GDN_HEAD_DIM

    h = jnp.concatenate([ctx[0], x[0]], axis=0)
    n = h.shape[0]
    mods = _modulations(c_ctx, c, ada_w, ada_b)
    cos, sin = _rope_tables(l_ctx, l_lat)

    for i in range(depth):
        j = i // 2
        mod = mods[i]
        if i % 2 == 0:
            w_in = gdn_w_in[j]
            w_main = w_in[:, :conv_dim + v_dim].astype(BF16)
            w_ab = jnp.zeros((d, LANES), F32).at[:, :4 * n_vh].set(w_in[:, conv_dim + v_dim:]).astype(BF16)
            p, ab = _project(h, mod, nct, [w_main, w_ab], [BF16, F32])
            gates = _gdn_gates(ab, gdn_a_log[j], gdn_dt_bias[j])
            feat = _gdn_features(p, gdn_conv_w[j], nct, qk_dim)
            kt = _transposed_keys(feat, qk_dim)
            scol, srow, srowc, glrow = _gdn_scalars(gates, n_vh)
            o_f, o_b = _gdn_scan(feat, kt, scol, srow, srowc, glrow, nct)
            acts = (o_f, o_b, p, gdn_norm_w[j], conv_dim // v_dim)
            h1, v, route, counts = _mixer_out("gdn", acts, gdn_w_out[j].astype(BF16), h, mod, nct, ln_w[i, 0], ln_b[i, 0],
                                       router_w[i], router_b[i], alpha)
        else:
            lambda_init = 0.8 - 0.6 * math.exp(-0.3 * i)
            qkv = _qkv_rope(h, mod, nct, diff_w_in[j].astype(BF16), cos, sin)
            a = _diff_attention(qkv, diff_lambda[j], diff_subln_w[j], nct, l_ctx, lambda_init)
            h1, v, route, counts = _mixer_out("attn", (a,), diff_w_out[j].astype(BF16), h, mod, nct, ln_w[i, 0], ln_b[i, 0],
                                       router_w[i], router_b[i], alpha)
        h = _moe(v, route, counts, n_e, h1, mod, nct, ln_w[i, 1], ln_b[i, 1], alpha,
                 i, moe_w_gate_up, moe_b_gate_up, moe_w_down, moe_b_down)
    return h[l_ctx:].reshape(batch, l_lat, d)
```

```python
import functools
import math

import jax
import jax.numpy as jnp
from jax import lax
from jax.experimental import pallas as pl
from jax.experimental.pallas import tpu as pltpu
from jax.experimental.pallas import tpu_sc as plsc

F32 = jnp.float32
BF16 = jnp.bfloat16
HIGHEST = lax.Precision.HIGHEST

GRID_W = 64
GDN_HEAD_DIM = 128
GDN_CHUNK = 64
DIFF_HEAD_DIM = 64
ROPE_THETA = 10000.0
TOP_K = 4
SWIGLU_LIMIT = 7.0
SWIGLU_ALPHA = 1.702
NORM_EPS = 1e-5
L2_EPS = 1e-6

LANES = 128
SUBLANES = 8
HALO = 16
VMEM_LIMIT = 56 * 1024 * 1024

TM = 256
MOE_BM = 256
ATTN_TK = 2048
GDN_PAIRS_PER_STEP = 4

SC_CORES = 2
SC_SUBCORES = 16
SC_GATHER_ROWS = 32


def _cparams(sem):
    return pltpu.CompilerParams(dimension_semantics=sem, vmem_limit_bytes=VMEM_LIMIT)


def _silu(x):
    return x * jax.nn.sigmoid(x)


def _dot(a, b):
    return jnp.dot(a, b, preferred_element_type=F32)


def _dot_nt(a, b):
    return lax.dot_general(a, b, (((1,), (1,)), ((), ())), preferred_element_type=F32)


def _modulate(h, mod_ref, shift, scale):
    return h * (1.0 + mod_ref[0, scale:scale + 1, :]) + mod_ref[0, shift:shift + 1, :]


def _pack_bf16_pairs(x):
    bits = pltpu.bitcast(x.astype(BF16).astype(F32), jnp.uint32)
    half = x.shape[1] // 2
    return (bits[:, half:] & jnp.uint32(0xFFFF0000)) | (bits[:, :half] >> 16)


def _unpack_bf16_pairs(w):
    lo = pltpu.bitcast(w << 16, F32)
    hi = pltpu.bitcast(w & jnp.uint32(0xFFFF0000), F32)
    return jnp.concatenate([lo, hi], axis=1).astype(BF16)


def _layer_norm(x, w, b):
    mu = jnp.mean(x, axis=-1, keepdims=True)
    xc = x - mu
    var = jnp.mean(xc * xc, axis=-1, keepdims=True)
    return xc * lax.rsqrt(var + NORM_EPS) * w + b


def _mod_kernel(s_ref, w_ref, b_ref, o_ref):
    s = _silu(s_ref[...])
    o_ref[0] = jnp.dot(s, w_ref[0], preferred_element_type=F32, precision=HIGHEST) + b_ref[0]


def _modulations(c_ctx, c, ada_w, ada_b):
    depth, d, d6 = ada_w.shape
    tn = d6 // 4
    s = jnp.zeros((SUBLANES, d), F32).at[0].set(c_ctx).at[1].set(c[0])
    out = pl.pallas_call(
        _mod_kernel,
        grid=(depth, d6 // tn),
        in_specs=[pl.BlockSpec((SUBLANES, d), lambda i, n: (0, 0)),
                  pl.BlockSpec((1, d, tn), lambda i, n: (i, 0, n)),
                  pl.BlockSpec((1, 1, tn), lambda i, n: (i, 0, n))],
        out_specs=pl.BlockSpec((1, SUBLANES, tn), lambda i, n: (i, 0, n)),
        out_shape=jax.ShapeDtypeStruct((depth, SUBLANES, d6), F32),
        compiler_params=_cparams(("parallel", "parallel")),
        name="adaln_mod",
    )(s, ada_w, ada_b.reshape(depth, 1, d6))
    return out[:, :2].reshape(depth, 2, 6, d)


def _proj_kernel(h_ref, mod_ref, *refs, n_w, chunk):
    w_refs, o_refs = refs[:n_w], refs[n_w:]
    u = _modulate(h_ref[...], mod_ref, 0, 1).astype(BF16)
    for w_ref, o_ref in zip(w_refs, o_refs):
        n = w_ref.shape[1]
        for j in range(0, n, chunk):
            jc = min(chunk, n - j)
            o_ref[:, j:j + jc] = _dot(u, w_ref[:, j:j + jc]).astype(o_ref.dtype)


def _project(h, mod, nct, weights, out_dtypes):
    n, d = h.shape
    return pl.pallas_call(
        functools.partial(_proj_kernel, n_w=len(weights), chunk=512),
        grid=(n // TM,),
        in_specs=[pl.BlockSpec((TM, d), lambda m: (m, 0)),
                  pl.BlockSpec((1, 6, d), lambda m: (jnp.where(m < nct, 0, 1), 0, 0))]
                 + [pl.BlockSpec(w.shape, lambda m: (0, 0)) for w in weights],
        out_specs=[pl.BlockSpec((TM, w.shape[1]), lambda m: (m, 0)) for w in weights],
        out_shape=[jax.ShapeDtypeStruct((n, w.shape[1]), dt) for w, dt in zip(weights, out_dtypes)],
        compiler_params=_cparams(("parallel",)),
        name="mod_proj",
    )(h, mod, *weights)


def _qkv_rope_kernel(h_ref, mod_ref, w_ref, cos_ref, sin_ref, o_ref, *, chunk, n_rope, n_q, q_scale):
    u = _modulate(h_ref[...], mod_ref, 0, 1).astype(BF16)
    rep = chunk // LANES
    cos = jnp.concatenate([cos_ref[...]] * rep, axis=1)
    sin = jnp.concatenate([sin_ref[...]] * rep, axis=1)
    lane = lax.broadcasted_iota(jnp.int32, (h_ref.shape[0], chunk), 1)
    first = (lane % 32) < 16
    for j in range(w_ref.shape[1] // chunk):
        y = _dot(u, w_ref[:, j * chunk:(j + 1) * chunk])
        if j < n_rope:
            rot = jnp.where(first, -pltpu.roll(y, chunk - 16, 1), pltpu.roll(y, 16, 1))
            y = y * cos + rot * sin
            if j < n_q:
                y = y * q_scale
        o_ref[:, j * chunk:(j + 1) * chunk] = y.astype(o_ref.dtype)


def _qkv_rope(h, mod, nct, w, cos, sin):
    n, d = h.shape
    chunk = 512
    return pl.pallas_call(
        functools.partial(_qkv_rope_kernel, chunk=chunk, n_rope=2 * d // chunk, n_q=d // chunk,
                          q_scale=DIFF_HEAD_DIM ** -0.5 * math.log2(math.e)),
        grid=(n // TM,),
        in_specs=[pl.BlockSpec((TM, d), lambda m: (m, 0)),
                  pl.BlockSpec((1, 6, d), lambda m: (jnp.where(m < nct, 0, 1), 0, 0)),
                  pl.BlockSpec(w.shape, lambda m: (0, 0)),
                  pl.BlockSpec((TM, LANES), lambda m: (m, 0)),
                  pl.BlockSpec((TM, LANES), lambda m: (m, 0))],
        out_specs=pl.BlockSpec((TM, w.shape[1]), lambda m: (m, 0)),
        out_shape=jax.ShapeDtypeStruct((n, w.shape[1]), BF16),
        compiler_params=_cparams(("parallel",)),
        name="qkv_rope",
    )(h, mod, w, cos, sin)


def _gdn_gates_kernel(ab_ref, alog_ref, dtb_ref, o_ref, *, chunk, n_heads):
    x = ab_ref[...]
    t = x + dtb_ref[...]
    softplus = jnp.maximum(t, 0.0) + jnp.log1p(jnp.exp(-jnp.abs(t)))
    g = -jnp.exp(alog_ref[...]) * softplus
    beta = jax.nn.sigmoid(x)
    tm = x.shape[0]
    r = lax.broadcasted_iota(jnp.int32, (tm, tm), 0)
    c = lax.broadcasted_iota(jnp.int32, (tm, tm), 1)
    same = (r // chunk) == (c // chunk)
    t_fwd = jnp.where(same & (r >= c), 1.0, 0.0).astype(F32)
    t_bwd = jnp.where(same & (r <= c), 1.0, 0.0).astype(F32)
    g_fwd = jnp.dot(t_fwd, g, preferred_element_type=F32, precision=HIGHEST)
    g_bwd = jnp.dot(t_bwd, g, preferred_element_type=F32, precision=HIGHEST)
    lane = lax.broadcasted_iota(jnp.int32, x.shape, 1)
    o_ref[...] = jnp.where(lane < n_heads, g_fwd, jnp.where(lane < 2 * n_heads, g_bwd, beta))


def _gdn_gates(ab, a_log, dt_bias):
    n = ab.shape[0]
    nh = a_log.shape[-1]
    pad = lambda v: jnp.zeros((1, LANES), F32).at[0, :2 * nh].set(v.reshape(-1))
    return pl.pallas_call(
        functools.partial(_gdn_gates_kernel, chunk=GDN_CHUNK, n_heads=nh),
        grid=(n // TM,),
        in_specs=[pl.BlockSpec((TM, LANES), lambda m: (m, 0)),
                  pl.BlockSpec((1, LANES), lambda m: (0, 0)),
                  pl.BlockSpec((1, LANES), lambda m: (0, 0))],
        out_specs=pl.BlockSpec((TM, LANES), lambda m: (m, 0)),
        out_shape=jax.ShapeDtypeStruct((n, LANES), F32),
        compiler_params=_cparams(("parallel",)),
        name="gdn_gates",
    )(ab, pad(a_log), pad(dt_bias))


def _gdn_feat_kernel(x_ref, prev_ref, next_ref, w_ref, o_ref, xs_ref, *, nct, n_tiles, n_norm, n_q, q_scale, width):
    m = pl.program_id(0)
    j = pl.program_id(1)
    tm, tn = x_ref.shape
    pad = width // 2
    prev_ok = jnp.logical_and(m != 0, m != nct)
    next_ok = jnp.logical_and(m != nct - 1, m != n_tiles - 1)
    xs_ref[0:HALO, :] = jnp.where(prev_ok, prev_ref[...].astype(F32), 0.0)
    xs_ref[HALO:HALO + tm, :] = x_ref[...].astype(F32)
    xs_ref[HALO + tm:, :] = jnp.where(next_ok, next_ref[...].astype(F32), 0.0)
    acc = jnp.zeros((tm, tn), F32)
    for t in range(width):
        acc = acc + w_ref[t:t + 1, :] * xs_ref[HALO - pad + t:HALO - pad + t + tm, :]
    y = _silu(acc)

    @pl.when(j >= n_norm)
    def _():
        o_ref[...] = y.astype(o_ref.dtype)

    @pl.when(j < n_norm)
    def _():
        scale = jnp.where(j < n_q, q_scale, 1.0).astype(F32)
        for hh in range(tn // GDN_HEAD_DIM):
            sl = slice(hh * GDN_HEAD_DIM, (hh + 1) * GDN_HEAD_DIM)
            yh = y[:, sl]
            inv = lax.rsqrt(jnp.sum(yh * yh, axis=-1, keepdims=True) + L2_EPS)
            o_ref[:, sl] = (yh * (inv * scale)).astype(o_ref.dtype)


def _gdn_features(p, conv_w, nct, qk_dim):
    n = p.shape[0]
    width, conv_dim = conv_w.shape
    tn = 512
    n_tiles = n // TM
    rb = TM // HALO
    last_rb = n // HALO - 1
    return pl.pallas_call(
        functools.partial(_gdn_feat_kernel, nct=nct, n_tiles=n_tiles, n_norm=2 * qk_dim // tn, n_q=qk_dim // tn,
                          q_scale=GDN_HEAD_DIM ** -0.5, width=width),
        grid=(n_tiles, conv_dim // tn),
        in_specs=[pl.BlockSpec((TM, tn), lambda m, j: (m, j)),
                  pl.BlockSpec((HALO, tn), lambda m, j: (jnp.maximum(m * rb - 1, 0), j)),
                  pl.BlockSpec((HALO, tn), lambda m, j: (jnp.minimum((m + 1) * rb, last_rb), j)),
                  pl.BlockSpec((width, tn), lambda m, j: (0, j))],
        out_specs=pl.BlockSpec((TM, tn), lambda m, j: (m, j)),
        out_shape=jax.ShapeDtypeStruct((n, conv_dim), BF16),
        scratch_shapes=[pltpu.VMEM((TM + 2 * HALO, tn), F32)],
        compiler_params=_cparams(("parallel", "parallel")),
        name="gdn_features",
    )(p, p, p, conv_w)


def _gdn_scan_kernel(*refs, chunk, groups):
    ins, o_refs, s_ref = refs[:16], refs[16:18], refs[18]
    t = pl.program_id(1)

    @pl.when(t == 0)
    def _():
        s_ref[...] = jnp.zeros_like(s_ref)

    rows = ins[0].shape[0]
    hd = GDN_HEAD_DIM
    n_chunk = rows // chunk
    n_double = int(math.log2(chunk)) - 1
    ri = lax.broadcasted_iota(jnp.int32, (rows, rows), 0)
    ci = lax.broadcasted_iota(jnp.int32, (rows, rows), 1)
    same = (ri // chunk) == (ci // chunk)
    eye = jnp.where(ri == ci, 1.0, 0.0).astype(F32)
    r64 = lax.broadcasted_iota(jnp.int32, (chunk, chunk), 0)
    c64 = lax.broadcasted_iota(jnp.int32, (chunk, chunk), 1)

    dirs = []
    for d in range(2):
        q_ref, k_ref, kt_ref, v_ref, scol_ref, srow_ref, srowc_ref, glrow_ref = ins[8 * d:8 * d + 8]
        for g in range(groups):
            k = k_ref[:, g * hd:(g + 1) * hd]
            dirs.append(dict(
                q=q_ref[:, g * hd:(g + 1) * hd], k=k, v=v_ref[:, 2 * g * hd:2 * (g + 1) * hd],
                kt=kt_ref[g * hd:(g + 1) * hd, :], srowc_ref=srowc_ref, glrow_ref=glrow_ref, g=g, o_ref=o_refs[d],
                scol=scol_ref[0, g],
                srow=srow_ref[0, g],
                incl=same & ((ri <= ci) if d else (ri >= ci)), strict=same & ((ri < ci) if d else (ri > ci)),
                incl64=(r64 <= c64) if d else (r64 >= c64),
                kk=_dot_nt(k, k),
                order=list(range(n_chunk - 1, -1, -1)) if d else list(range(n_chunk))))
    n_streams = len(dirs)
    chains = [(d, a) for d in range(n_streams) for a in range(2)]
    gcol = {c: dirs[c[0]]["scol"][:, c[1]:c[1] + 1] for c in chains}
    bcol = {c: dirs[c[0]]["scol"][:, 2 + c[1]:3 + c[1]] for c in chains}
    e_g = {c: jnp.exp(gcol[c]) for c in chains}
    e_k = {c: jnp.exp(dirs[c[0]]["scol"][:, 4 + c[1]:5 + c[1]] - gcol[c]) for c in chains}
    neg_l = {c: jnp.where(dirs[c[0]]["strict"],
                          -(bcol[c] * dirs[c[0]]["kk"]
                            * jnp.exp(jnp.where(dirs[c[0]]["incl"], gcol[c] - dirs[c[0]]["srow"][c[1]:c[1] + 1, :],
                                                -jnp.inf))), 0.0) for c in chains}
    inv = {c: eye + neg_l[c] for c in chains}
    pw = {c: neg_l[c].astype(BF16) for c in chains}
    for _ in range(n_double):
        pw = {c: _dot(pw[c], pw[c]).astype(BF16) for c in chains}
        inv = {c: inv[c] + _dot(inv[c].astype(BF16), pw[c]) for c in chains}
    rhs = {c: jnp.concatenate([dirs[c[0]]["v"][:, c[1] * hd:(c[1] + 1) * hd].astype(F32) * bcol[c],
                               dirs[c[0]]["k"].astype(F32) * (bcol[c] * e_g[c])], axis=1).astype(BF16) for c in chains}
    uw = {c: _dot(inv[c].astype(BF16), rhs[c]) for c in chains}

    state = [s_ref[d] for d in range(n_streams)]
    for i in range(n_chunk):
        js = [dirs[d]["order"][i] for d in range(n_streams)]
        rs = [slice(j * chunk, (j + 1) * chunk) for j in js]
        s_b = [st.astype(BF16) for st in state]
        qk_c = [_dot_nt(dirs[d]["q"][rs[d]], dirs[d]["k"][rs[d]]) for d in range(n_streams)]
        q_s = [_dot(dirs[d]["q"][rs[d]], s_b[d]) for d in range(n_streams)]
        v_new = {(d, a): uw[d, a][rs[d], :hd] - _dot(uw[d, a][rs[d], hd:].astype(BF16), s_b[d][:, a * hd:(a + 1) * hd])
                 for d, a in chains}
        qkd = {(d, a): (qk_c[d] * jnp.exp(jnp.where(dirs[d]["incl64"],
                                                    gcol[d, a][rs[d]]
                                                    - dirs[d]["srowc_ref"][0, dirs[d]["g"], js[d], a:a + 1, :],
                                                    -jnp.inf))).astype(BF16) for d, a in chains}
        for d, a in chains:
            o = e_g[d, a][rs[d]] * q_s[d][:, a * hd:(a + 1) * hd] + _dot(qkd[d, a], v_new[d, a].astype(BF16))
            col = (2 * dirs[d]["g"] + a) * hd
            dirs[d]["o_ref"][rs[d], col:col + hd] = o.astype(dirs[d]["o_ref"].dtype)
        for d in range(n_streams):
            v_s = jnp.concatenate([v_new[d, a] * e_k[d, a][rs[d]] for a in range(2)], axis=1).astype(BF16)
            parts = [jnp.zeros((js[d] * chunk, 2 * hd), BF16), v_s,
                     jnp.zeros(((n_chunk - 1 - js[d]) * chunk, 2 * hd), BF16)]
            v_pad = jnp.concatenate([p for p in parts if p.shape[0]], axis=0)
            e_l = jnp.exp(dirs[d]["glrow_ref"][0, dirs[d]["g"], js[d]])
            state[d] = state[d] * e_l + _dot(dirs[d]["kt"], v_pad)
    for d in range(n_streams):
        s_ref[d] = state[d]


def _transpose_kernel(x_ref, o_ref):
    o_ref[...] = x_ref[...].astype(F32).T.astype(o_ref.dtype)


def _transposed_keys(feat, qk_dim):
    n = feat.shape[0]
    tn = 512
    return pl.pallas_call(
        _transpose_kernel,
        grid=(n // TM, qk_dim // tn),
        in_specs=[pl.BlockSpec((TM, tn), lambda m, j: (m, qk_dim // tn + j))],
        out_specs=pl.BlockSpec((tn, TM), lambda m, j: (j, m)),
        out_shape=jax.ShapeDtypeStruct((qk_dim, n), feat.dtype),
        compiler_params=_cparams(("parallel", "parallel")),
        name="gdn_keys_t",
    )(feat)


def _gdn_scan(feat, kt, scol, srow, srowc, glrow, nct):
    n = feat.shape[0]
    hd = GDN_HEAD_DIM
    n_qk = kt.shape[0] // hd
    n_tiles = n // TM
    cpt = TM // GDN_CHUNK

    def tile(d, t):
        return jnp.where(t < nct, nct - 1 - t, n_tiles - 1 - (t - nct)) if d else t

    g = GDN_PAIRS_PER_STEP
    n_grp = n_qk // g
    assert n_grp * g == n_qk

    def specs(d):
        return [pl.BlockSpec((TM, g * hd), lambda h, t: (tile(d, t), h)),
                pl.BlockSpec((TM, g * hd), lambda h, t: (tile(d, t), n_grp + h)),
                pl.BlockSpec((g * hd, TM), lambda h, t: (h, tile(d, t))),
                pl.BlockSpec((TM, 2 * g * hd), lambda h, t: (tile(d, t), n_grp + h)),
                pl.BlockSpec((1, g, TM, 6), lambda h, t: (d, h, tile(d, t), 0)),
                pl.BlockSpec((1, g, 4, TM), lambda h, t: (d, h, 0, tile(d, t))),
                pl.BlockSpec((1, g, cpt, 4, GDN_CHUNK), lambda h, t: (d, h, tile(d, t), 0, 0)),
                pl.BlockSpec((1, g, cpt, 1, 2 * hd), lambda h, t: (d, h, tile(d, t), 0, 0))]

    args = [feat, feat, kt, feat, scol, srow, srowc, glrow]
    out = jax.ShapeDtypeStruct((n, 2 * n_qk * hd), BF16)
    return pl.pallas_call(
        functools.partial(_gdn_scan_kernel, chunk=GDN_CHUNK, groups=g),
        grid=(n_grp, n_tiles),
        in_specs=specs(0) + specs(1),
        out_specs=[pl.BlockSpec((TM, 2 * g * hd), lambda h, t: (tile(0, t), h)),
                   pl.BlockSpec((TM, 2 * g * hd), lambda h, t: (tile(1, t), h))],
        out_shape=[out, out],
        scratch_shapes=[pltpu.VMEM((2 * g, hd, 2 * hd), F32)],
        compiler_params=_cparams(("parallel", "arbitrary")),
        name="gdn_scan",
    )(*args, *args)


def _post_mixer(y, h_ref, mod_ref, lnw_ref, lnb_ref, rw_ref, rb_ref, h1_ref, v_ref, rt_ref, cnt_ref, carry_ref,
                alpha, n_e):
    @pl.when(pl.program_id(0) == 0)
    def _():
        carry_ref[...] = jnp.zeros_like(carry_ref)

    x = alpha * h_ref[...] + mod_ref[0, 2:3, :] * y
    h1 = _layer_norm(x, lnw_ref[...], lnb_ref[...])
    h1_ref[...] = h1
    v = _modulate(h1, mod_ref, 3, 4)
    v_ref[...] = _pack_bf16_pairs(v)
    logits = jnp.dot(v, rw_ref[...], preferred_element_type=F32, precision=HIGHEST) + rb_ref[...]

    tm = logits.shape[0]
    lane = lax.broadcasted_iota(jnp.int32, logits.shape, 1)
    rest = jnp.where(lane < n_e, logits, -jnp.inf)
    top_val, top_idx, picked = [], [], []
    for _ in range(TOP_K):
        mx = jnp.max(rest, axis=-1, keepdims=True)
        idx = jnp.min(jnp.where(rest == mx, lane, LANES), axis=-1, keepdims=True)
        hit = lane == idx
        top_val.append(mx)
        top_idx.append(idx)
        picked.append(hit)
        rest = jnp.where(hit, -jnp.inf, rest)
    e = [jnp.exp(tv - top_val[0]) for tv in top_val]
    denom = functools.reduce(lambda a, b: a + b, e)
    onehot = functools.reduce(lambda a, b: a + b, [jnp.where(hit, 1.0, 0.0) for hit in picked])
    r = lax.broadcasted_iota(jnp.int32, (tm, tm), 0)
    c = lax.broadcasted_iota(jnp.int32, (tm, tm), 1)
    earlier = jnp.where(r > c, 1.0, 0.0).astype(BF16)
    before = _dot(earlier, onehot.astype(BF16)) + carry_ref[0:1, :]
    rt = jnp.zeros(logits.shape, F32)
    for k in range(TOP_K):
        rank = jnp.sum(jnp.where(picked[k], before, 0.0), axis=-1, keepdims=True)
        rt = jnp.where(lane == k, top_idx[k].astype(F32), rt)
        rt = jnp.where(lane == TOP_K + k, rank, rt)
        rt = jnp.where(lane == 2 * TOP_K + k, e[k] / denom, rt)
    rt_ref[...] = rt
    total = carry_ref[0:1, :] + jnp.sum(onehot, axis=0, keepdims=True)
    carry_ref[...] = jnp.broadcast_to(total, carry_ref.shape)
    cnt_ref[...] = jnp.broadcast_to(total, cnt_ref.shape)


def _gdn_out_kernel(of_ref, ob_ref, z_ref, nw_ref, w_ref, h_ref, mod_ref, lnw_ref, lnb_ref, rw_ref, rb_ref,
                    h1_ref, v_ref, rt_ref, cnt_ref, carry_ref, a_ref, *, alpha, n_e):
    hd = GDN_HEAD_DIM
    for hh in range(of_ref.shape[1] // hd):
        sl = slice(hh * hd, (hh + 1) * hd)
        o = of_ref[:, sl].astype(F32) + ob_ref[:, sl].astype(F32)
        o = o * lax.rsqrt(jnp.mean(o * o, axis=-1, keepdims=True) + NORM_EPS) * nw_ref[...]
        a_ref[:, sl] = (o * _silu(z_ref[:, sl].astype(F32))).astype(BF16)
    y = _dot(a_ref[...], w_ref[...])
    _post_mixer(y, h_ref, mod_ref, lnw_ref, lnb_ref, rw_ref, rb_ref, h1_ref, v_ref, rt_ref, cnt_ref, carry_ref,
                alpha, n_e)


def _attn_out_kernel(a_ref, w_ref, h_ref, mod_ref, lnw_ref, lnb_ref, rw_ref, rb_ref, h1_ref, v_ref, rt_ref, cnt_ref,
                     carry_ref, *, alpha, n_e):
    y = _dot(a_ref[...], w_ref[...])
    _post_mixer(y, h_ref, mod_ref, lnw_ref, lnb_ref, rw_ref, rb_ref, h1_ref, v_ref, rt_ref, cnt_ref, carry_ref,
                alpha, n_e)


def _mixer_out(kind, acts, w_out, h, mod, nct, ln_w, ln_b, router_w, router_b, alpha):
    n, d = h.shape
    row = lambda width: pl.BlockSpec((TM, width), lambda m: (m, 0))
    full = lambda arr: pl.BlockSpec(arr.shape, lambda m: (0,) * arr.ndim)
    n_e = router_w.shape[1]
    rw = jnp.zeros((d, LANES), F32).at[:, :n_e].set(router_w)
    rb = jnp.zeros((1, LANES), F32).at[0, :n_e].set(router_b)
    tail = [w_out, h, mod, ln_w.reshape(1, d), ln_b.reshape(1, d), rw, rb]
    tail_specs = [full(w_out), row(d), pl.BlockSpec((1, 6, d), lambda m: (jnp.where(m < nct, 0, 1), 0, 0)),
                  pl.BlockSpec((1, d), lambda m: (0, 0)), pl.BlockSpec((1, d), lambda m: (0, 0)), full(rw), full(rb)]
    if kind == "gdn":
        o_f, o_b, p, norm_w, z_col = acts
        v_dim = o_f.shape[1]
        body = functools.partial(_gdn_out_kernel, alpha=alpha, n_e=n_e)
        args = [o_f, o_b, p, norm_w.reshape(1, -1)] + tail
        specs = [row(v_dim), row(v_dim), pl.BlockSpec((TM, v_dim), lambda m: (m, z_col)),
                 pl.BlockSpec((1, norm_w.shape[0]), lambda m: (0, 0))] + tail_specs
        scratch = [pltpu.VMEM((SUBLANES, LANES), F32), pltpu.VMEM((TM, v_dim), BF16)]
    else:
        (a,) = acts
        body = functools.partial(_attn_out_kernel, alpha=alpha, n_e=n_e)
        args = [a] + tail
        specs = [row(a.shape[1])] + tail_specs
        scratch = [pltpu.VMEM((SUBLANES, LANES), F32)]
    return pl.pallas_call(
        body,
        grid=(n // TM,),
        in_specs=specs,
        out_specs=[row(d), row(d // 2), row(LANES), pl.BlockSpec((SUBLANES, LANES), lambda m: (0, 0))],
        out_shape=[jax.ShapeDtypeStruct((n, d), F32), jax.ShapeDtypeStruct((n, d // 2), jnp.uint32),
                   jax.ShapeDtypeStruct((n, LANES), F32), jax.ShapeDtypeStruct((SUBLANES, LANES), F32)],
        scratch_shapes=scratch,
        compiler_params=_cparams(("arbitrary",)),
        name=kind + "_out_ln",
    )(*args)


def _attn_kernel(q_ref, k_ref, vt_ref, lam_ref, sw_ref, o_ref, acc_ref, m_ref, l_ref, sa_ref, sb_ref,
                 *, tk, nct, l_ctx, n_pairs, lambda_init):
    mt = pl.program_id(1)
    q = q_ref[...]
    lane = lax.broadcasted_iota(jnp.int32, q.shape, 1)
    zero = jnp.zeros_like(q)
    q_maps = (jnp.where(lane < DIFF_HEAD_DIM, q, zero), jnp.where(lane >= DIFF_HEAD_DIM, q, zero))
    kb = vt_ref.shape[3]
    acc_ref[...] = jnp.zeros_like(acc_ref)
    m_ref[...] = jnp.full_like(m_ref, -jnp.inf)
    l_ref[...] = jnp.zeros_like(l_ref)

    def scores(start, s_ref):
        kj = k_ref[pl.ds(start, tk), :]
        for i, qm in enumerate(q_maps):
            s_ref[i] = _dot_nt(kj, qm)

    def absorb(score_of_map, block0, n_blocks):
        for i in range(2):
            s = score_of_map(i)
            m_old = m_ref[i]
            m_new = jnp.maximum(m_old, jnp.max(s, axis=0, keepdims=True))
            alpha = jnp.exp2(m_old - m_new)
            p = jnp.exp2(s - m_new)
            l_ref[i] = alpha * l_ref[i] + jnp.sum(p, axis=0, keepdims=True)
            p = p.astype(BF16)
            pv = _dot(vt_ref[0, block0], p[0:kb])
            for b in range(1, n_blocks):
                pv = pv + _dot(vt_ref[0, block0 + b], p[b * kb:(b + 1) * kb])
            acc_ref[i] = alpha * acc_ref[i] + pv
            m_ref[i] = m_new

    def absorb_context():
        k_ctx = k_ref[0:l_ctx, :]
        absorb(lambda i: _dot_nt(k_ctx, q_maps[i]), 0, l_ctx // kb)

    @pl.when(mt < nct)
    def _():
        absorb_context()

    @pl.when(mt >= nct)
    def _():
        last = l_ctx + (2 * n_pairs - 1) * tk
        scores(l_ctx, sa_ref)
        absorb_context()

        def body(jj, carry):
            c0 = pl.multiple_of(l_ctx + 2 * jj * tk, LANES)
            c1 = pl.multiple_of(c0 + tk, LANES)
            c2 = pl.multiple_of(jnp.minimum(c1 + tk, last), LANES)
            scores(c1, sb_ref)
            absorb(lambda i: sa_ref[i], c0 // kb, tk // kb)
            scores(c2, sa_ref)
            absorb(lambda i: sb_ref[i], c1 // kb, tk // kb)
            return carry

        lax.fori_loop(0, n_pairs, body, 0)

    lam = lam_ref[...]
    lam_full = (jnp.exp(jnp.sum(lam[0:1] * lam[1:2], axis=-1, keepdims=True))
                - jnp.exp(jnp.sum(lam[2:3] * lam[3:4], axis=-1, keepdims=True)) + lambda_init)
    o = acc_ref[0] / l_ref[0] - lam_full * (acc_ref[1] / l_ref[1])
    o = o * lax.rsqrt(jnp.mean(o * o, axis=0, keepdims=True) + NORM_EPS) * sw_ref[...]
    o_ref[...] = (o * (1.0 - lambda_init)).T.astype(o_ref.dtype)


def _transpose_heads_kernel(x_ref, o_ref):
    hd = o_ref.shape[2]
    for hh in range(o_ref.shape[0]):
        o_ref[hh, 0] = x_ref[:, hh * hd:(hh + 1) * hd].astype(F32).T.astype(o_ref.dtype)


def _transposed_values(qkv, vd):
    n = qkv.shape[0]
    d = qkv.shape[1] // 3
    tn = 512
    return pl.pallas_call(
        _transpose_heads_kernel,
        grid=(n // TM, d // tn),
        in_specs=[pl.BlockSpec((TM, tn), lambda m, j: (m, 2 * d // tn + j))],
        out_specs=pl.BlockSpec((tn // vd, 1, vd, TM), lambda m, j: (j, m, 0, 0)),
        out_shape=jax.ShapeDtypeStruct((d // vd, n // TM, vd, TM), qkv.dtype),
        compiler_params=_cparams(("parallel", "parallel")),
        name="attn_values_t",
    )(qkv)


def _diff_attention(qkv, lam, subln_w, nct, l_ctx, lambda_init):
    n = qkv.shape[0]
    d = qkv.shape[1] // 3
    vd = subln_w.shape[0]
    n_heads = d // vd
    l_lat = n - l_ctx
    tk = math.gcd(l_lat // 2, ATTN_TK)
    assert tk % TM == 0
    vt = _transposed_values(qkv, vd)
    return pl.pallas_call(
        functools.partial(_attn_kernel, tk=tk, nct=nct, l_ctx=l_ctx, n_pairs=l_lat // (2 * tk),
                          lambda_init=lambda_init),
        grid=(n_heads, n // TM),
        in_specs=[pl.BlockSpec((TM, vd), lambda h, m: (m, h)),
                  pl.BlockSpec((n, vd), lambda h, m: (0, n_heads + h)),
                  pl.BlockSpec((1, n // TM, vd, TM), lambda h, m: (h, 0, 0, 0)),
                  pl.BlockSpec(lam.shape, lambda h, m: (0, 0)),
                  pl.BlockSpec((vd, 1), lambda h, m: (0, 0))],
        out_specs=pl.BlockSpec((TM, vd), lambda h, m: (m, h)),
        out_shape=jax.ShapeDtypeStruct((n, d), BF16),
        scratch_shapes=[pltpu.VMEM((2, vd, TM), F32), pltpu.VMEM((2, 1, TM), F32), pltpu.VMEM((2, 1, TM), F32),
                        pltpu.VMEM((2, tk, TM), F32), pltpu.VMEM((2, tk, TM), F32)],
        compiler_params=_cparams(("parallel", "parallel")),
        name="diff_attn",
    )(qkv, qkv, vt, lam, subln_w.reshape(vd, 1))


def _moe_kernel(be_ref, nb_ref, bv_ref, x_ref, wgu_ref, bgu_ref, wdn_ref, bdn_ref, y_ref, wgu_s, wdn_s, *, d_ff):
    b = pl.program_id(0)
    changed = jnp.logical_or(b == 0, be_ref[b] != be_ref[jnp.maximum(b - 1, 0)])

    @pl.when(changed)
    def _():
        wgu_s[...] = wgu_ref[0, 0].astype(BF16)
        wdn_s[...] = wdn_ref[0, 0].astype(BF16)

    @pl.when(b < nb_ref[0])
    def _():
        x = _unpack_bf16_pairs(x_ref[...])
        row = lax.broadcasted_iota(jnp.int32, x.shape, 0)
        x = jnp.where(row < bv_ref[b], x, jnp.zeros_like(x))
        h = _dot(x, wgu_s[...]) + bgu_ref[0, 0]
        gate = jnp.minimum(h[:, :d_ff], SWIGLU_LIMIT)
        up = jnp.clip(h[:, d_ff:], -SWIGLU_LIMIT, SWIGLU_LIMIT)
        glu = gate * jax.nn.sigmoid(gate * SWIGLU_ALPHA)
        act = ((up + 1.0) * glu).astype(BF16)
        y_ref[...] = (_dot(act, wdn_s[...]) + bdn_ref[0, 0]).astype(y_ref.dtype)

    @pl.when(b >= nb_ref[0])
    def _():
        y_ref[...] = jnp.zeros_like(y_ref)


def _moe_experts(x_sorted, block_e, n_used, block_valid, layer, w_gu, b_gu, w_dn, b_dn):
    n_rows = x_sorted.shape[0]
    depth, n_e, d, f2 = w_gu.shape
    grid_spec = pltpu.PrefetchScalarGridSpec(
        num_scalar_prefetch=3,
        grid=(n_rows // MOE_BM,),
        in_specs=[pl.BlockSpec((MOE_BM, d // 2), lambda b, be, nb, bv: (b, 0)),
                  pl.BlockSpec((1, 1, d, f2), lambda b, be, nb, bv: (layer, be[b], 0, 0)),
                  pl.BlockSpec((1, 1, 1, f2), lambda b, be, nb, bv: (layer, be[b], 0, 0)),
                  pl.BlockSpec((1, 1, f2 // 2, d), lambda b, be, nb, bv: (layer, be[b], 0, 0)),
                  pl.BlockSpec((1, 1, 1, d), lambda b, be, nb, bv: (layer, be[b], 0, 0))],
        out_specs=pl.BlockSpec((MOE_BM, d), lambda b, be, nb, bv: (b, 0)),
        scratch_shapes=[pltpu.VMEM((d, f2), BF16), pltpu.VMEM((f2 // 2, d), BF16)],
    )
    return pl.pallas_call(
        functools.partial(_moe_kernel, d_ff=f2 // 2),
        grid_spec=grid_spec,
        out_shape=jax.ShapeDtypeStruct((n_rows, d), BF16),
        compiler_params=_cparams(("arbitrary",)),
        name="moe_experts",
    )(block_e, n_used, block_valid, x_sorted, w_gu, b_gu.reshape(depth, n_e, 1, f2), w_dn, b_dn.reshape(depth, n_e, 1, d))


def _moe_combine_kernel(y0_ref, y1_ref, y2_ref, y3_ref, g_ref, h_ref, mod_ref, lnw_ref, lnb_ref, o_ref, *, alpha):
    g = g_ref[...]
    f = jnp.zeros(h_ref.shape, F32)
    for i, y_ref in enumerate((y0_ref, y1_ref, y2_ref, y3_ref)):
        f = f + g[:, i:i + 1] * y_ref[...].astype(F32)
    x = alpha * h_ref[...] + mod_ref[0, 5:6, :] * f
    o_ref[...] = _layer_norm(x, lnw_ref[...], lnb_ref[...])


def _moe_combine(ys, gates, h1, mod, nct, ln_w, ln_b, alpha):
    n, d = h1.shape
    row = lambda width: pl.BlockSpec((TM, width), lambda m: (m, 0))
    return pl.pallas_call(
        functools.partial(_moe_combine_kernel, alpha=alpha),
        grid=(n // TM,),
        in_specs=[row(d)] * TOP_K + [row(TOP_K), row(d),
                                     pl.BlockSpec((1, 6, d), lambda m: (jnp.where(m < nct, 0, 1), 0, 0)),
                                     pl.BlockSpec((1, d), lambda m: (0, 0)), pl.BlockSpec((1, d), lambda m: (0, 0))],
        out_specs=row(d),
        out_shape=jax.ShapeDtypeStruct((n, d), F32),
        compiler_params=_cparams(("parallel",)),
        name="moe_combine_ln",
    )(*ys, gates, h1, mod, ln_w.reshape(1, d), ln_b.reshape(1, d))


def _sc_gather_rows(table, idx):
    n_rows = idx.shape[0]
    d = table.shape[1]
    n_workers = SC_CORES * SC_SUBCORES
    per_worker = n_rows // n_workers
    step = SC_GATHER_ROWS
    assert per_worker * n_workers == n_rows and per_worker % (2 * step) == 0
    mesh = plsc.VectorSubcoreMesh(core_axis_name="c", subcore_axis_name="s",
                                  num_cores=SC_CORES, num_subcores=SC_SUBCORES)

    def body(table_hbm, idx_hbm, out_hbm, idx_a, idx_b, rows_a, rows_b, sem_a, sem_b):
        base = (lax.axis_index("s") * SC_CORES + lax.axis_index("c")) * per_worker

        @pl.loop(0, per_worker // (2 * step))
        def _(i):
            off_a = pl.multiple_of(base + 2 * i * step, SUBLANES)
            off_b = pl.multiple_of(off_a + step, SUBLANES)
            pltpu.sync_copy(idx_hbm.at[pl.ds(off_a, step)], idx_a)
            pltpu.sync_copy(idx_hbm.at[pl.ds(off_b, step)], idx_b)
            copy_a = pltpu.async_copy(table_hbm.at[idx_a], rows_a, sem_a)
            copy_b = pltpu.async_copy(table_hbm.at[idx_b], rows_b, sem_b)
            copy_a.wait()
            pltpu.sync_copy(rows_a, out_hbm.at[pl.ds(off_a, step)])
            copy_b.wait()
            pltpu.sync_copy(rows_b, out_hbm.at[pl.ds(off_b, step)])

    return pl.kernel(
        body,
        out_type=jax.ShapeDtypeStruct((n_rows, d), table.dtype),
        mesh=mesh,
        scratch_types=[pltpu.VMEM((step,), jnp.int32), pltpu.VMEM((step,), jnp.int32),
                       pltpu.VMEM((step, d), table.dtype), pltpu.VMEM((step, d), table.dtype),
                       pltpu.SemaphoreType.DMA, pltpu.SemaphoreType.DMA],
        name="sc_gather_rows",
    )(table, idx)


def _sc_scatter_rows(rows, dest_t, n_out):
    n, d = rows.shape
    n_slots = dest_t.shape[0] // n
    n_workers = SC_CORES * SC_SUBCORES
    per_worker = n // n_workers
    assert per_worker * n_workers == n
    step = max(s for s in range(SUBLANES, 2 * SC_GATHER_ROWS + 1, SUBLANES) if per_worker % s == 0)
    mesh = plsc.VectorSubcoreMesh(core_axis_name="c", subcore_axis_name="s",
                                  num_cores=SC_CORES, num_subcores=SC_SUBCORES)

    def body(rows_hbm, dest_hbm, out_hbm, idx_v, rows_v):
        base = (lax.axis_index("s") * SC_CORES + lax.axis_index("c")) * per_worker

        @pl.loop(0, per_worker // step)
        def _(i):
            off = pl.multiple_of(base + i * step, SUBLANES)
            pltpu.sync_copy(rows_hbm.at[pl.ds(off, step)], rows_v)
            for k in range(n_slots):
                pltpu.sync_copy(dest_hbm.at[pl.ds(pl.multiple_of(k * n + off, SUBLANES), step)], idx_v)
                pltpu.sync_copy(rows_v, out_hbm.at[idx_v])

    return pl.kernel(
        body,
        out_type=jax.ShapeDtypeStruct((n_out, d), rows.dtype),
        mesh=mesh,
        scratch_types=[pltpu.VMEM((step,), jnp.int32), pltpu.VMEM((step, d), rows.dtype)],
        name="sc_scatter_rows",
    )(rows, dest_t)


def _moe(v_rows, route, counts, n_e, h1, mod, nct, ln_w, ln_b, alpha, layer, w_gu, b_gu, w_dn, b_dn):
    n = v_rows.shape[0]
    top_idx = route[:, :TOP_K].astype(jnp.int32)
    rank = route[:, TOP_K:2 * TOP_K].astype(jnp.int32)
    gates = route[:, 2 * TOP_K:3 * TOP_K]
    counts = counts[0, :n_e].astype(jnp.int32)
    nk = n * TOP_K
    padded = (counts + MOE_BM - 1) // MOE_BM * MOE_BM
    pad_end = jnp.cumsum(padded)
    pad_start = pad_end - padded
    dest = (pad_start[top_idx] + rank).reshape(nk)
    n_blocks = -(-nk // MOE_BM) + n_e
    n_rows = n_blocks * MOE_BM
    block_row0 = jnp.arange(n_blocks, dtype=jnp.int32) * MOE_BM
    block_e = jnp.minimum(jnp.sum(pad_end[None, :] <= block_row0[:, None], axis=1, dtype=jnp.int32), n_e - 1)
    block_valid = jnp.clip(counts[block_e] - (block_row0 - pad_start[block_e]), 0, MOE_BM).astype(jnp.int32)
    n_used = (pad_end[-1:] // MOE_BM).astype(jnp.int32)
    x_sorted = _sc_scatter_rows(v_rows, dest.reshape(n, TOP_K).T.reshape(nk), n_rows)
    yb = _moe_experts(x_sorted, block_e, n_used, block_valid, layer, w_gu, b_gu, w_dn, b_dn)
    dest2 = dest.reshape(n, TOP_K)
    ys = [yb[dest2[:, i]] for i in range(TOP_K)]
    return _moe_combine(ys, gates, h1, mod, nct, ln_w, ln_b, alpha)


def _rope_tables(l_ctx, l_lat):
    rows = l_lat // GRID_W
    row = jnp.repeat(jnp.arange(rows, dtype=F32), GRID_W)
    col = jnp.tile(jnp.arange(GRID_W, dtype=F32), rows)
    axis_dim = DIFF_HEAD_DIM // 2
    inv_freq = ROPE_THETA ** (-jnp.arange(0, axis_dim, 2, dtype=F32) / axis_dim)
    ang_r = row[:, None] * inv_freq
    ang_c = col[:, None] * inv_freq
    ang = jnp.concatenate([ang_r, ang_r, ang_c, ang_c], -1)
    ang = jnp.concatenate([jnp.zeros((l_ctx, DIFF_HEAD_DIM), F32), ang], 0)
    ang = jnp.concatenate([ang, ang], -1)
    return jnp.cos(ang), jnp.sin(ang)


def _gdn_scalars(gates, n_vh):
    n = gates.shape[0]
    n_qk = n_vh // 2
    nc = n // GDN_CHUNK
    gc = gates[:, :2 * n_vh].reshape(n, 2, n_qk, 2)
    beta = gates[:, 2 * n_vh:4 * n_vh].reshape(n, 2, n_qk, 2)
    gcc = gc.reshape(nc, GDN_CHUNK, 2, n_qk, 2)
    gl = jnp.stack([gcc[:, -1, 0], gcc[:, 0, 1]], axis=1)
    gl_tok = jnp.repeat(gl, GDN_CHUNK, axis=0)
    col = jnp.concatenate([gc, beta, gl_tok], axis=-1)
    scol = jnp.transpose(col, (1, 2, 0, 3))
    srow = jnp.transpose(col[..., :4], (1, 2, 3, 0))
    srowc = jnp.transpose(col[..., :4].reshape(nc, GDN_CHUNK, 2, n_qk, 4), (2, 3, 0, 4, 1))
    glrow = jnp.broadcast_to(jnp.transpose(gl, (1, 2, 0, 3))[..., None], (2, n_qk, nc, 2, GDN_HEAD_DIM))
    glrow = glrow.reshape(2, n_qk, nc, 1, 2 * GDN_HEAD_DIM)
    return scol, srow, srowc, glrow


def kernel(x, c, ctx, c_ctx, ada_w, ada_b, ln_w, ln_b, gdn_w_in, gdn_conv_w, gdn_a_log, gdn_dt_bias, gdn_norm_w,
           gdn_w_out, diff_w_in, diff_lambda, diff_subln_w, diff_w_out, router_w, router_b, moe_w_gate_up,
           moe_b_gate_up, moe_w_down, moe_b_down):
    batch, l_lat, d = x.shape
    l_ctx = ctx.shape[1]
    depth = ada_w.shape[0]
    assert batch == 1 and l_ctx % TM == 0 and l_lat % TM == 0 and l_lat % GRID_W == 0
    nct = l_ctx // TM
    n_e = router_w.shape[-1]
    alpha = (2 * depth) ** 0.25
    n_vh = gdn_a_log.shape[-1]
    v_dim = n_vh * GDN_HEAD_DIM
    conv_dim = gdn_conv_w.shape[-1]
    qk_dim = (conv_dim - v_dim) // 2
    n_qk = qk_dim // GDN_HEAD_DIM

    h = jnp.concatenate([ctx[0], x[0]], axis=0)
    n = h.shape[0]
    mods = _modulations(c_ctx, c, ada_w, ada_b)
    cos, sin = _rope_tables(l_ctx, l_lat)

    for i in range(depth):
        j = i // 2
        mod = mods[i]
        if i % 2 == 0:
            w_in = gdn_w_in[j]
            w_main = w_in[:, :conv_dim + v_dim].astype(BF16)
            w_ab = jnp.zeros((d, LANES), F32).at[:, :4 * n_vh].set(w_in[:, conv_dim + v_dim:]).astype(BF16)
            p, ab = _project(h, mod, nct, [w_main, w_ab], [BF16, F32])
            gates = _gdn_gates(ab, gdn_a_log[j], gdn_dt_bias[j])
            feat = _gdn_features(p, gdn_conv_w[j], nct, qk_dim)
            kt = _transposed_keys(feat, qk_dim)
            scol, srow, srowc, glrow = _gdn_scalars(gates, n_vh)
            o_f, o_b = _gdn_scan(feat, kt, scol, srow, srowc, glrow, nct)
            acts = (o_f, o_b, p, gdn_norm_w[j], conv_dim // v_dim)
            h1, v, route, counts = _mixer_out("gdn", acts, gdn_w_out[j].astype(BF16), h, mod, nct, ln_w[i, 0], ln_b[i, 0],
                                       router_w[i], router_b[i], alpha)
        else:
            lambda_init = 0.8 - 0.6 * math.exp(-0.3 * i)
            qkv = _qkv_rope(h, mod, nct, diff_w_in[j].astype(BF16), cos, sin)
            a = _diff_attention(qkv, diff_lambda[j], diff_subln_w[j], nct, l_ctx, lambda_init)
            h1, v, route, counts = _mixer_out("attn", (a,), diff_w_out[j].astype(BF16), h, mod, nct, ln_w[i, 0], ln_b[i, 0],
                                       router_w[i], router_b[i], alpha)
        h = _moe(v, route, counts, n_e, h1, mod, nct, ln_w[i, 1], ln_b[i, 1], alpha,
                 i, moe_w_gate_up, moe_b_gate_up, moe_w_down, moe_b_down)
    return h[l_ctx:].reshape(batch, l_lat, d)
```
